```python
import math
import jax
import jax.numpy as jnp
from jax import lax
import numpy as np

D_MODEL = 2048
BATCH = 16
SEQ = 256
DEPTH = 2
DEC_BATCH = 2
DEC_SEQ = 2048
PAST_LEN = 512

GRID_W = 64
A_HEADS = 6
A_HD = 128
NA_ROWS = 8
NA_COLS = 16
B_HEADS = 5
B_QK = 64
B_HD = 128
C_HEADS = 5
C_DK = 128
C_DV = 128
MLSTM_CHUNK = 64
Q_BLOCK = 128
D_FF = 5504
CONV_W = 3
ROPE_BASE = 10000.0
LN_EPS = 1e-5
RMS_EPS = 1e-6
ALPHA = (2 * DEPTH) ** 0.25
BETA = (8 * DEPTH) ** -0.25
A_W = A_HEADS * A_HD
B_QKW = B_HEADS * 2 * B_QK
B_VW = B_HEADS * B_HD
C_QKW = C_HEADS * C_DK
C_VW = C_HEADS * C_DV
N_GATES = 4 * C_HEADS
IN_SPLITS = (A_W, A_W, A_W, B_QKW, B_QKW, B_VW, C_QKW, C_QKW, C_VW, C_VW, N_GATES)
IN_COLS = sum(IN_SPLITS)

kernel_name = 'hybrid_natten_diffattn_mlstm_dit_step'


def _split_cols(p):
    out, o = [], 0
    for w in IN_SPLITS:
        out.append(p[..., o:o + w])
        o += w
    return out


def _heads(t, n):
    b, s, _ = t.shape
    return t.reshape(b, s, n, -1).transpose(0, 2, 1, 3)


def _merge(t):
    b, h, s, d = t.shape
    return t.transpose(0, 2, 1, 3).reshape(b, s, h * d)


def _layernorm(x, g, b):
    xf = x.astype(jnp.float32)
    mu = xf.mean(-1, keepdims=True)
    var = jnp.square(xf - mu).mean(-1, keepdims=True)
    return (xf - mu) * lax.rsqrt(var + LN_EPS) * g.astype(jnp.float32) + b.astype(jnp.float32)


def _rmsnorm(x, g):
    xf = x.astype(jnp.float32)
    return xf * lax.rsqrt(jnp.mean(xf * xf, -1, keepdims=True) + RMS_EPS) * g.astype(jnp.float32)


def _axial_rope_tables(s, dim):
    t = jnp.arange(s)
    rows = (t // GRID_W).astype(jnp.float32)
    cols = (t % GRID_W).astype(jnp.float32)
    half = dim // 2
    freqs = ROPE_BASE ** (-jnp.arange(0, half, 2, dtype=jnp.float32) / half)
    ar = rows[:, None] * freqs
    ac = cols[:, None] * freqs
    return jnp.cos(ar), jnp.sin(ar), jnp.cos(ac), jnp.sin(ac)


def _rotate_half(x, cos, sin):
    x1, x2 = jnp.split(x, 2, axis=-1)
    return jnp.concatenate([x1 * cos - x2 * sin, x1 * sin + x2 * cos], axis=-1)


def _axial_rope(x, tabs):
    cr, sr, cc, sc = tabs
    xr, xc = jnp.split(x.astype(jnp.float32), 2, axis=-1)
    return jnp.concatenate([_rotate_half(xr, cr, sr), _rotate_half(xc, cc, sc)], axis=-1).astype(x.dtype)


def _blockwise_attention(q1_segs, k1_segs, v, scale, q2_segs=None, k2_segs=None, lam=None):
    bq, h, q_len, _ = q1_segs[0].shape
    nb = q_len // Q_BLOCK
    vf = v.astype(jnp.float32)

    def blocks(t):
        return jnp.moveaxis(t.reshape(bq, h, nb, Q_BLOCK, t.shape[-1]), 2, 0)

    def probs(qblk, ks):
        s = jnp.concatenate([jnp.einsum('bhqd,bhkd->bhqk', q, k) for q, k in zip(qblk, ks)], axis=-1)
        return jax.nn.softmax(s.astype(jnp.float32) * scale, axis=-1)

    q1b = [blocks(q) for q in q1_segs]
    if q2_segs is None:
        def one(qb):
            return jnp.einsum('bhqk,bhkd->bhqd', probs(qb, k1_segs), vf)
        out = lax.map(one, q1b)
    else:
        q2b = [blocks(q) for q in q2_segs]

        def one(qb):
            qa, qc = qb
            p = probs(qa, k1_segs) - lam * probs(qc, k2_segs)
            return jnp.einsum('bhqk,bhkd->bhqd', p, vf)
        out = lax.map(one, (q1b, q2b))
    return jnp.moveaxis(out, 0, 2).reshape(bq, h, q_len, -1)


def _neighbourhood_attention(q, k, v, k_ctx, v_ctx, rpb, scale):
    bq, h, s, dh = q.shape
    n_rows = s // GRID_W
    wr = min(NA_ROWS, n_rows)
    qg = q.reshape(bq, h, n_rows, GRID_W, dh)
    kg = k.reshape(bq, h, n_rows, GRID_W, dh)
    vg = v.reshape(bq, h, n_rows, GRID_W, dh)
    cols = np.arange(GRID_W)
    cs = np.clip(cols - NA_COLS // 2, 0, GRID_W - NA_COLS)
    col_mask = (cols[None, :] >= cs[:, None]) & (cols[None, :] < cs[:, None] + NA_COLS)
    col_idx = np.clip(cols[None, :] - cols[:, None] + NA_COLS - 1, 0, 2 * NA_COLS - 2)
    mask = jnp.asarray(np.tile(col_mask[:, None, :], (1, wr, 1)).reshape(GRID_W, wr * GRID_W))
    n_nb = wr * GRID_W
    vcf = v_ctx.astype(jnp.float32)

    def one(r):
        rs = jnp.clip(r - wr // 2, 0, n_rows - wr)
        kb = lax.dynamic_slice_in_dim(kg, rs, wr, axis=2).reshape(bq, h, n_nb, dh)
        vb = lax.dynamic_slice_in_dim(vg, rs, wr, axis=2).reshape(bq, h, n_nb, dh)
        qr = lax.dynamic_index_in_dim(qg, r, axis=2, keepdims=False)
        row_idx = NA_ROWS - 1 + rs + jnp.arange(wr) - r
        bias = rpb[:, row_idx[:, None, None], col_idx[None]]
        bias = bias.transpose(0, 2, 1, 3).reshape(h, GRID_W, n_nb).astype(jnp.float32)
        s_nb = jnp.einsum('bhqd,bhkd->bhqk', qr, kb).astype(jnp.float32) * scale + bias
        s_nb = jnp.where(mask, s_nb, -jnp.inf)
        s_ctx = jnp.einsum('bhqd,bhkd->bhqk', qr, k_ctx).astype(jnp.float32) * scale
        p = jax.nn.softmax(jnp.concatenate([s_nb, s_ctx], axis=-1), axis=-1)
        return (jnp.einsum('bhqk,bhkd->bhqd', p[..., :n_nb], vb.astype(jnp.float32))
                + jnp.einsum('bhqk,bhkd->bhqd', p[..., n_nb:], vcf))

    out = lax.map(one, jnp.arange(n_rows))
    return jnp.moveaxis(out, 0, 2).reshape(bq, h, s, dh)


def _mlstm_scan(q, k, v, i_pre, logf, c0, n0, m0):
    bq, h, s, _ = q.shape
    nc = s // MLSTM_CHUNK
    tril = jnp.asarray(np.tril(np.ones((MLSTM_CHUNK, MLSTM_CHUNK), dtype=bool)))

    def chunks(t):
        t = t.astype(jnp.float32)
        return jnp.moveaxis(t.reshape((bq, h, nc, MLSTM_CHUNK) + t.shape[3:]), 2, 0)

    def step(carry, xs):
        c_st, n_st, m_st = carry
        qc, kc, vc, ic, fc = xs
        b = jnp.cumsum(fc, axis=-1)
        d = jnp.where(tril, b[..., :, None] - b[..., None, :] + ic[..., None, :], -jnp.inf)
        inter = b + m_st[..., None]
        m_row = jnp.maximum(d.max(-1), inter)
        w_intra = jnp.exp(d - m_row[..., None])
        w_state = jnp.exp(inter - m_row)
        sc = jnp.einsum('bhtd,bhsd->bhts', qc, kc) * w_intra
        num = w_state[..., None] * jnp.einsum('bhtd,bhde->bhte', qc, c_st) + jnp.einsum('bhts,bhse->bhte', sc, vc)
        den = w_state * jnp.einsum('bhtd,bhd->bht', qc, n_st) + sc.sum(-1)
        h_out = num / jnp.maximum(jnp.abs(den), jnp.exp(-m_row))[..., None]
        b_last = b[..., -1]
        g = b_last[..., None] - b + ic
        m_new = jnp.maximum(b_last + m_st, g.max(-1))
        w_old = jnp.exp(b_last + m_st - m_new)
        w_s = jnp.exp(g - m_new[..., None])
        c_new = w_old[..., None, None] * c_st + jnp.einsum('bhs,bhsd,bhse->bhde', w_s, kc, vc)
        n_new = w_old[..., None] * n_st + jnp.einsum('bhs,bhsd->bhd', w_s, kc)
        return (c_new, n_new, m_new), h_out

    xs = tuple(chunks(t) for t in (q, k, v, i_pre, logf))
    init = (c0.astype(jnp.float32), n0.astype(jnp.float32), m0.astype(jnp.float32))
    (c_f, n_f, m_f), hs = lax.scan(step, init, xs)
    return jnp.moveaxis(hs, 0, 2).reshape(bq, h, s, -1), c_f, n_f, m_f


def _mlstm_bidir(q, k, v, gates, c0, n0, m0):
    def flip(t):
        return jnp.flip(t, axis=2)
    i_f, f_f = gates[0], jax.nn.log_sigmoid(gates[1])
    i_b, f_b = gates[2], jax.nn.log_sigmoid(gates[3])
    hf, cf, nf, mf = _mlstm_scan(q, k, v, i_f, f_f, c0[:, 0], n0[:, 0], m0[:, 0])
    hb, cb, nb, mb = _mlstm_scan(flip(q), flip(k), flip(v), flip(i_b), flip(f_b), c0[:, 1], n0[:, 1], m0[:, 1])
    return hf + flip(hb), jnp.stack([cf, cb], 1), jnp.stack([nf, nb], 1), jnp.stack([mf, mb], 1)


def _diff_lambda(b_lambda, lam_init):
    lf = b_lambda.astype(jnp.float32)
    return jnp.exp(jnp.sum(lf[0] * lf[1])) - jnp.exp(jnp.sum(lf[2] * lf[3])) + lam_init


def _mixer(h, w_in, c_gate_b, a_rpb, b_lambda, b_subln, c_norm, w_out, lam_init, ctx=None):
    bq, s, _ = h.shape
    qa, ka, va, qb, kb, vb, qc, kc, vc, oc, gc = _split_cols(h @ w_in)
    qa, ka, va = _heads(qa, A_HEADS), _heads(ka, A_HEADS), _heads(va, A_HEADS)
    qb, kb, vb = _heads(qb, B_HEADS), _heads(kb, B_HEADS), _heads(vb, B_HEADS)
    qc, kc, vc = (_heads(t, C_HEADS).astype(jnp.float32) for t in (qc, kc, vc))
    gates = (gc.reshape(bq, s, 4, C_HEADS).astype(jnp.float32) + c_gate_b.astype(jnp.float32)).transpose(2, 0, 3, 1)
    q1, q2 = jnp.split(qb, 2, axis=-1)
    k1, k2 = jnp.split(kb, 2, axis=-1)
    lam = _diff_lambda(b_lambda, lam_init)
    a_scale = A_HD ** -0.5
    b_scale = B_QK ** -0.5
    if ctx is None:
        ya = _blockwise_attention([qa], [ka], va, a_scale)
        yb = _blockwise_attention([q1], [k1], vb, b_scale, [q2], [k2], lam)
        c0 = jnp.zeros((bq, 2, C_HEADS, C_DK, C_DV), jnp.float32)
        n0 = jnp.zeros((bq, 2, C_HEADS, C_DK), jnp.float32)
        m0 = jnp.zeros((bq, 2, C_HEADS), jnp.float32)
    else:
        a_k, a_v, b_k, b_v, c0, n0, m0 = ctx
        ya = _neighbourhood_attention(qa, ka, va, a_k, a_v, a_rpb, a_scale)
        tabs = _axial_rope_tables(s, B_QK)
        bk1, bk2 = jnp.split(b_k, 2, axis=-1)
        v_all = jnp.concatenate([vb, b_v.astype(vb.dtype)], axis=2)
        yb = _blockwise_attention([_axial_rope(q1, tabs), q1], [_axial_rope(k1, tabs), bk1], v_all, b_scale,
                                  [_axial_rope(q2, tabs), q2], [_axial_rope(k2, tabs), bk2], lam)
    hc, c_fin, n_fin, m_fin = _mlstm_bidir(qc * C_DK ** -0.5, kc, vc, gates, c0, n0, m0)
    yb = _rmsnorm(yb, b_subln) * (1.0 - lam_init)
    yc = jax.nn.sigmoid(oc.astype(jnp.float32)) * _merge(_rmsnorm(hc, c_norm))
    y = jnp.concatenate([_merge(ya), _merge(yb), yc], axis=-1).astype(h.dtype) @ w_out
    if ctx is None:
        return y, (ka, va, kb, vb, c_fin, n_fin, m_fin)
    return y, None


def _conv_ffn(h, w_up, conv_w, conv_b, w_down):
    s = h.shape[1]
    u = h @ w_up
    pad = CONV_W // 2
    up = jnp.pad(u, ((0, 0), (pad, pad), (0, 0)))
    u = sum(up[:, j:j + s] * conv_w[j] for j in range(CONV_W)) + conv_b
    a, g = jnp.split(u, 2, axis=-1)
    return (jax.nn.silu(g) * a) @ w_down


def _block(x, cond, lp, lam_init, ctx=None):
    (w_mod, b_mod, w_in, c_gate_b, a_rpb, b_lambda, b_subln, c_norm, w_out,
     ln1_g, ln1_b, ln2_g, ln2_b, w_up, conv_w, conv_b, w_down) = lp
    mods = jax.nn.silu(cond.astype(jnp.float32)) @ w_mod + b_mod
    sh_a, sc_a, g_a, sh_f, sc_f, g_f = [m[:, None, :].astype(x.dtype) for m in jnp.split(mods, 6, axis=-1)]
    y, new_ctx = _mixer(x * (1 + sc_a) + sh_a, w_in, c_gate_b, a_rpb, b_lambda, b_subln, c_norm, w_out,
                        lam_init, ctx)
    x = _layernorm(ALPHA * x + g_a * y, ln1_g, ln1_b).astype(x.dtype)
    f = _conv_ffn(x * (1 + sc_f) + sh_f, w_up, conv_w, conv_b, w_down)
    x = _layernorm(ALPHA * x + g_f * f, ln2_g, ln2_b).astype(x.dtype)
    return x, new_ctx


def setup_inputs(seed: int = 0) -> dict:
    key = jax.random.key(seed)
    ks = jax.random.split(key, 32)

    def nrm(k, shape, s):
        return jax.random.normal(k, shape, jnp.float32) * s

    f_bias = jnp.linspace(3.0, 6.0, C_HEADS)
    gate_base = jnp.array([0.0, 1.0, 0.0, 1.0], jnp.float32)[:, None] * f_bias[None, :]
    return {
        'x_prompt': nrm(ks[0], (BATCH, SEQ, D_MODEL), 1.0),
        'x_sample': nrm(ks[1], (DEC_BATCH, DEC_SEQ, D_MODEL), 1.0),
        'cache_a_k': nrm(ks[2], (DEC_BATCH, DEPTH, A_HEADS, PAST_LEN, A_HD), 1.0),
        'cache_a_v': nrm(ks[3], (DEC_BATCH, DEPTH, A_HEADS, PAST_LEN, A_HD), 1.0),
        'cache_b_k': nrm(ks[4], (DEC_BATCH, DEPTH, B_HEADS, PAST_LEN, 2 * B_QK), 1.0),
        'cache_b_v': nrm(ks[5], (DEC_BATCH, DEPTH, B_HEADS, PAST_LEN, B_HD), 1.0),
        'state_c_C': nrm(ks[6], (DEC_BATCH, DEPTH, 2, C_HEADS, C_DK, C_DV), 1.0),
        'state_c_n': nrm(ks[7], (DEC_BATCH, DEPTH, 2, C_HEADS, C_DK), 1.0),
        'state_c_m': nrm(ks[8], (DEC_BATCH, DEPTH, 2, C_HEADS), 1.0),
        'c': nrm(ks[9], (DEC_BATCH, D_MODEL), 1.0),
        'c_ctx': nrm(ks[10], (D_MODEL,), 1.0),
        'w_mod': nrm(ks[11], (DEPTH, D_MODEL, 6 * D_MODEL), 0.5 * D_MODEL ** -0.5),
        'b_mod': nrm(ks[12], (DEPTH, 6 * D_MODEL), 0.02),
        'w_in': nrm(ks[13], (DEPTH, D_MODEL, IN_COLS), D_MODEL ** -0.5),
        'c_gate_b': nrm(ks[14], (DEPTH, 4, C_HEADS), 0.1) + gate_base,
        'a_rpb': nrm(ks[15], (DEPTH, A_HEADS, 2 * NA_ROWS - 1, 2 * NA_COLS - 1), 0.1),
        'b_lambda': nrm(ks[16], (DEPTH, 4, B_QK), 0.1),
        'b_subln': 1.0 + nrm(ks[17], (DEPTH, B_HD), 0.02),
        'c_norm': 1.0 + nrm(ks[18], (DEPTH, C_DV), 0.02),
        'w_out': nrm(ks[19], (DEPTH, D_MODEL, D_MODEL), BETA * D_MODEL ** -0.5),
        'ln1_g': 1.0 + nrm(ks[20], (DEPTH, D_MODEL), 0.02),
        'ln1_b': nrm(ks[21], (DEPTH, D_MODEL), 0.02),
        'ln2_g': 1.0 + nrm(ks[22], (DEPTH, D_MODEL), 0.02),
        'ln2_b': nrm(ks[23], (DEPTH, D_MODEL), 0.02),
        'w_up': nrm(ks[24], (DEPTH, D_MODEL, 2 * D_FF), D_MODEL ** -0.5),
        'conv_w': nrm(ks[25], (DEPTH, CONV_W, 2 * D_FF), CONV_W ** -0.5),
        'conv_b': nrm(ks[26], (DEPTH, 2 * D_FF), 0.02),
        'w_down': nrm(ks[27], (DEPTH, D_FF, D_MODEL), BETA * D_FF ** -0.5),
    }


def reference(x_prompt, x_sample, cache_a_k, cache_a_v, cache_b_k, cache_b_v, state_c_C, state_c_n, state_c_m,
              c, c_ctx, w_mod, b_mod, w_in, c_gate_b, a_rpb, b_lambda, b_subln, c_norm, w_out,
              ln1_g, ln1_b, ln2_g, ln2_b, w_up, conv_w, conv_b, w_down):
    xp = x_prompt
    xs = x_sample
    ctx_out = []
    for l in range(DEPTH):
        lam_init = 0.8 - 0.6 * math.exp(-0.3 * l)
        lp = (w_mod[l], b_mod[l], w_in[l], c_gate_b[l], a_rpb[l], b_lambda[l], b_subln[l], c_norm[l], w_out[l],
              ln1_g[l], ln1_b[l], ln2_g[l], ln2_b[l], w_up[l], conv_w[l], conv_b[l], w_down[l])
        xp, new_ctx = _block(xp, c_ctx[None, :], lp, lam_init)
        ctx_out.append(new_ctx)
        cached = (cache_a_k[:, l], cache_a_v[:, l], cache_b_k[:, l], cache_b_v[:, l],
                  state_c_C[:, l], state_c_n[:, l], state_c_m[:, l])
        xs, _ = _block(xs, c, lp, lam_init, cached)
    new_a_k = jnp.stack([t[0] for t in ctx_out], axis=1)
    new_a_v = jnp.stack([t[1] for t in ctx_out], axis=1)
    new_b_k = jnp.stack([t[2] for t in ctx_out], axis=1)
    new_b_v = jnp.stack([t[3] for t in ctx_out], axis=1)
    new_c_C = jnp.stack([t[4] for t in ctx_out], axis=1)
    new_c_n = jnp.stack([t[5] for t in ctx_out], axis=1)
    new_c_m = jnp.stack([t[6] for t in ctx_out], axis=1)
    return (xp, xs, new_a_k, new_a_v, new_b_k, new_b_v, new_c_C, new_c_n, new_c_m)
```

```python
import functools
import math

import jax
import jax.numpy as jnp
import numpy as np
from jax import lax
from jax.experimental import pallas as pl
from jax.experimental.pallas import tpu as pltpu

F32 = jnp.float32
BF16 = jnp.bfloat16

DEPTH = 2
GRID_W = 64
A_HEADS = 6
NA_ROWS = 8
NA_COLS = 16
B_HEADS = 5
B_QK = 64
C_HEADS = 5
HEAD_DIM = 128
CONV_W = 3
ROPE_BASE = 10000.0
LN_EPS = 1e-5
RMS_EPS = 1e-6
ALPHA = (2 * DEPTH) ** 0.25

LANE = 128
SUBLANE = 8
BF16_ROWS = 16
VMEM_LIMIT_BYTES = 56 * 1024 * 1024

MLSTM_CHUNK = 256
HI = lax.Precision.HIGHEST


def _tile(n, target, unit=LANE):
    if n <= target:
        return n
    best = unit
    for t in range(unit, target + 1, unit):
        if n % t == 0:
            best = t
    assert n % best == 0, (n, target, unit)
    return best


def _cparams(*sem):
    return pltpu.CompilerParams(dimension_semantics=sem, vmem_limit_bytes=VMEM_LIMIT_BYTES)


def _dot(a, b):
    return jnp.dot(a, b, preferred_element_type=F32)


def _dot_nt(a, b, precision=None):
    return lax.dot_general(a, b, (((1,), (1,)), ((), ())), preferred_element_type=F32, precision=precision)


def _dot_tn(a, b):
    return lax.dot_general(a, b, (((0,), (0,)), ((), ())), preferred_element_type=F32)


def _mods_kernel(cond_ref, w_ref, b_ref, o_ref):
    c = cond_ref[...]
    s = c / (1.0 + jnp.exp(-c))
    o_ref[0] = _dot(s.astype(BF16), w_ref[0].astype(BF16)) + b_ref[0]


def _mods(cond, w_mod, b_mod):
    depth, d, n = w_mod.shape
    tn = _tile(n, 1536)
    return pl.pallas_call(
        _mods_kernel,
        grid=(depth, n // tn),
        in_specs=[pl.BlockSpec((SUBLANE, d), lambda l, j: (0, 0)),
                  pl.BlockSpec((1, d, tn), lambda l, j: (l, 0, j)),
                  pl.BlockSpec((1, 1, tn), lambda l, j: (l, 0, j))],
        out_specs=pl.BlockSpec((1, SUBLANE, tn), lambda l, j: (l, 0, j)),
        out_shape=jax.ShapeDtypeStruct((depth, SUBLANE, n), F32),
        compiler_params=_cparams("arbitrary", "arbitrary"),
        name="mods",
    )(cond, w_mod, b_mod.reshape(depth, 1, n))


def _inproj_kernel(x_ref, mod_ref, w_ref, o_ref, h_scr):
    @pl.when(pl.program_id(1) == 0)
    def _():
        sh = mod_ref[0, 0:1, :]
        sc = mod_ref[0, 1:2, :]
        h_scr[...] = (x_ref[...] * (1.0 + sc) + sh).astype(BF16)

    o_ref[...] = _dot(h_scr[...], w_ref[...])


def _inproj(x, mods_l, w_bf, row_base, rows_per_cond):
    m, d = x.shape
    n = w_bf.shape[1]
    tm = _tile(m, 1024, SUBLANE)
    tm = min(tm, rows_per_cond)
    tn = _tile(n, 1152)
    per = rows_per_cond // tm
    return pl.pallas_call(
        _inproj_kernel,
        grid=(m // tm, n // tn),
        in_specs=[pl.BlockSpec((tm, d), lambda i, j: (i, 0)),
                  pl.BlockSpec((1, 6, d), lambda i, j: (row_base + i // per, 0, 0)),
                  pl.BlockSpec((d, tn), lambda i, j: (0, j))],
        out_specs=pl.BlockSpec((tm, tn), lambda i, j: (i, j)),
        out_shape=jax.ShapeDtypeStruct((m, n), F32),
        scratch_shapes=[pltpu.VMEM((tm, d), BF16)],
        compiler_params=_cparams("arbitrary", "arbitrary"),
        name="inproj",
    )(x, mods_l, w_bf)


def _lambda_value(lam_ref, lam_init):
    lf = lam_ref[...]
    t1 = jnp.sum(lf[0:1] * lf[1:2], axis=1, keepdims=True)
    t2 = jnp.sum(lf[2:3] * lf[3:4], axis=1, keepdims=True)
    return jnp.exp(t1) - jnp.exp(t2) + lam_init


def _split_maps(q):
    lane = lax.broadcasted_iota(jnp.int32, q.shape, 1)
    first = lane < B_QK
    return jnp.where(first, q, 0.0), jnp.where(first, 0.0, q)


def _subnorm(o, g_ref, lam_init):
    ms = jnp.mean(o * o, axis=-1, keepdims=True)
    return o * lax.rsqrt(ms + RMS_EPS) * g_ref[...] * (1.0 - lam_init)


def _ctx_attn_kernel(*refs, lam_init):
    qa_ref, ka_ref, va_ref = refs[0:3]
    b_refs = refs[3:3 + 3 * B_HEADS]
    lam_ref, subln_ref, ya_ref, yb_ref = refs[3 + 3 * B_HEADS:]
    a_scale = HEAD_DIM ** -0.5
    b_scale = B_QK ** -0.5
    for h in range(A_HEADS):
        sl = slice(h * HEAD_DIM, (h + 1) * HEAD_DIM)
        q = qa_ref[:, sl].astype(BF16)
        k = ka_ref[:, sl].astype(BF16)
        v = va_ref[:, sl].astype(BF16)
        s = _dot_nt(q, k) * a_scale
        e = jnp.exp(s - jnp.max(s, axis=-1, keepdims=True))
        l = jnp.sum(e, axis=-1, keepdims=True)
        ya_ref[:, sl] = (_dot(e.astype(BF16), v) / l).astype(ya_ref.dtype)
    lam = _lambda_value(lam_ref, lam_init)
    for h in range(B_HEADS):
        sl = slice(h * HEAD_DIM, (h + 1) * HEAD_DIM)
        q1, q2 = _split_maps(b_refs[3 * h][...] * b_scale)
        k = b_refs[3 * h + 1][...].astype(BF16)
        v = b_refs[3 * h + 2][...].astype(BF16)
        s1 = _dot_nt(q1.astype(BF16), k)
        s2 = _dot_nt(q2.astype(BF16), k)
        e1 = jnp.exp(s1 - jnp.max(s1, axis=-1, keepdims=True))
        e2 = jnp.exp(s2 - jnp.max(s2, axis=-1, keepdims=True))
        p = e1 / jnp.sum(e1, axis=-1, keepdims=True) - lam * (e2 / jnp.sum(e2, axis=-1, keepdims=True))
        o = _dot(p.astype(BF16), v)
        yb_ref[:, sl] = _subnorm(o, subln_ref, lam_init).astype(yb_ref.dtype)


def _ctx_attn(p, batch, seq, b_lambda_l, b_subln_l, lam_init):
    m = p.shape[0]
    aw = A_HEADS * HEAD_DIM
    nb = aw // HEAD_DIM
    in_specs = [pl.BlockSpec((seq, aw), lambda b: (b, 0)),
                pl.BlockSpec((seq, aw), lambda b: (b, 1)),
                pl.BlockSpec((seq, aw), lambda b: (b, 2))]
    args = [p, p, p]
    for h in range(B_HEADS):
        for seg in range(3):
            col = 3 * nb + seg * B_HEADS + h
            in_specs.append(pl.BlockSpec((seq, HEAD_DIM), functools.partial(lambda b, c: (b, c), c=col)))
            args.append(p)
    in_specs += [pl.BlockSpec((4, B_QK), lambda b: (0, 0)), pl.BlockSpec((1, HEAD_DIM), lambda b: (0, 0))]
    args += [b_lambda_l, b_subln_l.reshape(1, HEAD_DIM)]
    return pl.pallas_call(
        functools.partial(_ctx_attn_kernel, lam_init=lam_init),
        grid=(batch,),
        in_specs=in_specs,
        out_specs=[pl.BlockSpec((seq, aw), lambda b: (b, 0)),
                   pl.BlockSpec((seq, B_HEADS * HEAD_DIM), lambda b: (b, 0))],
        out_shape=[jax.ShapeDtypeStruct((m, aw), BF16),
                   jax.ShapeDtypeStruct((m, B_HEADS * HEAD_DIM), BF16)],
        compiler_params=_cparams("arbitrary"),
        name="ctx_attn",
    )(*args)


def _na_bias_table(rpb, n_rows):
    wr = min(NA_ROWS, n_rows)
    cols = np.arange(GRID_W)
    cs = np.clip(cols - NA_COLS // 2, 0, GRID_W - NA_COLS)
    col_mask = (cols[None, :] >= cs[:, None]) & (cols[None, :] < cs[:, None] + NA_COLS)
    col_idx = np.clip(cols[None, :] - cols[:, None] + NA_COLS - 1, 0, 2 * NA_COLS - 2)
    off = np.arange(wr)
    row_idx = NA_ROWS - 1 + np.arange(wr)[None, :] - off[:, None]
    bias = rpb[:, row_idx[:, :, None, None], col_idx[None, None]]
    bias = bias.transpose(0, 1, 3, 2, 4).reshape(rpb.shape[0], wr, GRID_W, wr * GRID_W)
    mask = np.tile(col_mask[:, None, :], (1, wr, 1)).reshape(GRID_W, wr * GRID_W)
    return jnp.where(jnp.asarray(mask), bias.astype(F32), -jnp.inf)


def _na_kernel(q_ref, k_ref, v_ref, kc_ref, vc_ref, bias_ref, o_ref, kb_scr, vb_scr, kcb_scr, vcb_scr, *, n_rows, wr):
    scale = HEAD_DIM ** -0.5
    kb_scr[...] = k_ref[...].astype(BF16)
    vb_scr[...] = v_ref[...].astype(BF16)
    kcb_scr[...] = kc_ref[...].astype(BF16)
    vcb_scr[...] = vc_ref[...].astype(BF16)

    def body(r, carry):
        rs = jnp.clip(r - wr // 2, 0, n_rows - wr)
        q = q_ref[pl.ds(pl.multiple_of(r * GRID_W, GRID_W), GRID_W), :].astype(BF16)
        w0 = pl.multiple_of(rs * GRID_W, GRID_W)
        kw = kb_scr[pl.ds(w0, wr * GRID_W), :]
        vw = vb_scr[pl.ds(w0, wr * GRID_W), :]
        s_nb = _dot_nt(q, kw) * scale + bias_ref[r - rs]
        s_c = _dot_nt(q, kcb_scr[...]) * scale
        mx = jnp.maximum(jnp.max(s_nb, axis=-1, keepdims=True), jnp.max(s_c, axis=-1, keepdims=True))
        p_nb = jnp.exp(s_nb - mx)
        p_c = jnp.exp(s_c - mx)
        l = jnp.sum(p_nb, axis=-1, keepdims=True) + jnp.sum(p_c, axis=-1, keepdims=True)
        o = (_dot(p_nb.astype(BF16), vw) + _dot(p_c.astype(BF16), vcb_scr[...])) / l
        o_ref[pl.ds(pl.multiple_of(r * GRID_W, GRID_W), GRID_W), :] = o.astype(o_ref.dtype)
        return carry

    lax.fori_loop(0, n_rows, body, 0)


def _na_attn(p, batch, seq, cache_k, cache_v, layer, bias):
    n_rows = seq // GRID_W
    wr = min(NA_ROWS, n_rows)
    past = cache_k.shape[3]
    blk = lambda off: pl.BlockSpec((seq, HEAD_DIM), lambda b, h: (b, off + h))
    cspec = pl.BlockSpec((None, None, None, past, HEAD_DIM), lambda b, h: (b, layer, h, 0, 0))
    return pl.pallas_call(
        functools.partial(_na_kernel, n_rows=n_rows, wr=wr),
        grid=(batch, A_HEADS),
        in_specs=[blk(0), blk(A_HEADS), blk(2 * A_HEADS), cspec, cspec,
                  pl.BlockSpec((None, wr, GRID_W, wr * GRID_W), lambda b, h: (h, 0, 0, 0))],
        out_specs=pl.BlockSpec((seq, HEAD_DIM), lambda b, h: (b, h)),
        out_shape=jax.ShapeDtypeStruct((batch * seq, A_HEADS * HEAD_DIM), BF16),
        scratch_shapes=[pltpu.VMEM((seq, HEAD_DIM), BF16), pltpu.VMEM((seq, HEAD_DIM), BF16),
                        pltpu.VMEM((past, HEAD_DIM), BF16), pltpu.VMEM((past, HEAD_DIM), BF16)],
        compiler_params=_cparams("arbitrary", "arbitrary"),
        name="na_attn",
    )(p, p, p, cache_k, cache_v, bias)


def _rope_tables(seq):
    t = np.arange(seq)
    rows = (t // GRID_W).astype(np.float32)
    cols = (t % GRID_W).astype(np.float32)
    half = B_QK // 2
    freqs = jnp.asarray(ROPE_BASE, F32) ** (-jnp.arange(0, half, 2, dtype=F32) / half)
    ar = jnp.asarray(rows)[:, None] * freqs
    ac = jnp.asarray(cols)[:, None] * freqs
    cr, sr, cc, sc = jnp.cos(ar), jnp.sin(ar), jnp.cos(ac), jnp.sin(ac)
    cos = jnp.concatenate([cr, cr, cc, cc] * 2, axis=-1)
    sin = jnp.concatenate([-sr, sr, -sc, sc] * 2, axis=-1)
    return cos, sin


def _rope(x, cos, sin):
    q = B_QK // 4
    lane = lax.broadcasted_iota(jnp.int32, x.shape, 1)
    first = (lane & (2 * q - 1)) < q
    partner = jnp.where(first, pltpu.roll(x, LANE - q, 1), pltpu.roll(x, q, 1))
    return x * cos + partner * sin


def _diff_lat_kernel(q_ref, k_ref, v_ref, kc_ref, vc_ref, cosq_ref, sinq_ref, cosk_ref, sink_ref, lam_ref, subln_ref,
                     o_ref, kr_scr, kcb_scr, vb_scr, vcb_scr, *, lam_init):
    scale = B_QK ** -0.5

    @pl.when(pl.program_id(2) == 0)
    def _():
        kr_scr[...] = _rope(k_ref[...], cosk_ref[...], sink_ref[...]).astype(BF16)
        kcb_scr[...] = kc_ref[...].astype(BF16)
        vb_scr[...] = v_ref[...].astype(BF16)
        vcb_scr[...] = vc_ref[...].astype(BF16)

    lam = _lambda_value(lam_ref, lam_init)
    q = q_ref[...] * scale
    qr1, qr2 = _split_maps(_rope(q, cosq_ref[...], sinq_ref[...]))
    q1, q2 = _split_maps(q)

    def probs(qr, qp):
        s_l = _dot_nt(qr.astype(BF16), kr_scr[...])
        s_c = _dot_nt(qp.astype(BF16), kcb_scr[...])
        mx = jnp.maximum(jnp.max(s_l, axis=-1, keepdims=True), jnp.max(s_c, axis=-1, keepdims=True))
        e_l = jnp.exp(s_l - mx)
        e_c = jnp.exp(s_c - mx)
        inv = 1.0 / (jnp.sum(e_l, axis=-1, keepdims=True) + jnp.sum(e_c, axis=-1, keepdims=True))
        return e_l * inv, e_c * inv

    p1_l, p1_c = probs(qr1, q1)
    p2_l, p2_c = probs(qr2, q2)
    o = (_dot((p1_l - lam * p2_l).astype(BF16), vb_scr[...])
         + _dot((p1_c - lam * p2_c).astype(BF16), vcb_scr[...]))
    o_ref[...] = _subnorm(o, subln_ref, lam_init).astype(o_ref.dtype)


def _diff_lat_attn(p, batch, seq, cache_k, cache_v, layer, cos, sin, b_lambda_l, b_subln_l, lam_init):
    past = cache_k.shape[3]
    tq = _tile(seq, 256, SUBLANE)
    nq = seq // tq
    base = 3 * A_HEADS
    cspec = pl.BlockSpec((None, None, None, past, HEAD_DIM), lambda b, h, i: (b, layer, h, 0, 0))
    kv = lambda off: pl.BlockSpec((seq, HEAD_DIM), lambda b, h, i: (b, off + h))
    return pl.pallas_call(
        functools.partial(_diff_lat_kernel, lam_init=lam_init),
        grid=(batch, B_HEADS, nq),
        in_specs=[pl.BlockSpec((tq, HEAD_DIM), lambda b, h, i: (b * nq + i, base + h)),
                  kv(base + B_HEADS), kv(base + 2 * B_HEADS), cspec, cspec,
                  pl.BlockSpec((tq, HEAD_DIM), lambda b, h, i: (i, 0)),
                  pl.BlockSpec((tq, HEAD_DIM), lambda b, h, i: (i, 0)),
                  pl.BlockSpec((seq, HEAD_DIM), lambda b, h, i: (0, 0)),
                  pl.BlockSpec((seq, HEAD_DIM), lambda b, h, i: (0, 0)),
                  pl.BlockSpec((4, B_QK), lambda b, h, i: (0, 0)),
                  pl.BlockSpec((1, HEAD_DIM), lambda b, h, i: (0, 0))],
        out_specs=pl.BlockSpec((tq, HEAD_DIM), lambda b, h, i: (b * nq + i, h)),
        out_shape=jax.ShapeDtypeStruct((batch * seq, B_HEADS * HEAD_DIM), BF16),
        scratch_shapes=[pltpu.VMEM((seq, HEAD_DIM), BF16), pltpu.VMEM((past, HEAD_DIM), BF16),
                        pltpu.VMEM((seq, HEAD_DIM), BF16), pltpu.VMEM((past, HEAD_DIM), BF16)],
        compiler_params=_cparams("arbitrary", "arbitrary", "arbitrary"),
        name="diff_lat_attn",
    )(p, p, p, cache_k, cache_v, cos, sin, cos, sin, b_lambda_l, b_subln_l.reshape(1, HEAD_DIM))


def _log_sigmoid(x):
    return jnp.minimum(x, 0.0) - jnp.log1p(jnp.exp(-jnp.abs(x)))


def _mlstm_kernel(q_ref, k_ref, v_ref, og_ref, g_ref, gb_ref, cn_ref, c0_ref, n0_ref, m0_ref,
                  y_ref, cf_ref, nf_ref, mf_ref, hf_scr, *, chunk, n_chunks):
    head = pl.program_id(1)
    scale = HEAD_DIM ** -0.5
    ln = chunk
    row8 = lax.broadcasted_iota(jnp.int32, (SUBLANE, LANE), 0)
    lane8 = lax.broadcasted_iota(jnp.int32, (SUBLANE, LANE), 1)
    sel = jnp.where((lane8 == row8 * C_HEADS + head) & (row8 < 4), 1.0, 0.0).astype(F32)
    ti = lax.broadcasted_iota(jnp.int32, (ln, ln), 0)
    si = lax.broadcasted_iota(jnp.int32, (ln, ln), 1)
    lower = si <= ti
    upper = si >= ti
    lower_f = jnp.where(lower, 1.0, 0.0).astype(F32)
    upper_f = jnp.where(upper, 1.0, 0.0).astype(F32)

    def chunk_step(r0, state, backward):
        c_st, n_st, m_st = state
        g = g_ref[pl.ds(r0, ln), :] + gb_ref[...]
        rows = _dot_nt(sel, g, HI)
        cols = _dot_nt(g, sel, HI)
        lf_rows = _log_sigmoid(rows)
        lf_cols = _log_sigmoid(cols)
        if backward:
            b_row = lax.dot_general(lf_rows, lower_f, (((1,), (0,)), ((), ())), precision=HI,
                                    preferred_element_type=F32)[3:4]
            b_col = lax.dot_general(upper_f, lf_cols, (((1,), (0,)), ((), ())), precision=HI,
                                    preferred_element_type=F32)[:, 3:4]
            i_row, i_col, mask = rows[2:3], cols[:, 2:3], upper
            b_last = b_col[0:1]
        else:
            b_row = lax.dot_general(lf_rows, upper_f, (((1,), (0,)), ((), ())), precision=HI,
                                    preferred_element_type=F32)[1:2]
            b_col = lax.dot_general(lower_f, lf_cols, (((1,), (0,)), ((), ())), precision=HI,
                                    preferred_element_type=F32)[:, 1:2]
            i_row, i_col, mask = rows[0:1], cols[:, 0:1], lower
            b_last = b_col[ln - 1:ln]
        d = jnp.where(mask, b_col - b_row + i_row, -jnp.inf)
        inter = b_col + m_st
        m_row = jnp.maximum(jnp.max(d, axis=-1, keepdims=True), inter)
        w_intra = jnp.exp(d - m_row)
        w_state = jnp.exp(inter - m_row)
        qf = q_ref[pl.ds(r0, ln), :] * scale
        kf = k_ref[pl.ds(r0, ln), :]
        qb = qf.astype(BF16)
        kb = kf.astype(BF16)
        vb = v_ref[pl.ds(r0, ln), :].astype(BF16)
        sc = _dot_nt(qb, kb) * w_intra
        num = w_state * _dot(qb, c_st.astype(BF16)) + _dot(sc.astype(BF16), vb)
        den = w_state * jnp.sum(qf * n_st, axis=-1, keepdims=True) + jnp.sum(sc, axis=-1, keepdims=True)
        h_out = num / jnp.maximum(jnp.abs(den), jnp.exp(-m_row))
        gg = b_last - b_col + i_col
        m_new = jnp.maximum(b_last + m_st, jnp.max(gg, axis=0, keepdims=True))
        w_old = jnp.exp(b_last + m_st - m_new)
        w_s = jnp.exp(gg - m_new)
        kw = kf * w_s
        c_new = w_old * c_st + _dot_tn(kw.astype(BF16), vb)
        n_new = w_old * n_st + jnp.sum(kw, axis=0, keepdims=True)
        return h_out, (c_new, n_new, m_new)

    def start(c):
        return pl.multiple_of(c * ln, ln)

    def fwd_body(c, state):
        r0 = start(c)
        h_out, state = chunk_step(r0, state, False)
        hf_scr[pl.ds(r0, ln), :] = h_out
        return state

    def bwd_body(j, state):
        r0 = start(n_chunks - 1 - j)
        h_out, state = chunk_step(r0, state, True)
        hs = hf_scr[pl.ds(r0, ln), :] + h_out
        ms = jnp.mean(hs * hs, axis=-1, keepdims=True)
        hn = hs * lax.rsqrt(ms + RMS_EPS) * cn_ref[...]
        og = og_ref[pl.ds(r0, ln), :]
        y_ref[pl.ds(r0, ln), :] = (hn / (1.0 + jnp.exp(-og))).astype(y_ref.dtype)
        return state

    for direction, body in ((0, fwd_body), (1, bwd_body)):
        init = (c0_ref[direction], n0_ref[direction], m0_ref[direction][:, 0:1])
        c_f, n_f, m_f = lax.fori_loop(0, n_chunks, body, init)
        cf_ref[direction] = c_f
        nf_ref[direction] = n_f
        mf_ref[direction] = jnp.broadcast_to(m_f, (1, LANE))


def _mlstm(p, batch, seq, gate_bias, c_norm_l, c0, n0, m0):
    chunk = min(MLSTM_CHUNK, seq)
    n_chunks = seq // chunk
    base = 3 * A_HEADS + 3 * B_HEADS
    hd = HEAD_DIM
    blk = lambda off: pl.BlockSpec((seq, hd), lambda b, h: (b, off + h))
    st_c = pl.BlockSpec((None, 2, None, hd, hd), lambda b, h: (b, 0, h, 0, 0))
    st_n = pl.BlockSpec((None, 2, None, 1, hd), lambda b, h: (b, 0, h, 0, 0))
    n0r = n0.reshape(batch, 2, C_HEADS, 1, hd)
    m0r = jnp.broadcast_to(m0[..., None, None], (batch, 2, C_HEADS, 1, LANE))
    y, c_f, n_f, m_f = pl.pallas_call(
        functools.partial(_mlstm_kernel, chunk=chunk, n_chunks=n_chunks),
        grid=(batch, C_HEADS),
        in_specs=[blk(base), blk(base + C_HEADS), blk(base + 2 * C_HEADS), blk(base + 3 * C_HEADS),
                  pl.BlockSpec((seq, LANE), lambda b, h: (b, base + 4 * C_HEADS)),
                  pl.BlockSpec((1, LANE), lambda b, h: (0, 0)),
                  pl.BlockSpec((1, hd), lambda b, h: (0, 0)),
                  st_c, st_n, st_n],
        out_specs=[pl.BlockSpec((seq, hd), lambda b, h: (b, h)), st_c, st_n, st_n],
        out_shape=[jax.ShapeDtypeStruct((batch * seq, C_HEADS * hd), BF16),
                   jax.ShapeDtypeStruct((batch, 2, C_HEADS, hd, hd), F32),
                   jax.ShapeDtypeStruct((batch, 2, C_HEADS, 1, hd), F32),
                   jax.ShapeDtypeStruct((batch, 2, C_HEADS, 1, LANE), F32)],
        scratch_shapes=[pltpu.VMEM((seq, hd), F32)],
        compiler_params=_cparams("arbitrary", "arbitrary"),
        name="mlstm",
    )(p, p, p, p, p, gate_bias, c_norm_l.reshape(1, hd), c0, n0r, m0r)
    return y, c_f, n_f.reshape(batch, 2, C_HEADS, hd), m_f[:, :, :, 0, 0]


def _layernorm(z, g_ref, b_ref):
    mu = jnp.mean(z, axis=-1, keepdims=True)
    zc = z - mu
    var = jnp.mean(zc * zc, axis=-1, keepdims=True)
    return zc * lax.rsqrt(var + LN_EPS) * g_ref[...] + b_ref[...]


def _outproj_kernel(ya_ref, yb_ref, yc_ref, w_ref, x_ref, mod_ref, g_ref, b_ref, x1_ref, h2_ref):
    ycat = jnp.concatenate([ya_ref[...], yb_ref[...], yc_ref[...]], axis=-1)
    y = _dot(ycat, w_ref[...])
    gate = mod_ref[0, 2:3, :]
    x1 = _layernorm(ALPHA * x_ref[...] + gate * y, g_ref, b_ref)
    x1_ref[...] = x1
    h2_ref[...] = (x1 * (1.0 + mod_ref[0, 4:5, :]) + mod_ref[0, 3:4, :]).astype(BF16)


def _outproj(ya, yb, yc, w_bf, x, mods_l, ln_g, ln_b, row_base, rows_per_cond):
    m, d = x.shape
    tm = min(_tile(m, 512, SUBLANE), rows_per_cond)
    per = rows_per_cond // tm
    row = lambda w: pl.BlockSpec((tm, w), lambda i: (i, 0))
    vec = pl.BlockSpec((1, d), lambda i: (0, 0))
    return pl.pallas_call(
        _outproj_kernel,
        grid=(m // tm,),
        in_specs=[row(ya.shape[1]), row(yb.shape[1]), row(yc.shape[1]),
                  pl.BlockSpec((d, d), lambda i: (0, 0)), row(d),
                  pl.BlockSpec((1, 6, d), lambda i: (row_base + i // per, 0, 0)), vec, vec],
        out_specs=[row(d), row(d)],
        out_shape=[jax.ShapeDtypeStruct((m, d), F32), jax.ShapeDtypeStruct((m, d), BF16)],
        compiler_params=_cparams("arbitrary"),
        name="outproj_ln",
    )(ya, yb, yc, w_bf, x, mods_l, ln_g.reshape(1, d), ln_b.reshape(1, d))


def _ffn_kernel(h_ref, hp_ref, hn_ref, wu_ref, cw_ref, cb_ref, wd_ref, x_ref, mod_ref, g_ref, b_ref,
                o_ref, hext_scr, acc_scr, *, seq, tf):
    i = pl.program_id(0)
    f = pl.program_id(1)
    tm = h_ref.shape[0]
    halo = BF16_ROWS

    @pl.when(f == 0)
    def _():
        hext_scr[0:halo, :] = hp_ref[...]
        hext_scr[halo:halo + tm, :] = h_ref[...]
        hext_scr[halo + tm:, :] = hn_ref[...]
        acc_scr[...] = jnp.zeros_like(acc_scr)

    u = _dot(hext_scr[...], wu_ref[...])
    pos = (i * tm + lax.broadcasted_iota(jnp.int32, (tm, 1), 0)) & (seq - 1)
    u_prev = jnp.where(pos != 0, u[halo - 1:halo - 1 + tm], 0.0)
    u_next = jnp.where(pos != seq - 1, u[halo + 1:halo + 1 + tm], 0.0)
    conv = u_prev * cw_ref[0:1, :] + u[halo:halo + tm] * cw_ref[1:2, :] + u_next * cw_ref[2:3, :] + cb_ref[...]
    a = conv[:, :tf]
    g = conv[:, tf:]
    act = (g / (1.0 + jnp.exp(-g))) * a
    acc_scr[...] += _dot(act.astype(BF16), wd_ref[...])

    @pl.when(f == pl.num_programs(1) - 1)
    def _():
        z = ALPHA * x_ref[...] + mod_ref[0, 5:6, :] * acc_scr[...]
        o_ref[...] = _layernorm(z, g_ref, b_ref)


def _ffn(h2, x1, wu_il, cw_il, cb_il, wd_bf, mods_l, ln_g, ln_b, row_base, rows_per_cond, seq, tf):
    m, d = x1.shape
    fp = wd_bf.shape[0]
    tm = min(_tile(m, 512, SUBLANE), rows_per_cond)
    per = rows_per_cond // tm
    assert seq & (seq - 1) == 0 and m % seq == 0
    nh = m // BF16_ROWS
    hb = tm // BF16_ROWS
    vec = pl.BlockSpec((1, d), lambda i, f: (0, 0))
    return pl.pallas_call(
        functools.partial(_ffn_kernel, seq=seq, tf=tf),
        grid=(m // tm, fp // tf),
        in_specs=[pl.BlockSpec((tm, d), lambda i, f: (i, 0)),
                  pl.BlockSpec((BF16_ROWS, d), lambda i, f: (jnp.maximum(i * hb - 1, 0), 0)),
                  pl.BlockSpec((BF16_ROWS, d), lambda i, f: (jnp.minimum((i + 1) * hb, nh - 1), 0)),
                  pl.BlockSpec((d, 2 * tf), lambda i, f: (0, f)),
                  pl.BlockSpec((CONV_W, 2 * tf), lambda i, f: (0, f)),
                  pl.BlockSpec((1, 2 * tf), lambda i, f: (0, f)),
                  pl.BlockSpec((tf, d), lambda i, f: (f, 0)),
                  pl.BlockSpec((tm, d), lambda i, f: (i, 0)),
                  pl.BlockSpec((1, 6, d), lambda i, f: (row_base + i // per, 0, 0)), vec, vec],
        out_specs=pl.BlockSpec((tm, d), lambda i, f: (i, 0)),
        out_shape=jax.ShapeDtypeStruct((m, d), F32),
        scratch_shapes=[pltpu.VMEM((tm + 2 * BF16_ROWS, d), BF16), pltpu.VMEM((tm, d), F32)],
        compiler_params=_cparams("arbitrary", "arbitrary"),
        name="ffn_ln",
    )(h2, h2, h2, wu_il, cw_il, cb_il, wd_bf, x1, mods_l, ln_g.reshape(1, d), ln_b.reshape(1, d))


def _pad_cols(w, n):
    return jnp.pad(w, ((0, 0), (0, n - w.shape[1])))


def _interleave(a, g, tf):
    r, fp = a.shape
    return jnp.stack([a.reshape(r, fp // tf, tf), g.reshape(r, fp // tf, tf)], axis=2).reshape(r, 2 * fp)


def _prep_layer(w_in_l, w_out_l, w_up_l, conv_w_l, conv_b_l, w_down_l, tf):
    n_in = w_in_l.shape[1]
    n_pad = -(-n_in // LANE) * LANE
    dff = w_down_l.shape[0]
    fp = -(-dff // tf) * tf
    w_in_bf = _pad_cols(w_in_l, n_pad).astype(BF16)
    wu = _interleave(_pad_cols(w_up_l[:, :dff], fp), _pad_cols(w_up_l[:, dff:], fp), tf).astype(BF16)
    cw = _interleave(_pad_cols(conv_w_l[:, :dff], fp), _pad_cols(conv_w_l[:, dff:], fp), tf)
    cb2 = conv_b_l.reshape(1, -1)
    cb = _interleave(_pad_cols(cb2[:, :dff], fp), _pad_cols(cb2[:, dff:], fp), tf)
    wd = jnp.pad(w_down_l, ((0, fp - dff), (0, 0))).astype(BF16)
    return w_in_bf, w_out_l.astype(BF16), wu, cw, cb, wd


def _gate_bias_row(c_gate_b_l):
    return _pad_cols(c_gate_b_l.reshape(1, -1), LANE)


def kernel(x_prompt, x_sample, cache_a_k, cache_a_v, cache_b_k, cache_b_v, state_c_C, state_c_n, state_c_m, c, c_ctx, w_mod, b_mod, w_in, c_gate_b, a_rpb, b_lambda, b_subln, c_norm, w_out, ln1_g, ln1_b, ln2_g, ln2_b, w_up, conv_w, conv_b, w_down):
    batch, seq, d = x_prompt.shape
    dec_batch, dec_seq, _ = x_sample.shape
    depth = w_in.shape[0]
    tf = 512

    cond = jnp.concatenate([c_ctx[None, :], c], axis=0)
    cond = jnp.pad(cond, ((0, SUBLANE - cond.shape[0]), (0, 0)))
    mods = _mods(cond, w_mod, b_mod).reshape(depth, SUBLANE, 6, d)

    xp = x_prompt.reshape(batch * seq, d)
    xs = x_sample.reshape(dec_batch * dec_seq, d)
    cos, sin = _rope_tables(dec_seq)
    zeros_c = jnp.zeros((batch, 2, C_HEADS, HEAD_DIM, HEAD_DIM), F32)
    zeros_n = jnp.zeros((batch, 2, C_HEADS, HEAD_DIM), F32)
    zeros_m = jnp.zeros((batch, 2, C_HEADS), F32)
    aw = A_HEADS * HEAD_DIM
    bw = B_HEADS * HEAD_DIM

    def heads(t, n):
        return t.reshape(batch, seq, n, HEAD_DIM).transpose(0, 2, 1, 3)

    outs = [[] for _ in range(7)]
    for l in range(depth):
        lam_init = 0.8 - 0.6 * math.exp(-0.3 * l)
        w_in_bf, w_out_bf, wu, cw, cb, wd = _prep_layer(w_in[l], w_out[l], w_up[l], conv_w[l], conv_b[l], w_down[l], tf)
        gate_bias = _gate_bias_row(c_gate_b[l])

        pp = _inproj(xp, mods[l], w_in_bf, 0, batch * seq)
        ya, yb = _ctx_attn(pp, batch, seq, b_lambda[l], b_subln[l], lam_init)
        yc, c_f, n_f, m_f = _mlstm(pp, batch, seq, gate_bias, c_norm[l], zeros_c, zeros_n, zeros_m)
        outs[0].append(heads(pp[:, aw:2 * aw], A_HEADS))
        outs[1].append(heads(pp[:, 2 * aw:3 * aw], A_HEADS))
        outs[2].append(heads(pp[:, 3 * aw + bw:3 * aw + 2 * bw], B_HEADS))
        outs[3].append(heads(pp[:, 3 * aw + 2 * bw:3 * aw + 3 * bw], B_HEADS))
        outs[4].append(c_f)
        outs[5].append(n_f)
        outs[6].append(m_f)
        x1, h2 = _outproj(ya, yb, yc, w_out_bf, xp, mods[l], ln1_g[l], ln1_b[l], 0, batch * seq)
        xp = _ffn(h2, x1, wu, cw, cb, wd, mods[l], ln2_g[l], ln2_b[l], 0, batch * seq, seq, tf)

        ps = _inproj(xs, mods[l], w_in_bf, 1, dec_seq)
        ya = _na_attn(ps, dec_batch, dec_seq, cache_a_k, cache_a_v, l, _na_bias_table(a_rpb[l], dec_seq // GRID_W))
        yb = _diff_lat_attn(ps, dec_batch, dec_seq, cache_b_k, cache_b_v, l, cos, sin, b_lambda[l], b_subln[l], lam_init)
        yc, _, _, _ = _mlstm(ps, dec_batch, dec_seq, gate_bias, c_norm[l],
                             state_c_C[:, l], state_c_n[:, l], state_c_m[:, l])
        x1, h2 = _outproj(ya, yb, yc, w_out_bf, xs, mods[l], ln1_g[l], ln1_b[l], 1, dec_seq)
        xs = _ffn(h2, x1, wu, cw, cb, wd, mods[l], ln2_g[l], ln2_b[l], 1, dec_seq, dec_seq, tf)

    stacked = [jnp.stack(o, axis=1) for o in outs]
    return (xp.reshape(batch, seq, d), xs.reshape(dec_batch, dec_seq, d), *stacked)
```

```python
import functools
import math

import jax
import jax.numpy as jnp
import numpy as np
from jax import lax
from jax.experimental import pallas as pl
from jax.experimental.pallas import tpu as pltpu

F32 = jnp.float32
BF16 = jnp.bfloat16

DEPTH = 2
GRID_W = 64
A_HEADS = 6
NA_ROWS = 8
NA_COLS = 16
B_HEADS = 5
B_QK = 64
C_HEADS = 5
HEAD_DIM = 128
CONV_W = 3
ROPE_BASE = 10000.0
LN_EPS = 1e-5
RMS_EPS = 1e-6
ALPHA = (2 * DEPTH) ** 0.25

LANE = 128
SUBLANE = 8
BF16_ROWS = 16
VMEM_LIMIT_BYTES = 56 * 1024 * 1024

MLSTM_CHUNK = 256
HI = lax.Precision.HIGHEST


def _tile(n, target, unit=LANE):
    if n <= target:
        return n
    best = unit
    for t in range(unit, target + 1, unit):
        if n % t == 0:
            best = t
    assert n % best == 0, (n, target, unit)
    return best


def _cparams(*sem):
    return pltpu.CompilerParams(dimension_semantics=sem, vmem_limit_bytes=VMEM_LIMIT_BYTES)


def _dot(a, b):
    return jnp.dot(a, b, preferred_element_type=F32)


def _dot_nt(a, b, precision=None):
    return lax.dot_general(a, b, (((1,), (1,)), ((), ())), preferred_element_type=F32, precision=precision)


def _dot_tn(a, b):
    return lax.dot_general(a, b, (((0,), (0,)), ((), ())), preferred_element_type=F32)


def _mods_kernel(cond_ref, w_ref, b_ref, o_ref):
    c = cond_ref[...]
    s = c / (1.0 + jnp.exp(-c))
    o_ref[0] = _dot(s.astype(BF16), w_ref[0].astype(BF16)) + b_ref[0]


def _mods(cond, w_mod, b_mod):
    depth, d, n = w_mod.shape
    tn = _tile(n, 1536)
    return pl.pallas_call(
        _mods_kernel,
        grid=(depth, n // tn),
        in_specs=[pl.BlockSpec((SUBLANE, d), lambda l, j: (0, 0)),
                  pl.BlockSpec((1, d, tn), lambda l, j: (l, 0, j)),
                  pl.BlockSpec((1, 1, tn), lambda l, j: (l, 0, j))],
        out_specs=pl.BlockSpec((1, SUBLANE, tn), lambda l, j: (l, 0, j)),
        out_shape=jax.ShapeDtypeStruct((depth, SUBLANE, n), F32),
        compiler_params=_cparams("arbitrary", "arbitrary"),
        name="mods",
    )(cond, w_mod, b_mod.reshape(depth, 1, n))


def _inproj_kernel(x_ref, mod_ref, w_ref, o_ref, h_scr):
    @pl.when(pl.program_id(1) == 0)
    def _():
        sh = mod_ref[0, 0:1, :]
        sc = mod_ref[0, 1:2, :]
        h_scr[...] = (x_ref[...] * (1.0 + sc) + sh).astype(BF16)

    o_ref[...] = _dot(h_scr[...], w_ref[...])


def _inproj(x, mods_l, w_bf, row_base, rows_per_cond):
    m, d = x.shape
    n = w_bf.shape[1]
    tm = _tile(m, 1024, SUBLANE)
    tm = min(tm, rows_per_cond)
    tn = _tile(n, 1152)
    per = rows_per_cond // tm
    return pl.pallas_call(
        _inproj_kernel,
        grid=(m // tm, n // tn),
        in_specs=[pl.BlockSpec((tm, d), lambda i, j: (i, 0)),
                  pl.BlockSpec((1, 6, d), lambda i, j: (row_base + i // per, 0, 0)),
                  pl.BlockSpec((d, tn), lambda i, j: (0, j))],
        out_specs=pl.BlockSpec((tm, tn), lambda i, j: (i, j)),
        out_shape=jax.ShapeDtypeStruct((m, n), F32),
        scratch_shapes=[pltpu.VMEM((tm, d), BF16)],
        compiler_params=_cparams("arbitrary", "arbitrary"),
        name="inproj",
    )(x, mods_l, w_bf)


def _lambda_value(lam_ref, lam_init):
    lf = lam_ref[...]
    t1 = jnp.sum(lf[0:1] * lf[1:2], axis=1, keepdims=True)
    t2 = jnp.sum(lf[2:3] * lf[3:4], axis=1, keepdims=True)
    return jnp.exp(t1) - jnp.exp(t2) + lam_init


def _split_maps(q):
    lane = lax.broadcasted_iota(jnp.int32, q.shape, 1)
    first = lane < B_QK
    return jnp.where(first, q, 0.0), jnp.where(first, 0.0, q)


def _subnorm(o, g_ref, lam_init):
    ms = jnp.mean(o * o, axis=-1, keepdims=True)
    return o * lax.rsqrt(ms + RMS_EPS) * g_ref[...] * (1.0 - lam_init)


def _ctx_attn_kernel(*refs, lam_init):
    qa_ref, ka_ref, va_ref = refs[0:3]
    b_refs = refs[3:3 + 3 * B_HEADS]
    lam_ref, subln_ref = refs[3 + 3 * B_HEADS:5 + 3 * B_HEADS]
    ya_ref, yb_ref, nak_ref, nav_ref, nbk_ref, nbv_ref = refs[-6:]
    a_scale = HEAD_DIM ** -0.5
    b_scale = B_QK ** -0.5
    for h in range(A_HEADS):
        sl = slice(h * HEAD_DIM, (h + 1) * HEAD_DIM)
        nak_ref[h] = ka_ref[:, sl]
        nav_ref[h] = va_ref[:, sl]
        q = qa_ref[:, sl].astype(BF16)
        k = ka_ref[:, sl].astype(BF16)
        v = va_ref[:, sl].astype(BF16)
        s = _dot_nt(q, k) * a_scale
        e = jnp.exp(s - jnp.max(s, axis=-1, keepdims=True))
        l = jnp.sum(e, axis=-1, keepdims=True)
        ya_ref[:, sl] = (_dot(e.astype(BF16), v) / l).astype(ya_ref.dtype)
    lam = _lambda_value(lam_ref, lam_init)
    for h in range(B_HEADS):
        sl = slice(h * HEAD_DIM, (h + 1) * HEAD_DIM)
        q1, q2 = _split_maps(b_refs[3 * h][...] * b_scale)
        nbk_ref[h] = b_refs[3 * h + 1][...]
        nbv_ref[h] = b_refs[3 * h + 2][...]
        k = b_refs[3 * h + 1][...].astype(BF16)
        v = b_refs[3 * h + 2][...].astype(BF16)
        s1 = _dot_nt(q1.astype(BF16), k)
        s2 = _dot_nt(q2.astype(BF16), k)
        e1 = jnp.exp(s1 - jnp.max(s1, axis=-1, keepdims=True))
        e2 = jnp.exp(s2 - jnp.max(s2, axis=-1, keepdims=True))
        p = e1 / jnp.sum(e1, axis=-1, keepdims=True) - lam * (e2 / jnp.sum(e2, axis=-1, keepdims=True))
        o = _dot(p.astype(BF16), v)
        yb_ref[:, sl] = _subnorm(o, subln_ref, lam_init).astype(yb_ref.dtype)


def _ctx_attn(p, batch, seq, b_lambda_l, b_subln_l, lam_init, layer, depth, carried):
    m = p.shape[0]
    aw = A_HEADS * HEAD_DIM
    nb = aw // HEAD_DIM
    in_specs = [pl.BlockSpec((seq, aw), lambda b: (b, 0)),
                pl.BlockSpec((seq, aw), lambda b: (b, 1)),
                pl.BlockSpec((seq, aw), lambda b: (b, 2))]
    args = [p, p, p]
    for h in range(B_HEADS):
        for seg in range(3):
            col = 3 * nb + seg * B_HEADS + h
            in_specs.append(pl.BlockSpec((seq, HEAD_DIM), functools.partial(lambda b, c: (b, c), c=col)))
            args.append(p)
    in_specs += [pl.BlockSpec((4, B_QK), lambda b: (0, 0)), pl.BlockSpec((1, HEAD_DIM), lambda b: (0, 0))]
    args += [b_lambda_l, b_subln_l.reshape(1, HEAD_DIM)]
    aliases = {}
    if carried is not None:
        for j, buf in enumerate(carried):
            aliases[len(args)] = 2 + j
            in_specs.append(pl.BlockSpec(memory_space=pl.ANY))
            args.append(buf)
    kv_spec = lambda nh: pl.BlockSpec((None, None, nh, seq, HEAD_DIM), lambda b: (b, layer, 0, 0, 0))
    kv_shape = lambda nh: jax.ShapeDtypeStruct((batch, depth, nh, seq, HEAD_DIM), F32)
    out = pl.pallas_call(
        functools.partial(_ctx_attn_kernel, lam_init=lam_init),
        grid=(batch,),
        in_specs=in_specs,
        out_specs=[pl.BlockSpec((seq, aw), lambda b: (b, 0)),
                   pl.BlockSpec((seq, B_HEADS * HEAD_DIM), lambda b: (b, 0)),
                   kv_spec(A_HEADS), kv_spec(A_HEADS), kv_spec(B_HEADS), kv_spec(B_HEADS)],
        out_shape=[jax.ShapeDtypeStruct((m, aw), BF16),
                   jax.ShapeDtypeStruct((m, B_HEADS * HEAD_DIM), BF16),
                   kv_shape(A_HEADS), kv_shape(A_HEADS), kv_shape(B_HEADS), kv_shape(B_HEADS)],
        input_output_aliases=aliases,
        compiler_params=_cparams("arbitrary"),
        name="ctx_attn",
    )(*args)
    return out[0], out[1], tuple(out[2:])


def _na_bias_table(rpb, n_rows):
    wr = min(NA_ROWS, n_rows)
    cols = np.arange(GRID_W)
    cs = np.clip(cols - NA_COLS // 2, 0, GRID_W - NA_COLS)
    col_mask = (cols[None, :] >= cs[:, None]) & (cols[None, :] < cs[:, None] + NA_COLS)
    col_idx = np.clip(cols[None, :] - cols[:, None] + NA_COLS - 1, 0, 2 * NA_COLS - 2)
    onehot = (col_idx[None] == np.arange(2 * NA_COLS - 1)[:, None, None]).astype(np.float32)
    toe = jnp.einsum('hrd,dqk->hqrk', rpb.astype(F32), jnp.asarray(onehot), precision=HI)
    toe = jnp.where(jnp.asarray(col_mask)[None, :, None, :], toe, -jnp.inf)
    tabs = [toe[:, :, NA_ROWS - 1 - e:NA_ROWS - 1 - e + wr, :].reshape(rpb.shape[0], GRID_W, wr * GRID_W)
            for e in range(wr)]
    return jnp.stack(tabs, axis=1)


def _na_kernel(q_ref, k_ref, v_ref, kc_ref, vc_ref, bias_ref, o_ref, kb_scr, vb_scr, kcb_scr, vcb_scr, *, n_rows, wr):
    scale = HEAD_DIM ** -0.5
    kb_scr[...] = k_ref[...].astype(BF16)
    vb_scr[...] = v_ref[...].astype(BF16)
    kcb_scr[...] = kc_ref[...].astype(BF16)
    vcb_scr[...] = vc_ref[...].astype(BF16)

    def body(r, carry):
        rs = jnp.clip(r - wr // 2, 0, n_rows - wr)
        q = q_ref[pl.ds(pl.multiple_of(r * GRID_W, GRID_W), GRID_W), :].astype(BF16)
        w0 = pl.multiple_of(rs * GRID_W, GRID_W)
        kw = kb_scr[pl.ds(w0, wr * GRID_W), :]
        vw = vb_scr[pl.ds(w0, wr * GRID_W), :]
        s_nb = _dot_nt(q, kw) * scale + bias_ref[r - rs]
        s_c = _dot_nt(q, kcb_scr[...]) * scale
        mx = jnp.maximum(jnp.max(s_nb, axis=-1, keepdims=True), jnp.max(s_c, axis=-1, keepdims=True))
        p_nb = jnp.exp(s_nb - mx)
        p_c = jnp.exp(s_c - mx)
        l = jnp.sum(p_nb, axis=-1, keepdims=True) + jnp.sum(p_c, axis=-1, keepdims=True)
        o = (_dot(p_nb.astype(BF16), vw) + _dot(p_c.astype(BF16), vcb_scr[...])) / l
        o_ref[pl.ds(pl.multiple_of(r * GRID_W, GRID_W), GRID_W), :] = o.astype(o_ref.dtype)
        return carry

    lax.fori_loop(0, n_rows, body, 0)


def _na_attn(p, batch, seq, cache_k, cache_v, layer, bias):
    n_rows = seq // GRID_W
    wr = min(NA_ROWS, n_rows)
    past = cache_k.shape[3]
    blk = lambda off: pl.BlockSpec((seq, HEAD_DIM), lambda b, h: (b, off + h))
    cspec = pl.BlockSpec((None, None, None, past, HEAD_DIM), lambda b, h: (b, layer, h, 0, 0))
    return pl.pallas_call(
        functools.partial(_na_kernel, n_rows=n_rows, wr=wr),
        grid=(batch, A_HEADS),
        in_specs=[blk(0), blk(A_HEADS), blk(2 * A_HEADS), cspec, cspec,
                  pl.BlockSpec((None, wr, GRID_W, wr * GRID_W), lambda b, h: (h, 0, 0, 0))],
        out_specs=pl.BlockSpec((seq, HEAD_DIM), lambda b, h: (b, h)),
        out_shape=jax.ShapeDtypeStruct((batch * seq, A_HEADS * HEAD_DIM), BF16),
        scratch_shapes=[pltpu.VMEM((seq, HEAD_DIM), BF16), pltpu.VMEM((seq, HEAD_DIM), BF16),
                        pltpu.VMEM((past, HEAD_DIM), BF16), pltpu.VMEM((past, HEAD_DIM), BF16)],
        compiler_params=_cparams("arbitrary", "arbitrary"),
        name="na_attn",
    )(p, p, p, cache_k, cache_v, bias)


def _rope_tables(seq):
    t = np.arange(seq)
    rows = (t // GRID_W).astype(np.float32)
    cols = (t % GRID_W).astype(np.float32)
    half = B_QK // 2
    freqs = jnp.asarray(ROPE_BASE, F32) ** (-jnp.arange(0, half, 2, dtype=F32) / half)
    ar = jnp.asarray(rows)[:, None] * freqs
    ac = jnp.asarray(cols)[:, None] * freqs
    cr, sr, cc, sc = jnp.cos(ar), jnp.sin(ar), jnp.cos(ac), jnp.sin(ac)
    cos = jnp.concatenate([cr, cr, cc, cc] * 2, axis=-1)
    sin = jnp.concatenate([-sr, sr, -sc, sc] * 2, axis=-1)
    return cos, sin


def _rope(x, cos, sin):
    q = B_QK // 4
    lane = lax.broadcasted_iota(jnp.int32, x.shape, 1)
    first = (lane & (2 * q - 1)) < q
    partner = jnp.where(first, pltpu.roll(x, LANE - q, 1), pltpu.roll(x, q, 1))
    return x * cos + partner * sin


def _diff_lat_kernel(q_ref, k_ref, v_ref, kc_ref, vc_ref, cosq_ref, sinq_ref, cosk_ref, sink_ref, lam_ref, subln_ref,
                     o_ref, kr_scr, kcb_scr, vb_scr, vcb_scr, *, lam_init):
    scale = B_QK ** -0.5

    @pl.when(pl.program_id(2) == 0)
    def _():
        kr_scr[...] = _rope(k_ref[...], cosk_ref[...], sink_ref[...]).astype(BF16)
        kcb_scr[...] = kc_ref[...].astype(BF16)
        vb_scr[...] = v_ref[...].astype(BF16)
        vcb_scr[...] = vc_ref[...].astype(BF16)

    lam = _lambda_value(lam_ref, lam_init)
    q = q_ref[...] * scale
    qr1, qr2 = _split_maps(_rope(q, cosq_ref[...], sinq_ref[...]))
    q1, q2 = _split_maps(q)

    def probs(qr, qp):
        s_l = _dot_nt(qr.astype(BF16), kr_scr[...])
        s_c = _dot_nt(qp.astype(BF16), kcb_scr[...])
        mx = jnp.maximum(jnp.max(s_l, axis=-1, keepdims=True), jnp.max(s_c, axis=-1, keepdims=True))
        e_l = jnp.exp(s_l - mx)
        e_c = jnp.exp(s_c - mx)
        inv = 1.0 / (jnp.sum(e_l, axis=-1, keepdims=True) + jnp.sum(e_c, axis=-1, keepdims=True))
        return e_l * inv, e_c * inv

    p1_l, p1_c = probs(qr1, q1)
    p2_l, p2_c = probs(qr2, q2)
    o = (_dot((p1_l - lam * p2_l).astype(BF16), vb_scr[...])
         + _dot((p1_c - lam * p2_c).astype(BF16), vcb_scr[...]))
    o_ref[...] = _subnorm(o, subln_ref, lam_init).astype(o_ref.dtype)


def _diff_lat_attn(p, batch, seq, cache_k, cache_v, layer, cos, sin, b_lambda_l, b_subln_l, lam_init):
    past = cache_k.shape[3]
    tq = _tile(seq, 256, SUBLANE)
    nq = seq // tq
    base = 3 * A_HEADS
    cspec = pl.BlockSpec((None, None, None, past, HEAD_DIM), lambda b, h, i: (b, layer, h, 0, 0))
    kv = lambda off: pl.BlockSpec((seq, HEAD_DIM), lambda b, h, i: (b, off + h))
    return pl.pallas_call(
        functools.partial(_diff_lat_kernel, lam_init=lam_init),
        grid=(batch, B_HEADS, nq),
        in_specs=[pl.BlockSpec((tq, HEAD_DIM), lambda b, h, i: (b * nq + i, base + h)),
                  kv(base + B_HEADS), kv(base + 2 * B_HEADS), cspec, cspec,
                  pl.BlockSpec((tq, HEAD_DIM), lambda b, h, i: (i, 0)),
                  pl.BlockSpec((tq, HEAD_DIM), lambda b, h, i: (i, 0)),
                  pl.BlockSpec((seq, HEAD_DIM), lambda b, h, i: (0, 0)),
                  pl.BlockSpec((seq, HEAD_DIM), lambda b, h, i: (0, 0)),
                  pl.BlockSpec((4, B_QK), lambda b, h, i: (0, 0)),
                  pl.BlockSpec((1, HEAD_DIM), lambda b, h, i: (0, 0))],
        out_specs=pl.BlockSpec((tq, HEAD_DIM), lambda b, h, i: (b * nq + i, h)),
        out_shape=jax.ShapeDtypeStruct((batch * seq, B_HEADS * HEAD_DIM), BF16),
        scratch_shapes=[pltpu.VMEM((seq, HEAD_DIM), BF16), pltpu.VMEM((past, HEAD_DIM), BF16),
                        pltpu.VMEM((seq, HEAD_DIM), BF16), pltpu.VMEM((past, HEAD_DIM), BF16)],
        compiler_params=_cparams("arbitrary", "arbitrary", "arbitrary"),
        name="diff_lat_attn",
    )(p, p, p, cache_k, cache_v, cos, sin, cos, sin, b_lambda_l, b_subln_l.reshape(1, HEAD_DIM))


def _log_sigmoid(x):
    return jnp.minimum(x, 0.0) - jnp.log1p(jnp.exp(-jnp.abs(x)))


def _mlstm_kernel(q_ref, k_ref, v_ref, og_ref, g_ref, gb_ref, cn_ref, c0_ref, n0_ref, m0_ref,
                  y_ref, cf_ref, nf_ref, mf_ref, hf_scr, *, chunk, n_chunks):
    head = pl.program_id(1)
    scale = HEAD_DIM ** -0.5
    ln = chunk
    row8 = lax.broadcasted_iota(jnp.int32, (SUBLANE, LANE), 0)
    lane8 = lax.broadcasted_iota(jnp.int32, (SUBLANE, LANE), 1)
    sel = jnp.where((lane8 == row8 * C_HEADS + head) & (row8 < 4), 1.0, 0.0).astype(F32)
    ti = lax.broadcasted_iota(jnp.int32, (ln, ln), 0)
    si = lax.broadcasted_iota(jnp.int32, (ln, ln), 1)
    lower = si <= ti
    upper = si >= ti
    lower_f = jnp.where(lower, 1.0, 0.0).astype(F32)
    upper_f = jnp.where(upper, 1.0, 0.0).astype(F32)

    def chunk_step(r0, state, backward):
        c_st, n_st, m_st = state
        g = g_ref[pl.ds(r0, ln), :] + gb_ref[...]
        rows = _dot_nt(sel, g, HI)
        cols = _dot_nt(g, sel, HI)
        lf_rows = _log_sigmoid(rows)
        lf_cols = _log_sigmoid(cols)
        if backward:
            b_row = lax.dot_general(lf_rows, lower_f, (((1,), (0,)), ((), ())), precision=HI,
                                    preferred_element_type=F32)[3:4]
            b_col = lax.dot_general(upper_f, lf_cols, (((1,), (0,)), ((), ())), precision=HI,
                                    preferred_element_type=F32)[:, 3:4]
            i_row, i_col, mask = rows[2:3], cols[:, 2:3], upper
            b_last = b_col[0:1]
        else:
            b_row = lax.dot_general(lf_rows, upper_f, (((1,), (0,)), ((), ())), precision=HI,
                                    preferred_element_type=F32)[1:2]
            b_col = lax.dot_general(lower_f, lf_cols, (((1,), (0,)), ((), ())), precision=HI,
                                    preferred_element_type=F32)[:, 1:2]
            i_row, i_col, mask = rows[0:1], cols[:, 0:1], lower
            b_last = b_col[ln - 1:ln]
        d = jnp.where(mask, b_col - b_row + i_row, -jnp.inf)
        inter = b_col + m_st
        m_row = jnp.maximum(jnp.max(d, axis=-1, keepdims=True), inter)
        w_intra = jnp.exp(d - m_row)
        w_state = jnp.exp(inter - m_row)
        qf = q_ref[pl.ds(r0, ln), :] * scale
        kf = k_ref[pl.ds(r0, ln), :]
        qb = qf.astype(BF16)
        kb = kf.astype(BF16)
        vb = v_ref[pl.ds(r0, ln), :].astype(BF16)
        sc = _dot_nt(qb, kb) * w_intra
        num = w_state * _dot(qb, c_st.astype(BF16)) + _dot(sc.astype(BF16), vb)
        den = w_state * jnp.sum(qf * n_st, axis=-1, keepdims=True) + jnp.sum(sc, axis=-1, keepdims=True)
        h_out = num / jnp.maximum(jnp.abs(den), jnp.exp(-m_row))
        gg = b_last - b_col + i_col
        m_new = jnp.maximum(b_last + m_st, jnp.max(gg, axis=0, keepdims=True))
        w_old = jnp.exp(b_last + m_st - m_new)
        w_s = jnp.exp(gg - m_new)
        kw = kf * w_s
        c_new = w_old * c_st + _dot_tn(kw.astype(BF16), vb)
        n_new = w_old * n_st + jnp.sum(kw, axis=0, keepdims=True)
        return h_out, (c_new, n_new, m_new)

    def start(c):
        return pl.multiple_of(c * ln, ln)

    def fwd_body(c, state):
        r0 = start(c)
        h_out, state = chunk_step(r0, state, False)
        hf_scr[pl.ds(r0, ln), :] = h_out
        return state

    def bwd_body(j, state):
        r0 = start(n_chunks - 1 - j)
        h_out, state = chunk_step(r0, state, True)
        hs = hf_scr[pl.ds(r0, ln), :] + h_out
        ms = jnp.mean(hs * hs, axis=-1, keepdims=True)
        hn = hs * lax.rsqrt(ms + RMS_EPS) * cn_ref[...]
        og = og_ref[pl.ds(r0, ln), :]
        y_ref[pl.ds(r0, ln), :] = (hn / (1.0 + jnp.exp(-og))).astype(y_ref.dtype)
        return state

    for direction, body in ((0, fwd_body), (1, bwd_body)):
        init = (c0_ref[direction], n0_ref[direction], m0_ref[direction][:, 0:1])
        c_f, n_f, m_f = lax.fori_loop(0, n_chunks, body, init)
        cf_ref[direction] = c_f
        nf_ref[direction] = n_f
        mf_ref[direction] = jnp.broadcast_to(m_f, (1, LANE))


def _mlstm(p, batch, seq, gate_bias, c_norm_l, c0, n0, m0):
    chunk = min(MLSTM_CHUNK, seq)
    n_chunks = seq // chunk
    base = 3 * A_HEADS + 3 * B_HEADS
    hd = HEAD_DIM
    blk = lambda off: pl.BlockSpec((seq, hd), lambda b, h: (b, off + h))
    st_c = pl.BlockSpec((None, 2, None, hd, hd), lambda b, h: (b, 0, h, 0, 0))
    st_n = pl.BlockSpec((None, 2, None, 1, hd), lambda b, h: (b, 0, h, 0, 0))
    n0r = n0.reshape(batch, 2, C_HEADS, 1, hd)
    m0r = jnp.broadcast_to(m0[..., None, None], (batch, 2, C_HEADS, 1, LANE))
    y, c_f, n_f, m_f = pl.pallas_call(
        functools.partial(_mlstm_kernel, chunk=chunk, n_chunks=n_chunks),
        grid=(batch, C_HEADS),
        in_specs=[blk(base), blk(base + C_HEADS), blk(base + 2 * C_HEADS), blk(base + 3 * C_HEADS),
                  pl.BlockSpec((seq, LANE), lambda b, h: (b, base + 4 * C_HEADS)),
                  pl.BlockSpec((1, LANE), lambda b, h: (0, 0)),
                  pl.BlockSpec((1, hd), lambda b, h: (0, 0)),
                  st_c, st_n, st_n],
        out_specs=[pl.BlockSpec((seq, hd), lambda b, h: (b, h)), st_c, st_n, st_n],
        out_shape=[jax.ShapeDtypeStruct((batch * seq, C_HEADS * hd), BF16),
                   jax.ShapeDtypeStruct((batch, 2, C_HEADS, hd, hd), F32),
                   jax.ShapeDtypeStruct((batch, 2, C_HEADS, 1, hd), F32),
                   jax.ShapeDtypeStruct((batch, 2, C_HEADS, 1, LANE), F32)],
        scratch_shapes=[pltpu.VMEM((seq, hd), F32)],
        compiler_params=_cparams("arbitrary", "arbitrary"),
        name="mlstm",
    )(p, p, p, p, p, gate_bias, c_norm_l.reshape(1, hd), c0, n0r, m0r)
    return y, c_f, n_f.reshape(batch, 2, C_HEADS, hd), m_f[:, :, :, 0, 0]


def _layernorm(z, g_ref, b_ref):
    mu = jnp.mean(z, axis=-1, keepdims=True)
    zc = z - mu
    var = jnp.mean(zc * zc, axis=-1, keepdims=True)
    return zc * lax.rsqrt(var + LN_EPS) * g_ref[...] + b_ref[...]


def _outproj_kernel(ya_ref, yb_ref, yc_ref, w_ref, x_ref, mod_ref, g_ref, b_ref, x1_ref, h2_ref):
    ycat = jnp.concatenate([ya_ref[...], yb_ref[...], yc_ref[...]], axis=-1)
    y = _dot(ycat, w_ref[...])
    gate = mod_ref[0, 2:3, :]
    x1 = _layernorm(ALPHA * x_ref[...] + gate * y, g_ref, b_ref)
    x1_ref[...] = x1
    h2_ref[...] = (x1 * (1.0 + mod_ref[0, 4:5, :]) + mod_ref[0, 3:4, :]).astype(BF16)


def _outproj(ya, yb, yc, w_bf, x, mods_l, ln_g, ln_b, row_base, rows_per_cond):
    m, d = x.shape
    tm = min(_tile(m, 512, SUBLANE), rows_per_cond)
    per = rows_per_cond // tm
    row = lambda w: pl.BlockSpec((tm, w), lambda i: (i, 0))
    vec = pl.BlockSpec((1, d), lambda i: (0, 0))
    return pl.pallas_call(
        _outproj_kernel,
        grid=(m // tm,),
        in_specs=[row(ya.shape[1]), row(yb.shape[1]), row(yc.shape[1]),
                  pl.BlockSpec((d, d), lambda i: (0, 0)), row(d),
                  pl.BlockSpec((1, 6, d), lambda i: (row_base + i // per, 0, 0)), vec, vec],
        out_specs=[row(d), row(d)],
        out_shape=[jax.ShapeDtypeStruct((m, d), F32), jax.ShapeDtypeStruct((m, d), BF16)],
        compiler_params=_cparams("arbitrary"),
        name="outproj_ln",
    )(ya, yb, yc, w_bf, x, mods_l, ln_g.reshape(1, d), ln_b.reshape(1, d))


def _ffn_kernel(h_ref, hp_ref, hn_ref, wa_ref, wg_ref, cwa_ref, cwg_ref, cba_ref, cbg_ref, wd_ref, x_ref, mod_ref,
                g_ref, b_ref, o_ref, hext_scr, acc_scr, *, seq):
    i = pl.program_id(0)
    f = pl.program_id(1)
    tm = h_ref.shape[0]
    halo = BF16_ROWS

    @pl.when(f == 0)
    def _():
        hext_scr[0:halo, :] = hp_ref[...]
        hext_scr[halo:halo + tm, :] = h_ref[...]
        hext_scr[halo + tm:, :] = hn_ref[...]
        acc_scr[...] = jnp.zeros_like(acc_scr)

    pos = (i * tm + lax.broadcasted_iota(jnp.int32, (tm, 1), 0)) & (seq - 1)
    has_prev = pos != 0
    has_next = pos != seq - 1

    def up_conv(w_ref, cw_ref, cb_ref):
        u = _dot(hext_scr[...], w_ref[...])
        u_prev = jnp.where(has_prev, u[halo - 1:halo - 1 + tm], 0.0)
        u_next = jnp.where(has_next, u[halo + 1:halo + 1 + tm], 0.0)
        return u_prev * cw_ref[0:1, :] + u[halo:halo + tm] * cw_ref[1:2, :] + u_next * cw_ref[2:3, :] + cb_ref[...]

    a = up_conv(wa_ref, cwa_ref, cba_ref)
    g = up_conv(wg_ref, cwg_ref, cbg_ref)
    act = (g / (1.0 + jnp.exp(-g))) * a
    acc_scr[...] += _dot(act.astype(BF16), wd_ref[...])

    @pl.when(f == pl.num_programs(1) - 1)
    def _():
        z = ALPHA * x_ref[...] + mod_ref[0, 5:6, :] * acc_scr[...]
        o_ref[...] = _layernorm(z, g_ref, b_ref)


def _ffn(h2, x1, ffn_w, mods_l, ln_g, ln_b, row_base, rows_per_cond, seq, tf):
    wa_bf, wg_bf, cwa, cwg, cba, cbg, wd_bf = ffn_w
    m, d = x1.shape
    fp = wd_bf.shape[0]
    tm = min(_tile(m, 512, SUBLANE), rows_per_cond)
    per = rows_per_cond // tm
    assert seq & (seq - 1) == 0 and m % seq == 0
    nh = m // BF16_ROWS
    hb = tm // BF16_ROWS
    vec = pl.BlockSpec((1, d), lambda i, f: (0, 0))
    wcol = lambda rows: pl.BlockSpec((rows, tf), lambda i, f: (0, f))
    return pl.pallas_call(
        functools.partial(_ffn_kernel, seq=seq),
        grid=(m // tm, fp // tf),
        in_specs=[pl.BlockSpec((tm, d), lambda i, f: (i, 0)),
                  pl.BlockSpec((BF16_ROWS, d), lambda i, f: (jnp.maximum(i * hb - 1, 0), 0)),
                  pl.BlockSpec((BF16_ROWS, d), lambda i, f: (jnp.minimum((i + 1) * hb, nh - 1), 0)),
                  wcol(d), wcol(d), wcol(CONV_W), wcol(CONV_W), wcol(1), wcol(1),
                  pl.BlockSpec((tf, d), lambda i, f: (f, 0)),
                  pl.BlockSpec((tm, d), lambda i, f: (i, 0)),
                  pl.BlockSpec((1, 6, d), lambda i, f: (row_base + i // per, 0, 0)), vec, vec],
        out_specs=pl.BlockSpec((tm, d), lambda i, f: (i, 0)),
        out_shape=jax.ShapeDtypeStruct((m, d), F32),
        scratch_shapes=[pltpu.VMEM((tm + 2 * BF16_ROWS, d), BF16), pltpu.VMEM((tm, d), F32)],
        compiler_params=_cparams("arbitrary", "arbitrary"),
        name="ffn_ln",
    )(h2, h2, h2, wa_bf, wg_bf, cwa, cwg, cba, cbg, wd_bf, x1, mods_l, ln_g.reshape(1, d), ln_b.reshape(1, d))


def _pad_cols(w, n):
    return jnp.pad(w, ((0, 0), (0, n - w.shape[1])))


def _prep_layer(w_in_l, w_out_l, w_up_l, conv_w_l, conv_b_l, w_down_l, tf):
    n_in = w_in_l.shape[1]
    n_pad = -(-n_in // LANE) * LANE
    dff = w_down_l.shape[0]
    fp = -(-dff // tf) * tf
    w_in_bf = _pad_cols(w_in_l, n_pad).astype(BF16)
    halves = lambda t: (_pad_cols(t[:, :dff], fp), _pad_cols(t[:, dff:], fp))
    wa, wg = halves(w_up_l)
    cwa, cwg = halves(conv_w_l)
    cba, cbg = halves(conv_b_l.reshape(1, -1))
    wd = jnp.pad(w_down_l, ((0, fp - dff), (0, 0))).astype(BF16)
    return w_in_bf, w_out_l.astype(BF16), (wa.astype(BF16), wg.astype(BF16), cwa, cwg, cba, cbg, wd)


def _gate_bias_row(c_gate_b_l):
    return _pad_cols(c_gate_b_l.reshape(1, -1), LANE)


def kernel(x_prompt, x_sample, cache_a_k, cache_a_v, cache_b_k, cache_b_v, state_c_C, state_c_n, state_c_m, c, c_ctx, w_mod, b_mod, w_in, c_gate_b, a_rpb, b_lambda, b_subln, c_norm, w_out, ln1_g, ln1_b, ln2_g, ln2_b, w_up, conv_w, conv_b, w_down):
    batch, seq, d = x_prompt.shape
    dec_batch, dec_seq, _ = x_sample.shape
    depth = w_in.shape[0]
    tf = 512

    cond = jnp.concatenate([c_ctx[None, :], c], axis=0)
    cond = jnp.pad(cond, ((0, SUBLANE - cond.shape[0]), (0, 0)))
    mods = _mods(cond, w_mod, b_mod).reshape(depth, SUBLANE, 6, d)

    xp = x_prompt.reshape(batch * seq, d)
    xs = x_sample.reshape(dec_batch * dec_seq, d)
    cos, sin = _rope_tables(dec_seq)
    zeros_c = jnp.zeros((batch, 2, C_HEADS, HEAD_DIM, HEAD_DIM), F32)
    zeros_n = jnp.zeros((batch, 2, C_HEADS, HEAD_DIM), F32)
    zeros_m = jnp.zeros((batch, 2, C_HEADS), F32)

    new_kv = None
    outs = [[] for _ in range(3)]
    for l in range(depth):
        lam_init = 0.8 - 0.6 * math.exp(-0.3 * l)
        w_in_bf, w_out_bf, ffn_w = _prep_layer(w_in[l], w_out[l], w_up[l], conv_w[l], conv_b[l], w_down[l], tf)
        gate_bias = _gate_bias_row(c_gate_b[l])

        pp = _inproj(xp, mods[l], w_in_bf, 0, batch * seq)
        ya, yb, new_kv = _ctx_attn(pp, batch, seq, b_lambda[l], b_subln[l], lam_init, l, depth, new_kv)
        yc, c_f, n_f, m_f = _mlstm(pp, batch, seq, gate_bias, c_norm[l], zeros_c, zeros_n, zeros_m)
        outs[0].append(c_f)
        outs[1].append(n_f)
        outs[2].append(m_f)
        x1, h2 = _outproj(ya, yb, yc, w_out_bf, xp, mods[l], ln1_g[l], ln1_b[l], 0, batch * seq)
        xp = _ffn(h2, x1, ffn_w, mods[l], ln2_g[l], ln2_b[l], 0, batch * seq, seq, tf)

        ps = _inproj(xs, mods[l], w_in_bf, 1, dec_seq)
        ya = _na_attn(ps, dec_batch, dec_seq, cache_a_k, cache_a_v, l, _na_bias_table(a_rpb[l], dec_seq // GRID_W))
        yb = _diff_lat_attn(ps, dec_batch, dec_seq, cache_b_k, cache_b_v, l, cos, sin, b_lambda[l], b_subln[l], lam_init)
        yc, _, _, _ = _mlstm(ps, dec_batch, dec_seq, gate_bias, c_norm[l],
                             state_c_C[:, l], state_c_n[:, l], state_c_m[:, l])
        x1, h2 = _outproj(ya, yb, yc, w_out_bf, xs, mods[l], ln1_g[l], ln1_b[l], 1, dec_seq)
        xs = _ffn(h2, x1, ffn_w, mods[l], ln2_g[l], ln2_b[l], 1, dec_seq, dec_seq, tf)

    states = [jnp.stack(o, axis=1) for o in outs]
    return (xp.reshape(batch, seq, d), xs.reshape(dec_batch, dec_seq, d), *new_kv, *states)
```

```python
import functools
import math

import jax
import jax.numpy as jnp
import numpy as np
from jax import lax
from jax.experimental import pallas as pl
from jax.experimental.pallas import tpu as pltpu

F32 = jnp.float32
BF16 = jnp.bfloat16

DEPTH = 2
GRID_W = 64
A_HEADS = 6
NA_ROWS = 8
NA_COLS = 16
B_HEADS = 5
B_QK = 64
C_HEADS = 5
HEAD_DIM = 128
CONV_W = 3
ROPE_BASE = 10000.0
LN_EPS = 1e-5
RMS_EPS = 1e-6
ALPHA = (2 * DEPTH) ** 0.25

LANE = 128
SUBLANE = 8
BF16_ROWS = 16
VMEM_LIMIT_BYTES = 56 * 1024 * 1024

MLSTM_CHUNK = 256
HI = lax.Precision.HIGHEST


def _tile(n, target, unit=LANE):
    if n <= target:
        return n
    best = unit
    for t in range(unit, target + 1, unit):
        if n % t == 0:
            best = t
    assert n % best == 0, (n, target, unit)
    return best


def _cparams(*sem):
    return pltpu.CompilerParams(dimension_semantics=sem, vmem_limit_bytes=VMEM_LIMIT_BYTES)


def _dot(a, b):
    return jnp.dot(a, b, preferred_element_type=F32)


def _dot_nt(a, b, precision=None):
    return lax.dot_general(a, b, (((1,), (1,)), ((), ())), preferred_element_type=F32, precision=precision)


def _dot_tn(a, b):
    return lax.dot_general(a, b, (((0,), (0,)), ((), ())), preferred_element_type=F32)


def _mods_kernel(cond_ref, w_ref, b_ref, o_ref):
    c = cond_ref[...]
    s = c / (1.0 + jnp.exp(-c))
    o_ref[0] = _dot(s.astype(BF16), w_ref[0].astype(BF16)) + b_ref[0]


def _mods(cond, w_mod, b_mod):
    depth, d, n = w_mod.shape
    tn = _tile(n, 1536)
    return pl.pallas_call(
        _mods_kernel,
        grid=(depth, n // tn),
        in_specs=[pl.BlockSpec((SUBLANE, d), lambda l, j: (0, 0)),
                  pl.BlockSpec((1, d, tn), lambda l, j: (l, 0, j)),
                  pl.BlockSpec((1, 1, tn), lambda l, j: (l, 0, j))],
        out_specs=pl.BlockSpec((1, SUBLANE, tn), lambda l, j: (l, 0, j)),
        out_shape=jax.ShapeDtypeStruct((depth, SUBLANE, n), F32),
        compiler_params=_cparams("arbitrary", "arbitrary"),
        name="mods",
    )(cond, w_mod, b_mod.reshape(depth, 1, n))


def _inproj_kernel(x_ref, mod_ref, w_ref, o_ref, h_scr):
    @pl.when(pl.program_id(1) == 0)
    def _():
        sh = mod_ref[0, 0:1, :]
        sc = mod_ref[0, 1:2, :]
        h_scr[...] = (x_ref[...] * (1.0 + sc) + sh).astype(BF16)

    o_ref[...] = _dot(h_scr[...], w_ref[...])


def _inproj(x, mods_l, w_bf, layer, row_base, rows_per_cond):
    m, d = x.shape
    n = w_bf.shape[2]
    tm = _tile(m, 1024, SUBLANE)
    tm = min(tm, rows_per_cond)
    tn = _tile(n, 1152)
    per = rows_per_cond // tm
    return pl.pallas_call(
        _inproj_kernel,
        grid=(m // tm, n // tn),
        in_specs=[pl.BlockSpec((tm, d), lambda i, j: (i, 0)),
                  pl.BlockSpec((1, 6, d), lambda i, j: (row_base + i // per, 0, 0)),
                  pl.BlockSpec((None, d, tn), lambda i, j: (layer, 0, j))],
        out_specs=pl.BlockSpec((tm, tn), lambda i, j: (i, j)),
        out_shape=jax.ShapeDtypeStruct((m, n), F32),
        scratch_shapes=[pltpu.VMEM((tm, d), BF16)],
        compiler_params=_cparams("arbitrary", "arbitrary"),
        name="inproj",
    )(x, mods_l, w_bf)


def _lambda_value(lam_ref, lam_init):
    lf = lam_ref[...]
    t1 = jnp.sum(lf[0:1] * lf[1:2], axis=1, keepdims=True)
    t2 = jnp.sum(lf[2:3] * lf[3:4], axis=1, keepdims=True)
    return jnp.exp(t1) - jnp.exp(t2) + lam_init


def _split_maps(q):
    lane = lax.broadcasted_iota(jnp.int32, q.shape, 1)
    first = lane < B_QK
    return jnp.where(first, q, 0.0), jnp.where(first, 0.0, q)


def _subnorm(o, g_ref, lam_init):
    ms = jnp.mean(o * o, axis=-1, keepdims=True)
    return o * lax.rsqrt(ms + RMS_EPS) * g_ref[...] * (1.0 - lam_init)


def _ctx_attn_kernel(*refs, lam_init):
    qa_ref, ka_ref, va_ref = refs[0:3]
    b_refs = refs[3:3 + 3 * B_HEADS]
    lam_ref, subln_ref = refs[3 + 3 * B_HEADS:5 + 3 * B_HEADS]
    ya_ref, yb_ref, nak_ref, nav_ref, nbk_ref, nbv_ref = refs[-6:]
    a_scale = HEAD_DIM ** -0.5
    b_scale = B_QK ** -0.5
    for h in range(A_HEADS):
        sl = slice(h * HEAD_DIM, (h + 1) * HEAD_DIM)
        nak_ref[h] = ka_ref[:, sl]
        nav_ref[h] = va_ref[:, sl]
        q = qa_ref[:, sl].astype(BF16)
        k = ka_ref[:, sl].astype(BF16)
        v = va_ref[:, sl].astype(BF16)
        s = _dot_nt(q, k) * a_scale
        e = jnp.exp(s - jnp.max(s, axis=-1, keepdims=True))
        l = jnp.sum(e, axis=-1, keepdims=True)
        ya_ref[:, sl] = (_dot(e.astype(BF16), v) / l).astype(ya_ref.dtype)
    lam = _lambda_value(lam_ref, lam_init)
    for h in range(B_HEADS):
        sl = slice(h * HEAD_DIM, (h + 1) * HEAD_DIM)
        q1, q2 = _split_maps(b_refs[3 * h][...] * b_scale)
        nbk_ref[h] = b_refs[3 * h + 1][...]
        nbv_ref[h] = b_refs[3 * h + 2][...]
        k = b_refs[3 * h + 1][...].astype(BF16)
        v = b_refs[3 * h + 2][...].astype(BF16)
        s1 = _dot_nt(q1.astype(BF16), k)
        s2 = _dot_nt(q2.astype(BF16), k)
        e1 = jnp.exp(s1 - jnp.max(s1, axis=-1, keepdims=True))
        e2 = jnp.exp(s2 - jnp.max(s2, axis=-1, keepdims=True))
        p = e1 / jnp.sum(e1, axis=-1, keepdims=True) - lam * (e2 / jnp.sum(e2, axis=-1, keepdims=True))
        o = _dot(p.astype(BF16), v)
        yb_ref[:, sl] = _subnorm(o, subln_ref, lam_init).astype(yb_ref.dtype)


def _ctx_attn(p, batch, seq, b_lambda_l, b_subln_l, lam_init, layer, depth, carried):
    m = p.shape[0]
    aw = A_HEADS * HEAD_DIM
    nb = aw // HEAD_DIM
    in_specs = [pl.BlockSpec((seq, aw), lambda b: (b, 0)),
                pl.BlockSpec((seq, aw), lambda b: (b, 1)),
                pl.BlockSpec((seq, aw), lambda b: (b, 2))]
    args = [p, p, p]
    for h in range(B_HEADS):
        for seg in range(3):
            col = 3 * nb + seg * B_HEADS + h
            in_specs.append(pl.BlockSpec((seq, HEAD_DIM), functools.partial(lambda b, c: (b, c), c=col)))
            args.append(p)
    in_specs += [pl.BlockSpec((4, B_QK), lambda b: (0, 0)), pl.BlockSpec((1, HEAD_DIM), lambda b: (0, 0))]
    args += [b_lambda_l, b_subln_l.reshape(1, HEAD_DIM)]
    aliases = {}
    if carried is not None:
        for j, buf in enumerate(carried):
            aliases[len(args)] = 2 + j
            in_specs.append(pl.BlockSpec(memory_space=pl.ANY))
            args.append(buf)
    kv_spec = lambda nh: pl.BlockSpec((None, None, nh, seq, HEAD_DIM), lambda b: (b, layer, 0, 0, 0))
    kv_shape = lambda nh: jax.ShapeDtypeStruct((batch, depth, nh, seq, HEAD_DIM), F32)
    out = pl.pallas_call(
        functools.partial(_ctx_attn_kernel, lam_init=lam_init),
        grid=(batch,),
        in_specs=in_specs,
        out_specs=[pl.BlockSpec((seq, aw), lambda b: (b, 0)),
                   pl.BlockSpec((seq, B_HEADS * HEAD_DIM), lambda b: (b, 0)),
                   kv_spec(A_HEADS), kv_spec(A_HEADS), kv_spec(B_HEADS), kv_spec(B_HEADS)],
        out_shape=[jax.ShapeDtypeStruct((m, aw), BF16),
                   jax.ShapeDtypeStruct((m, B_HEADS * HEAD_DIM), BF16),
                   kv_shape(A_HEADS), kv_shape(A_HEADS), kv_shape(B_HEADS), kv_shape(B_HEADS)],
        input_output_aliases=aliases,
        compiler_params=_cparams("arbitrary"),
        name="ctx_attn",
    )(*args)
    return out[0], out[1], tuple(out[2:])


def _na_bias_table(rpb, n_rows):
    wr = min(NA_ROWS, n_rows)
    cols = np.arange(GRID_W)
    cs = np.clip(cols - NA_COLS // 2, 0, GRID_W - NA_COLS)
    col_mask = (cols[None, :] >= cs[:, None]) & (cols[None, :] < cs[:, None] + NA_COLS)
    col_idx = np.clip(cols[None, :] - cols[:, None] + NA_COLS - 1, 0, 2 * NA_COLS - 2)
    onehot = (col_idx[None] == np.arange(2 * NA_COLS - 1)[:, None, None]).astype(np.float32)
    toe = jnp.einsum('hrd,dqk->hqrk', rpb.astype(F32), jnp.asarray(onehot), precision=HI)
    toe = jnp.where(jnp.asarray(col_mask)[None, :, None, :], toe, -jnp.inf)
    tabs = [toe[:, :, NA_ROWS - 1 - e:NA_ROWS - 1 - e + wr, :].reshape(rpb.shape[0], GRID_W, wr * GRID_W)
            for e in range(wr)]
    return jnp.stack(tabs, axis=1)


def _na_kernel(q_ref, k_ref, v_ref, kc_ref, vc_ref, bias_ref, o_ref, kb_scr, vb_scr, kcb_scr, vcb_scr, *, n_rows, wr):
    scale = HEAD_DIM ** -0.5
    kb_scr[...] = k_ref[...].astype(BF16)
    vb_scr[...] = v_ref[...].astype(BF16)
    kcb_scr[...] = kc_ref[...].astype(BF16)
    vcb_scr[...] = vc_ref[...].astype(BF16)

    def body(r, carry):
        rs = jnp.clip(r - wr // 2, 0, n_rows - wr)
        q = q_ref[pl.ds(pl.multiple_of(r * GRID_W, GRID_W), GRID_W), :].astype(BF16)
        w0 = pl.multiple_of(rs * GRID_W, GRID_W)
        kw = kb_scr[pl.ds(w0, wr * GRID_W), :]
        vw = vb_scr[pl.ds(w0, wr * GRID_W), :]
        s_nb = _dot_nt(q, kw) * scale + bias_ref[r - rs]
        s_c = _dot_nt(q, kcb_scr[...]) * scale
        mx = jnp.maximum(jnp.max(s_nb, axis=-1, keepdims=True), jnp.max(s_c, axis=-1, keepdims=True))
        p_nb = jnp.exp(s_nb - mx)
        p_c = jnp.exp(s_c - mx)
        l = jnp.sum(p_nb, axis=-1, keepdims=True) + jnp.sum(p_c, axis=-1, keepdims=True)
        o = (_dot(p_nb.astype(BF16), vw) + _dot(p_c.astype(BF16), vcb_scr[...])) / l
        o_ref[pl.ds(pl.multiple_of(r * GRID_W, GRID_W), GRID_W), :] = o.astype(o_ref.dtype)
        return carry

    lax.fori_loop(0, n_rows, body, 0, unroll=4)


def _na_attn(p, batch, seq, cache_k, cache_v, layer, bias):
    n_rows = seq // GRID_W
    wr = min(NA_ROWS, n_rows)
    past = cache_k.shape[3]
    blk = lambda off: pl.BlockSpec((seq, HEAD_DIM), lambda b, h: (b, off + h))
    cspec = pl.BlockSpec((None, None, None, past, HEAD_DIM), lambda b, h: (b, layer, h, 0, 0))
    return pl.pallas_call(
        functools.partial(_na_kernel, n_rows=n_rows, wr=wr),
        grid=(batch, A_HEADS),
        in_specs=[blk(0), blk(A_HEADS), blk(2 * A_HEADS), cspec, cspec,
                  pl.BlockSpec((None, wr, GRID_W, wr * GRID_W), lambda b, h: (h, 0, 0, 0))],
        out_specs=pl.BlockSpec((seq, HEAD_DIM), lambda b, h: (b, h)),
        out_shape=jax.ShapeDtypeStruct((batch * seq, A_HEADS * HEAD_DIM), BF16),
        scratch_shapes=[pltpu.VMEM((seq, HEAD_DIM), BF16), pltpu.VMEM((seq, HEAD_DIM), BF16),
                        pltpu.VMEM((past, HEAD_DIM), BF16), pltpu.VMEM((past, HEAD_DIM), BF16)],
        compiler_params=_cparams("arbitrary", "arbitrary"),
        name="na_attn",
    )(p, p, p, cache_k, cache_v, bias)


def _rope_tables(seq):
    t = np.arange(seq)
    rows = (t // GRID_W).astype(np.float32)
    cols = (t % GRID_W).astype(np.float32)
    half = B_QK // 2
    freqs = jnp.asarray(ROPE_BASE, F32) ** (-jnp.arange(0, half, 2, dtype=F32) / half)
    ar = jnp.asarray(rows)[:, None] * freqs
    ac = jnp.asarray(cols)[:, None] * freqs
    cr, sr, cc, sc = jnp.cos(ar), jnp.sin(ar), jnp.cos(ac), jnp.sin(ac)
    cos = jnp.concatenate([cr, cr, cc, cc] * 2, axis=-1)
    sin = jnp.concatenate([-sr, sr, -sc, sc] * 2, axis=-1)
    return cos, sin


def _rope(x, cos, sin):
    q = B_QK // 4
    lane = lax.broadcasted_iota(jnp.int32, x.shape, 1)
    first = (lane & (2 * q - 1)) < q
    partner = jnp.where(first, pltpu.roll(x, LANE - q, 1), pltpu.roll(x, q, 1))
    return x * cos + partner * sin


def _diff_lat_kernel(q_ref, k_ref, v_ref, kc_ref, vc_ref, cosq_ref, sinq_ref, cosk_ref, sink_ref, lam_ref, subln_ref,
                     o_ref, kr_scr, kcb_scr, vb_scr, vcb_scr, *, lam_init):
    scale = B_QK ** -0.5

    @pl.when(pl.program_id(2) == 0)
    def _():
        kr_scr[...] = _rope(k_ref[...], cosk_ref[...], sink_ref[...]).astype(BF16)
        kcb_scr[...] = kc_ref[...].astype(BF16)
        vb_scr[...] = v_ref[...].astype(BF16)
        vcb_scr[...] = vc_ref[...].astype(BF16)

    lam = _lambda_value(lam_ref, lam_init)
    q = q_ref[...] * scale
    qr1, qr2 = _split_maps(_rope(q, cosq_ref[...], sinq_ref[...]))
    q1, q2 = _split_maps(q)

    def probs(qr, qp):
        s_l = _dot_nt(qr.astype(BF16), kr_scr[...])
        s_c = _dot_nt(qp.astype(BF16), kcb_scr[...])
        mx = jnp.maximum(jnp.max(s_l, axis=-1, keepdims=True), jnp.max(s_c, axis=-1, keepdims=True))
        e_l = jnp.exp(s_l - mx)
        e_c = jnp.exp(s_c - mx)
        inv = 1.0 / (jnp.sum(e_l, axis=-1, keepdims=True) + jnp.sum(e_c, axis=-1, keepdims=True))
        return e_l * inv, e_c * inv

    p1_l, p1_c = probs(qr1, q1)
    p2_l, p2_c = probs(qr2, q2)
    o = (_dot((p1_l - lam * p2_l).astype(BF16), vb_scr[...])
         + _dot((p1_c - lam * p2_c).astype(BF16), vcb_scr[...]))
    o_ref[...] = _subnorm(o, subln_ref, lam_init).astype(o_ref.dtype)


def _diff_lat_attn(p, batch, seq, cache_k, cache_v, layer, cos, sin, b_lambda_l, b_subln_l, lam_init):
    past = cache_k.shape[3]
    tq = _tile(seq, 256, SUBLANE)
    nq = seq // tq
    base = 3 * A_HEADS
    cspec = pl.BlockSpec((None, None, None, past, HEAD_DIM), lambda b, h, i: (b, layer, h, 0, 0))
    kv = lambda off: pl.BlockSpec((seq, HEAD_DIM), lambda b, h, i: (b, off + h))
    return pl.pallas_call(
        functools.partial(_diff_lat_kernel, lam_init=lam_init),
        grid=(batch, B_HEADS, nq),
        in_specs=[pl.BlockSpec((tq, HEAD_DIM), lambda b, h, i: (b * nq + i, base + h)),
                  kv(base + B_HEADS), kv(base + 2 * B_HEADS), cspec, cspec,
                  pl.BlockSpec((tq, HEAD_DIM), lambda b, h, i: (i, 0)),
                  pl.BlockSpec((tq, HEAD_DIM), lambda b, h, i: (i, 0)),
                  pl.BlockSpec((seq, HEAD_DIM), lambda b, h, i: (0, 0)),
                  pl.BlockSpec((seq, HEAD_DIM), lambda b, h, i: (0, 0)),
                  pl.BlockSpec((4, B_QK), lambda b, h, i: (0, 0)),
                  pl.BlockSpec((1, HEAD_DIM), lambda b, h, i: (0, 0))],
        out_specs=pl.BlockSpec((tq, HEAD_DIM), lambda b, h, i: (b * nq + i, h)),
        out_shape=jax.ShapeDtypeStruct((batch * seq, B_HEADS * HEAD_DIM), BF16),
        scratch_shapes=[pltpu.VMEM((seq, HEAD_DIM), BF16), pltpu.VMEM((past, HEAD_DIM), BF16),
                        pltpu.VMEM((seq, HEAD_DIM), BF16), pltpu.VMEM((past, HEAD_DIM), BF16)],
        compiler_params=_cparams("arbitrary", "arbitrary", "arbitrary"),
        name="diff_lat_attn",
    )(p, p, p, cache_k, cache_v, cos, sin, cos, sin, b_lambda_l, b_subln_l.reshape(1, HEAD_DIM))


def _log_sigmoid(x):
    return jnp.minimum(x, 0.0) - jnp.log1p(jnp.exp(-jnp.abs(x)))


def _split3(x):
    hi = x.astype(BF16)
    r = x - hi.astype(F32)
    mid = r.astype(BF16)
    lo = (r - mid.astype(F32)).astype(BF16)
    return hi, mid, lo


def _mlstm_kernel(*refs, chunk, n_chunks):
    nh = C_HEADS
    q_refs, k_refs, v_refs, og_refs = (refs[i * nh:(i + 1) * nh] for i in range(4))
    (g_ref, gb_ref, cn_ref, c0_ref, n0_ref, m0_ref,
     y_ref, cf_ref, nf_ref, mf_ref, hf_scr, hb_scr) = refs[4 * nh:]
    scale = HEAD_DIM ** -0.5
    ln = chunk
    n_sel = 4 * SUBLANE
    assert 4 * nh <= n_sel
    sel = jnp.where(lax.broadcasted_iota(jnp.int32, (n_sel, LANE), 0)
                    == lax.broadcasted_iota(jnp.int32, (n_sel, LANE), 1), 1.0, 0.0).astype(BF16)
    ti = lax.broadcasted_iota(jnp.int32, (ln, ln), 0)
    si = lax.broadcasted_iota(jnp.int32, (ln, ln), 1)
    lower = si <= ti
    upper = si >= ti
    lower_b = jnp.where(lower, 1.0, 0.0).astype(BF16)
    upper_b = jnp.where(upper, 1.0, 0.0).astype(BF16)

    def gate_terms(r0, directions):
        g = g_ref[pl.ds(r0, ln), :] + gb_ref[...]
        gp = _split3(g)
        rows = sum(_dot_nt(sel, p) for p in gp)
        cols = sum(_dot_nt(p, sel) for p in gp)
        lf_rows = _split3(_log_sigmoid(rows))
        lf_cols = _split3(_log_sigmoid(cols))
        t = dict(rows=rows, cols=cols)
        if 0 in directions:
            t["row0"] = sum(_dot(p, upper_b) for p in lf_rows)
            t["col0"] = sum(_dot(lower_b, p) for p in lf_cols)
        if 1 in directions:
            t["row1"] = sum(_dot(p, lower_b) for p in lf_rows)
            t["col1"] = sum(_dot(upper_b, p) for p in lf_cols)
        return t

    def head_terms(h, r0):
        qf = q_refs[h][pl.ds(r0, ln), :] * scale
        kf = k_refs[h][pl.ds(r0, ln), :]
        qb = qf.astype(BF16)
        vb = v_refs[h][pl.ds(r0, ln), :].astype(BF16)
        return dict(qf=qf, kf=kf, qb=qb, vb=vb, qk=_dot_nt(qb, kf.astype(BF16)))

    def chunk_step(gt, t, h, state, backward):
        c_st, n_st, m_st = state
        rows, cols, qf, kf, qb, vb = gt["rows"], gt["cols"], t["qf"], t["kf"], t["qb"], t["vb"]
        gi = (2 * nh if backward else 0) + h
        gf = gi + nh
        i_row, i_col = rows[gi:gi + 1], cols[:, gi:gi + 1]
        if backward:
            b_row, b_col, mask = gt["row1"][gf:gf + 1], gt["col1"][:, gf:gf + 1], upper
            b_last = b_col[0:1]
        else:
            b_row, b_col, mask = gt["row0"][gf:gf + 1], gt["col0"][:, gf:gf + 1], lower
            b_last = b_col[ln - 1:ln]
        d = jnp.where(mask, b_col - b_row + i_row, -jnp.inf)
        inter = b_col + m_st
        m_row = jnp.maximum(jnp.max(d, axis=-1, keepdims=True), inter)
        w_intra = jnp.exp(d - m_row)
        w_state = jnp.exp(inter - m_row)
        sc = t["qk"] * w_intra
        num = w_state * _dot(qb, c_st.astype(BF16)) + _dot(sc.astype(BF16), vb)
        den = w_state * jnp.sum(qf * n_st, axis=-1, keepdims=True) + jnp.sum(sc, axis=-1, keepdims=True)
        h_out = num / jnp.maximum(jnp.abs(den), jnp.exp(-m_row))
        gg = b_last - b_col + i_col
        m_new = jnp.maximum(b_last + m_st, jnp.max(gg, axis=0, keepdims=True))
        w_old = jnp.exp(b_last + m_st - m_new)
        w_s = jnp.exp(gg - m_new)
        kw = kf * w_s
        c_new = w_old * c_st + _dot_tn(kw.astype(BF16), vb)
        n_new = w_old * n_st + jnp.sum(kw, axis=0, keepdims=True)
        return h_out, (c_new, n_new, m_new)

    def start(c):
        return pl.multiple_of(c * ln, ln)

    def finish(h, r0, hs):
        ms = jnp.mean(hs * hs, axis=-1, keepdims=True)
        hn = hs * lax.rsqrt(ms + RMS_EPS) * cn_ref[...]
        og = og_refs[h][pl.ds(r0, ln), :]
        y_ref[pl.ds(r0, ln), h * HEAD_DIM:(h + 1) * HEAD_DIM] = (hn / (1.0 + jnp.exp(-og))).astype(y_ref.dtype)

    init = tuple(tuple((c0_ref[dr, h], n0_ref[dr, h], m0_ref[dr, h][:, 0:1]) for h in range(nh)) for dr in (0, 1))
    if n_chunks == 1:
        gt = gate_terms(0, (0, 1))
        st_f, st_b = [], []
        for h in range(nh):
            t = head_terms(h, 0)
            h_f, s_f = chunk_step(gt, t, h, init[0][h], False)
            h_b, s_b = chunk_step(gt, t, h, init[1][h], True)
            finish(h, 0, h_f + h_b)
            st_f.append(s_f)
            st_b.append(s_b)
        final = (st_f, st_b)
    else:
        def body(j, states):
            r_f = start(j)
            r_b = start(n_chunks - 1 - j)
            gt_f = gate_terms(r_f, (0,))
            gt_b = gate_terms(r_b, (1,))
            st_f, st_b = [], []
            for h in range(nh):
                sl = slice(h * HEAD_DIM, (h + 1) * HEAD_DIM)
                h_f, s_f = chunk_step(gt_f, head_terms(h, r_f), h, states[0][h], False)
                h_b, s_b = chunk_step(gt_b, head_terms(h, r_b), h, states[1][h], True)
                hf_scr[pl.ds(r_f, ln), sl] = h_f
                hb_scr[pl.ds(r_b, ln), sl] = h_b
                st_f.append(s_f)
                st_b.append(s_b)
            return tuple(st_f), tuple(st_b)

        final = lax.fori_loop(0, n_chunks, body, init)

        def fin_body(j, carry):
            r0 = start(j)
            for h in range(nh):
                sl = slice(h * HEAD_DIM, (h + 1) * HEAD_DIM)
                finish(h, r0, hf_scr[pl.ds(r0, ln), sl] + hb_scr[pl.ds(r0, ln), sl])
            return carry

        lax.fori_loop(0, n_chunks, fin_body, 0)
    for dr in (0, 1):
        for h in range(nh):
            c_f, n_f, m_f = final[dr][h]
            cf_ref[dr, h] = c_f
            nf_ref[dr, h] = n_f
            mf_ref[dr, h] = jnp.broadcast_to(m_f, (1, LANE))


def _mlstm(p, batch, seq, gate_bias, c_norm_l, c0, n0, m0):
    chunk = min(MLSTM_CHUNK, seq)
    n_chunks = seq // chunk
    base = 3 * A_HEADS + 3 * B_HEADS
    hd = HEAD_DIM
    nh = C_HEADS
    head_bytes = 4 * nh * seq * hd * 4
    mode = dict(pipeline_mode=pl.Buffered(1)) if 2 * head_bytes > VMEM_LIMIT_BYTES // 2 else {}
    blk = lambda col: pl.BlockSpec((seq, hd), functools.partial(lambda b, c: (b, c), c=col), **mode)
    st_c = pl.BlockSpec((None, 2, nh, hd, hd), lambda b: (b, 0, 0, 0, 0))
    st_n = pl.BlockSpec((None, 2, nh, 1, hd), lambda b: (b, 0, 0, 0, 0))
    n0r = n0.reshape(batch, 2, nh, 1, hd)
    m0r = jnp.broadcast_to(m0[..., None, None], (batch, 2, nh, 1, LANE))
    y, c_f, n_f, m_f = pl.pallas_call(
        functools.partial(_mlstm_kernel, chunk=chunk, n_chunks=n_chunks),
        grid=(batch,),
        in_specs=[blk(base + seg * nh + h) for seg in range(4) for h in range(nh)]
                 + [pl.BlockSpec((seq, LANE), lambda b: (b, base + 4 * nh)),
                    pl.BlockSpec((1, LANE), lambda b: (0, 0)),
                    pl.BlockSpec((1, hd), lambda b: (0, 0)),
                    st_c, st_n, st_n],
        out_specs=[pl.BlockSpec((seq, nh * hd), lambda b: (b, 0)), st_c, st_n, st_n],
        out_shape=[jax.ShapeDtypeStruct((batch * seq, nh * hd), BF16),
                   jax.ShapeDtypeStruct((batch, 2, nh, hd, hd), F32),
                   jax.ShapeDtypeStruct((batch, 2, nh, 1, hd), F32),
                   jax.ShapeDtypeStruct((batch, 2, nh, 1, LANE), F32)],
        scratch_shapes=[pltpu.VMEM((seq, nh * hd), F32), pltpu.VMEM((seq, nh * hd), F32)],
        compiler_params=_cparams("arbitrary"),
        name="mlstm",
    )(*([p] * (4 * nh + 1)), gate_bias, c_norm_l.reshape(1, hd), c0, n0r, m0r)
    return y, c_f, n_f.reshape(batch, 2, C_HEADS, hd), m_f[:, :, :, 0, 0]


def _layernorm(z, g_ref, b_ref):
    mu = jnp.mean(z, axis=-1, keepdims=True)
    zc = z - mu
    var = jnp.mean(zc * zc, axis=-1, keepdims=True)
    return zc * lax.rsqrt(var + LN_EPS) * g_ref[...] + b_ref[...]


def _outproj_kernel(ya_ref, yb_ref, yc_ref, w_ref, x_ref, mod_ref, g_ref, b_ref, x1_ref, h2_ref):
    ycat = jnp.concatenate([ya_ref[...], yb_ref[...], yc_ref[...]], axis=-1)
    y = _dot(ycat, w_ref[...])
    gate = mod_ref[0, 2:3, :]
    x1 = _layernorm(ALPHA * x_ref[...] + gate * y, g_ref, b_ref)
    x1_ref[...] = x1
    h2_ref[...] = (x1 * (1.0 + mod_ref[0, 4:5, :]) + mod_ref[0, 3:4, :]).astype(BF16)


def _outproj(ya, yb, yc, w_bf, layer, x, mods_l, ln_g, ln_b, row_base, rows_per_cond):
    m, d = x.shape
    tm = min(_tile(m, 512, SUBLANE), rows_per_cond)
    per = rows_per_cond // tm
    row = lambda w: pl.BlockSpec((tm, w), lambda i: (i, 0))
    vec = pl.BlockSpec((1, d), lambda i: (0, 0))
    return pl.pallas_call(
        _outproj_kernel,
        grid=(m // tm,),
        in_specs=[row(ya.shape[1]), row(yb.shape[1]), row(yc.shape[1]),
                  pl.BlockSpec((None, d, d), lambda i: (layer, 0, 0)), row(d),
                  pl.BlockSpec((1, 6, d), lambda i: (row_base + i // per, 0, 0)), vec, vec],
        out_specs=[row(d), row(d)],
        out_shape=[jax.ShapeDtypeStruct((m, d), F32), jax.ShapeDtypeStruct((m, d), BF16)],
        compiler_params=_cparams("arbitrary"),
        name="outproj_ln",
    )(ya, yb, yc, w_bf, x, mods_l, ln_g.reshape(1, d), ln_b.reshape(1, d))


def _ffn_kernel(h_ref, hp_ref, hn_ref, wa_ref, wg_ref, cwa_ref, cwg_ref, cba_ref, cbg_ref, wd_ref, x_ref, mod_ref,
                g_ref, b_ref, o_ref, hext_scr, acc_scr, *, seq):
    i = pl.program_id(0)
    f = pl.program_id(1)
    tm = h_ref.shape[0]
    halo = BF16_ROWS

    @pl.when(f == 0)
    def _():
        hext_scr[0:halo, :] = hp_ref[...]
        hext_scr[halo:halo + tm, :] = h_ref[...]
        hext_scr[halo + tm:, :] = hn_ref[...]
        acc_scr[...] = jnp.zeros_like(acc_scr)

    pos = (i * tm + lax.broadcasted_iota(jnp.int32, (tm, 1), 0)) & (seq - 1)
    has_prev = pos != 0
    has_next = pos != seq - 1

    def up_conv(w_ref, cw_ref, cb_ref):
        u = _dot(hext_scr[...], w_ref[...])
        rows = u.shape[0]
        u_prev = jnp.where(has_prev, pltpu.roll(u, 1, 0)[halo:halo + tm], 0.0)
        u_next = jnp.where(has_next, pltpu.roll(u, rows - 1, 0)[halo:halo + tm], 0.0)
        return u_prev * cw_ref[0:1, :] + u[halo:halo + tm] * cw_ref[1:2, :] + u_next * cw_ref[2:3, :] + cb_ref[...]

    a = up_conv(wa_ref, cwa_ref, cba_ref)
    g = up_conv(wg_ref, cwg_ref, cbg_ref)
    act = (g / (1.0 + jnp.exp(-g))) * a
    acc_scr[...] += _dot(act.astype(BF16), wd_ref[...])

    @pl.when(f == pl.num_programs(1) - 1)
    def _():
        z = ALPHA * x_ref[...] + mod_ref[0, 5:6, :] * acc_scr[...]
        o_ref[...] = _layernorm(z, g_ref, b_ref)


def _ffn(h2, x1, ffn_w, layer, mods_l, ln_g, ln_b, row_base, rows_per_cond, seq, tf):
    wu_bf, cw, cb, wd_bf = ffn_w
    m, d = x1.shape
    fp = wd_bf.shape[1]
    tm = min(_tile(m, 512, SUBLANE), rows_per_cond)
    per = rows_per_cond // tm
    assert seq & (seq - 1) == 0 and m % seq == 0
    nh = m // BF16_ROWS
    hb = tm // BF16_ROWS
    vec = pl.BlockSpec((1, d), lambda i, f: (0, 0))
    wcol = lambda rows, half: pl.BlockSpec((None, None, rows, tf),
                                           functools.partial(lambda i, f, hf: (layer, hf, 0, f), hf=half))
    return pl.pallas_call(
        functools.partial(_ffn_kernel, seq=seq),
        grid=(m // tm, fp // tf),
        in_specs=[pl.BlockSpec((tm, d), lambda i, f: (i, 0)),
                  pl.BlockSpec((BF16_ROWS, d), lambda i, f: (jnp.maximum(i * hb - 1, 0), 0)),
                  pl.BlockSpec((BF16_ROWS, d), lambda i, f: (jnp.minimum((i + 1) * hb, nh - 1), 0)),
                  wcol(d, 0), wcol(d, 1), wcol(CONV_W, 0), wcol(CONV_W, 1), wcol(1, 0), wcol(1, 1),
                  pl.BlockSpec((None, tf, d), lambda i, f: (layer, f, 0)),
                  pl.BlockSpec((tm, d), lambda i, f: (i, 0)),
                  pl.BlockSpec((1, 6, d), lambda i, f: (row_base + i // per, 0, 0)), vec, vec],
        out_specs=pl.BlockSpec((tm, d), lambda i, f: (i, 0)),
        out_shape=jax.ShapeDtypeStruct((m, d), F32),
        scratch_shapes=[pltpu.VMEM((tm + 2 * BF16_ROWS, d), BF16), pltpu.VMEM((tm, d), F32)],
        compiler_params=_cparams("arbitrary", "arbitrary"),
        name="ffn_ln",
    )(h2, h2, h2, wu_bf, wu_bf, cw, cw, cb, cb, wd_bf, x1, mods_l, ln_g.reshape(1, d), ln_b.reshape(1, d))


def _pad_cols(w, n):
    return jnp.pad(w, [(0, 0)] * (w.ndim - 1) + [(0, n - w.shape[-1])])


def _cast_kernel(x_ref, o_ref, *, rows_valid, cols_valid):
    tr, tc = x_ref.shape
    x = x_ref[...]
    if rows_valid is not None:
        r = pl.program_id(1) * tr + lax.broadcasted_iota(jnp.int32, (tr, tc), 0)
        x = jnp.where(r < rows_valid, x, 0.0)
    if cols_valid is not None:
        c = pl.program_id(2) * tc + lax.broadcasted_iota(jnp.int32, (tr, tc), 1)
        x = jnp.where(c < cols_valid, x, 0.0)
    o_ref[:, :tc] = x.astype(BF16)
    if o_ref.shape[1] > tc:
        o_ref[:, tc:] = jnp.zeros((tr, o_ref.shape[1] - tc), BF16)


def _cast_weights(w, tr, tc, rows_out, cols_out):
    depth, r, c = w.shape
    nr, nc = -(-rows_out // tr), -(-cols_out // tc)
    assert nr * tr == rows_out and nc * tc == cols_out
    return pl.pallas_call(
        functools.partial(_cast_kernel, rows_valid=r if rows_out > r else None, cols_valid=c if cols_out > c else None),
        grid=(depth, nr, nc),
        in_specs=[pl.BlockSpec((None, tr, tc), lambda l, i, j: (l, i, j))],
        out_specs=pl.BlockSpec((None, tr, tc), lambda l, i, j: (l, i, j)),
        out_shape=jax.ShapeDtypeStruct((depth, rows_out, cols_out), BF16),
        compiler_params=_cparams("arbitrary", "arbitrary", "arbitrary"),
        name="cast_weights",
    )(w)


def _cast_up_weights(w_up, dff, fp, tr):
    depth, d, _ = w_up.shape
    return pl.pallas_call(
        functools.partial(_cast_kernel, rows_valid=None, cols_valid=None),
        grid=(depth, d // tr, 2),
        in_specs=[pl.BlockSpec((None, tr, dff), lambda l, i, j: (l, i, j))],
        out_specs=pl.BlockSpec((None, None, tr, fp), lambda l, i, j: (l, j, i, 0)),
        out_shape=jax.ShapeDtypeStruct((depth, 2, d, fp), BF16),
        compiler_params=_cparams("arbitrary", "arbitrary", "arbitrary"),
        name="cast_up_weights",
    )(w_up)


def _prep_weights(w_in, w_out, w_up, conv_w, conv_b, w_down, tf):
    depth, d, n_in = w_in.shape
    n_pad = -(-n_in // LANE) * LANE
    dff = w_down.shape[1]
    fp = -(-dff // tf) * tf
    w_in_bf = _cast_weights(w_in, _tile(d, 512, SUBLANE), _tile(n_pad, 1152), d, n_pad)
    w_out_bf = _cast_weights(w_out, _tile(d, 512, SUBLANE), d, d, d)
    w_up_bf = _cast_up_weights(w_up, dff, fp, _tile(d, 256, SUBLANE))
    w_down_bf = _cast_weights(w_down, tf, d, fp, d)
    halves = lambda t: jnp.stack([_pad_cols(t[..., :dff], fp), _pad_cols(t[..., dff:], fp)], axis=1)
    return w_in_bf, w_out_bf, w_up_bf, halves(conv_w), halves(conv_b[:, None, :]), w_down_bf


def _gate_bias_row(c_gate_b_l):
    return _pad_cols(c_gate_b_l.reshape(1, -1), LANE)


def kernel(x_prompt, x_sample, cache_a_k, cache_a_v, cache_b_k, cache_b_v, state_c_C, state_c_n, state_c_m, c, c_ctx, w_mod, b_mod, w_in, c_gate_b, a_rpb, b_lambda, b_subln, c_norm, w_out, ln1_g, ln1_b, ln2_g, ln2_b, w_up, conv_w, conv_b, w_down):
    batch, seq, d = x_prompt.shape
    dec_batch, dec_seq, _ = x_sample.shape
    depth = w_in.shape[0]
    tf = 512

    cond = jnp.concatenate([c_ctx[None, :], c], axis=0)
    cond = jnp.pad(cond, ((0, SUBLANE - cond.shape[0]), (0, 0)))
    mods = _mods(cond, w_mod, b_mod).reshape(depth, SUBLANE, 6, d)

    xp = x_prompt.reshape(batch * seq, d)
    xs = x_sample.reshape(dec_batch * dec_seq, d)
    cos, sin = _rope_tables(dec_seq)
    zeros_c = jnp.zeros((batch, 2, C_HEADS, HEAD_DIM, HEAD_DIM), F32)
    zeros_n = jnp.zeros((batch, 2, C_HEADS, HEAD_DIM), F32)
    zeros_m = jnp.zeros((batch, 2, C_HEADS), F32)

    w_in_bf, w_out_bf, *ffn_w = _prep_weights(w_in, w_out, w_up, conv_w, conv_b, w_down, tf)
    new_kv = None
    outs = [[] for _ in range(3)]
    for l in range(depth):
        lam_init = 0.8 - 0.6 * math.exp(-0.3 * l)
        gate_bias = _gate_bias_row(c_gate_b[l])

        pp = _inproj(xp, mods[l], w_in_bf, l, 0, batch * seq)
        ya, yb, new_kv = _ctx_attn(pp, batch, seq, b_lambda[l], b_subln[l], lam_init, l, depth, new_kv)
        yc, c_f, n_f, m_f = _mlstm(pp, batch, seq, gate_bias, c_norm[l], zeros_c, zeros_n, zeros_m)
        outs[0].append(c_f)
        outs[1].append(n_f)
        outs[2].append(m_f)
        x1, h2 = _outproj(ya, yb, yc, w_out_bf, l, xp, mods[l], ln1_g[l], ln1_b[l], 0, batch * seq)
        xp = _ffn(h2, x1, ffn_w, l, mods[l], ln2_g[l], ln2_b[l], 0, batch * seq, seq, tf)

        ps = _inproj(xs, mods[l], w_in_bf, l, 1, dec_seq)
        ya = _na_attn(ps, dec_batch, dec_seq, cache_a_k, cache_a_v, l, _na_bias_table(a_rpb[l], dec_seq // GRID_W))
        yb = _diff_lat_attn(ps, dec_batch, dec_seq, cache_b_k, cache_b_v, l, cos, sin, b_lambda[l], b_subln[l], lam_init)
        yc, _, _, _ = _mlstm(ps, dec_batch, dec_seq, gate_bias, c_norm[l],
                             state_c_C[:, l], state_c_n[:, l], state_c_m[:, l])
        x1, h2 = _outproj(ya, yb, yc, w_out_bf, l, xs, mods[l], ln1_g[l], ln1_b[l], 1, dec_seq)
        xs = _ffn(h2, x1, ffn_w, l, mods[l], ln2_g[l], ln2_b[l], 1, dec_seq, dec_seq, tf)

    states = [jnp.stack(o, axis=1) for o in outs]
    return (xp.reshape(batch, seq, d), xs.reshape(dec_batch, dec_seq, d), *new_kv, *states)
```

```python
import functools
import math

import jax
import jax.numpy as jnp
import numpy as np
from jax import lax
from jax.experimental import pallas as pl
from jax.experimental.pallas import tpu as pltpu

F32 = jnp.float32
BF16 = jnp.bfloat16

DEPTH = 2
GRID_W = 64
A_HEADS = 6
NA_ROWS = 8
NA_COLS = 16
NA_QROWS = 4
B_HEADS = 5
B_QK = 64
C_HEADS = 5
HEAD_DIM = 128
CONV_W = 3
ROPE_BASE = 10000.0
LN_EPS = 1e-5
RMS_EPS = 1e-6
ALPHA = (2 * DEPTH) ** 0.25
NEG_LOG2_E = -math.log2(math.e)

LANE = 128
SUBLANE = 8
BF16_ROWS = 16
VMEM_LIMIT_BYTES = 56 * 1024 * 1024

MLSTM_CHUNK = 256
HI = lax.Precision.HIGHEST


def _tile(n, target, unit=LANE):
    if n <= target:
        return n
    best = unit
    for t in range(unit, target + 1, unit):
        if n % t == 0:
            best = t
    assert n % best == 0, (n, target, unit)
    return best


def _cparams(*sem):
    return pltpu.CompilerParams(dimension_semantics=sem, vmem_limit_bytes=VMEM_LIMIT_BYTES)


def _dot(a, b):
    return jnp.dot(a, b, preferred_element_type=F32)


def _dot_nt(a, b, precision=None):
    return lax.dot_general(a, b, (((1,), (1,)), ((), ())), preferred_element_type=F32, precision=precision)


def _dot_tn(a, b):
    return lax.dot_general(a, b, (((0,), (0,)), ((), ())), preferred_element_type=F32)


def _mods_kernel(cond_ref, w_ref, b_ref, o_ref):
    c = cond_ref[...]
    s = c / (1.0 + jnp.exp(-c))
    o_ref[0] = _dot(s.astype(BF16), w_ref[0].astype(BF16)) + b_ref[0]


def _mods(cond, w_mod, b_mod):
    depth, d, n = w_mod.shape
    tn = _tile(n, 1536)
    return pl.pallas_call(
        _mods_kernel,
        grid=(depth, n // tn),
        in_specs=[pl.BlockSpec((SUBLANE, d), lambda l, j: (0, 0)),
                  pl.BlockSpec((1, d, tn), lambda l, j: (l, 0, j)),
                  pl.BlockSpec((1, 1, tn), lambda l, j: (l, 0, j))],
        out_specs=pl.BlockSpec((1, SUBLANE, tn), lambda l, j: (l, 0, j)),
        out_shape=jax.ShapeDtypeStruct((depth, SUBLANE, n), F32),
        compiler_params=_cparams("arbitrary", "arbitrary"),
        name="mods",
    )(cond, w_mod, b_mod.reshape(depth, 1, n))


def _inproj_kernel(x_ref, mod_ref, w_ref, o_ref, h_scr):
    @pl.when(pl.program_id(1) == 0)
    def _():
        sh = mod_ref[0, 0:1, :]
        sc = mod_ref[0, 1:2, :]
        h_scr[...] = (x_ref[...] * (1.0 + sc) + sh).astype(BF16)

    o_ref[...] = _dot(h_scr[...], w_ref[...])


def _inproj(x, mods_l, w_bf, layer, row_base, rows_per_cond):
    m, d = x.shape
    n = w_bf.shape[2]
    tm = _tile(m, 1024, SUBLANE)
    tm = min(tm, rows_per_cond)
    tn = _tile(n, 1152)
    per = rows_per_cond // tm
    return pl.pallas_call(
        _inproj_kernel,
        grid=(m // tm, n // tn),
        in_specs=[pl.BlockSpec((tm, d), lambda i, j: (i, 0)),
                  pl.BlockSpec((1, 6, d), lambda i, j: (row_base + i // per, 0, 0)),
                  pl.BlockSpec((None, d, tn), lambda i, j: (layer, 0, j))],
        out_specs=pl.BlockSpec((tm, tn), lambda i, j: (i, j)),
        out_shape=jax.ShapeDtypeStruct((m, n), F32),
        scratch_shapes=[pltpu.VMEM((tm, d), BF16)],
        compiler_params=_cparams("arbitrary", "arbitrary"),
        name="inproj",
    )(x, mods_l, w_bf)


def _lambda_value(lam_ref, lam_init):
    lf = lam_ref[...]
    t1 = jnp.sum(lf[0:1] * lf[1:2], axis=1, keepdims=True)
    t2 = jnp.sum(lf[2:3] * lf[3:4], axis=1, keepdims=True)
    return jnp.exp(t1) - jnp.exp(t2) + lam_init


def _split_maps(q):
    lane = lax.broadcasted_iota(jnp.int32, q.shape, 1)
    first = lane < B_QK
    return jnp.where(first, q, 0.0), jnp.where(first, 0.0, q)


def _subnorm(o, g_ref, lam_init):
    ms = jnp.mean(o * o, axis=-1, keepdims=True)
    return o * lax.rsqrt(ms + RMS_EPS) * g_ref[...] * (1.0 - lam_init)


def _ctx_attn_kernel(*refs, lam_init):
    qa_ref, ka_ref, va_ref = refs[0:3]
    b_refs = refs[3:3 + 3 * B_HEADS]
    lam_ref, subln_ref = refs[3 + 3 * B_HEADS:5 + 3 * B_HEADS]
    ya_ref, yb_ref, nak_ref, nav_ref, nbk_ref, nbv_ref = refs[-6:]
    a_scale = HEAD_DIM ** -0.5
    b_scale = B_QK ** -0.5
    for h in range(A_HEADS):
        sl = slice(h * HEAD_DIM, (h + 1) * HEAD_DIM)
        nak_ref[h] = ka_ref[:, sl]
        nav_ref[h] = va_ref[:, sl]
        q = qa_ref[:, sl].astype(BF16)
        k = ka_ref[:, sl].astype(BF16)
        v = va_ref[:, sl].astype(BF16)
        s = _dot_nt(q, k) * a_scale
        e = jnp.exp(s - jnp.max(s, axis=-1, keepdims=True))
        l = jnp.sum(e, axis=-1, keepdims=True)
        ya_ref[:, sl] = (_dot(e.astype(BF16), v) / l).astype(ya_ref.dtype)
    lam = _lambda_value(lam_ref, lam_init)
    for h in range(B_HEADS):
        sl = slice(h * HEAD_DIM, (h + 1) * HEAD_DIM)
        q1, q2 = _split_maps(b_refs[3 * h][...] * b_scale)
        nbk_ref[h] = b_refs[3 * h + 1][...]
        nbv_ref[h] = b_refs[3 * h + 2][...]
        k = b_refs[3 * h + 1][...].astype(BF16)
        v = b_refs[3 * h + 2][...].astype(BF16)
        s1 = _dot_nt(q1.astype(BF16), k)
        s2 = _dot_nt(q2.astype(BF16), k)
        e1 = jnp.exp(s1 - jnp.max(s1, axis=-1, keepdims=True))
        e2 = jnp.exp(s2 - jnp.max(s2, axis=-1, keepdims=True))
        p = e1 / jnp.sum(e1, axis=-1, keepdims=True) - lam * (e2 / jnp.sum(e2, axis=-1, keepdims=True))
        o = _dot(p.astype(BF16), v)
        yb_ref[:, sl] = _subnorm(o, subln_ref, lam_init).astype(yb_ref.dtype)


def _ctx_attn(p, batch, seq, b_lambda_l, b_subln_l, lam_init, layer, depth, carried):
    m = p.shape[0]
    aw = A_HEADS * HEAD_DIM
    nb = aw // HEAD_DIM
    in_specs = [pl.BlockSpec((seq, aw), lambda b: (b, 0)),
                pl.BlockSpec((seq, aw), lambda b: (b, 1)),
                pl.BlockSpec((seq, aw), lambda b: (b, 2))]
    args = [p, p, p]
    for h in range(B_HEADS):
        for seg in range(3):
            col = 3 * nb + seg * B_HEADS + h
            in_specs.append(pl.BlockSpec((seq, HEAD_DIM), functools.partial(lambda b, c: (b, c), c=col)))
            args.append(p)
    in_specs += [pl.BlockSpec((4, B_QK), lambda b: (0, 0)), pl.BlockSpec((1, HEAD_DIM), lambda b: (0, 0))]
    args += [b_lambda_l, b_subln_l.reshape(1, HEAD_DIM)]
    aliases = {}
    if carried is not None:
        for j, buf in enumerate(carried):
            aliases[len(args)] = 2 + j
            in_specs.append(pl.BlockSpec(memory_space=pl.ANY))
            args.append(buf)
    kv_spec = lambda nh: pl.BlockSpec((None, None, nh, seq, HEAD_DIM), lambda b: (b, layer, 0, 0, 0))
    kv_shape = lambda nh: jax.ShapeDtypeStruct((batch, depth, nh, seq, HEAD_DIM), F32)
    out = pl.pallas_call(
        functools.partial(_ctx_attn_kernel, lam_init=lam_init),
        grid=(batch,),
        in_specs=in_specs,
        out_specs=[pl.BlockSpec((seq, aw), lambda b: (b, 0)),
                   pl.BlockSpec((seq, B_HEADS * HEAD_DIM), lambda b: (b, 0)),
                   kv_spec(A_HEADS), kv_spec(A_HEADS), kv_spec(B_HEADS), kv_spec(B_HEADS)],
        out_shape=[jax.ShapeDtypeStruct((m, aw), BF16),
                   jax.ShapeDtypeStruct((m, B_HEADS * HEAD_DIM), BF16),
                   kv_shape(A_HEADS), kv_shape(A_HEADS), kv_shape(B_HEADS), kv_shape(B_HEADS)],
        input_output_aliases=aliases,
        compiler_params=_cparams("arbitrary"),
        name="ctx_attn",
    )(*args)
    return out[0], out[1], tuple(out[2:])


def _na_plan(n_rows):
    wr = min(NA_ROWS, n_rows)
    qrows = NA_QROWS
    uw = -(-(wr + qrows - 1) // 2) * 2
    if n_rows % qrows or n_rows < uw:
        qrows, uw = 1, wr
    rs = lambda r: int(np.clip(r - wr // 2, 0, n_rows - wr))
    starts, var_of_block, variants = [], [], []
    for blk in range(n_rows // qrows):
        q0 = blk * qrows
        ws = int(np.clip(rs(q0), 0, n_rows - uw))
        valid = tuple(tuple(rs(q0 + i) <= ws + j < rs(q0 + i) + wr for j in range(uw)) for i in range(qrows))
        var = (q0 - ws, valid)
        if var not in variants:
            variants.append(var)
        starts.append(ws)
        var_of_block.append(variants.index(var))
    return qrows, uw, starts, var_of_block, variants


def _na_bias_kernel(toe_ref, o_ref, *, n_rows):
    qrows, uw, _, _, variants = _na_plan(n_rows)
    w = GRID_W
    neg = jnp.full((w, w), -jnp.inf, F32)
    for v, (off, valid) in enumerate(variants):
        for i in range(qrows):
            for j in range(uw):
                dr = NA_ROWS - 1 + j - off - i
                tile = toe_ref[dr] if valid[i][j] else neg
                o_ref[v, i * w:(i + 1) * w, j * w:(j + 1) * w] = tile


def _na_bias_table(a_rpb, n_rows):
    depth, heads, nr, _ = a_rpb.shape
    qrows, uw, _, _, variants = _na_plan(n_rows)
    cols = np.arange(GRID_W)
    cs = np.clip(cols - NA_COLS // 2, 0, GRID_W - NA_COLS)
    col_mask = (cols[None, :] >= cs[:, None]) & (cols[None, :] < cs[:, None] + NA_COLS)
    col_idx = np.clip(cols[None, :] - cols[:, None] + NA_COLS - 1, 0, 2 * NA_COLS - 2)
    onehot = (col_idx[None] == np.arange(2 * NA_COLS - 1)[:, None, None]).astype(np.float32)
    toe = jnp.einsum('lhrd,dqk->lhrqk', a_rpb.astype(F32), jnp.asarray(onehot), precision=HI)
    toe = jnp.where(jnp.asarray(col_mask), toe, -jnp.inf)
    shape = (len(variants), qrows * GRID_W, uw * GRID_W)
    return pl.pallas_call(
        functools.partial(_na_bias_kernel, n_rows=n_rows),
        grid=(depth, heads),
        in_specs=[pl.BlockSpec((None, None, nr, GRID_W, GRID_W), lambda l, h: (l, h, 0, 0, 0))],
        out_specs=pl.BlockSpec((None, None) + shape, lambda l, h: (l, h, 0, 0, 0)),
        out_shape=jax.ShapeDtypeStruct((depth, heads) + shape, F32),
        compiler_params=_cparams("arbitrary", "arbitrary"),
        name="na_bias",
    )(toe)


def _select_by_block(blk, values):
    out = jnp.int32(values[0])
    for k in range(1, len(values)):
        out = jnp.where(blk == k, jnp.int32(values[k]), out)
    return out


def _na_kernel(q_ref, k_ref, v_ref, kc_ref, vc_ref, bias_ref, o_ref, kb_scr, vb_scr, kcb_scr, vcb_scr, *, n_rows):
    scale = HEAD_DIM ** -0.5
    qrows, uw, starts, var_of_block, _ = _na_plan(n_rows)
    nq = qrows * GRID_W
    kb_scr[...] = k_ref[...].astype(BF16)
    vb_scr[...] = v_ref[...].astype(BF16)
    kcb_scr[...] = kc_ref[...].astype(BF16)
    vcb_scr[...] = vc_ref[...].astype(BF16)

    def body(blk, carry):
        q0 = pl.multiple_of(blk * nq, nq)
        q = q_ref[pl.ds(q0, nq), :].astype(BF16)
        w0 = pl.multiple_of(_select_by_block(blk, starts) * GRID_W, GRID_W)
        kw = kb_scr[pl.ds(w0, uw * GRID_W), :]
        vw = vb_scr[pl.ds(w0, uw * GRID_W), :]
        s_nb = _dot_nt(q, kw) * scale + bias_ref[_select_by_block(blk, var_of_block)]
        s_c = _dot_nt(q, kcb_scr[...]) * scale
        mx = jnp.maximum(jnp.max(s_nb, axis=-1, keepdims=True), jnp.max(s_c, axis=-1, keepdims=True))
        p_nb = jnp.exp(s_nb - mx)
        p_c = jnp.exp(s_c - mx)
        l = jnp.sum(p_nb, axis=-1, keepdims=True) + jnp.sum(p_c, axis=-1, keepdims=True)
        o = (_dot(p_nb.astype(BF16), vw) + _dot(p_c.astype(BF16), vcb_scr[...])) / l
        o_ref[pl.ds(q0, nq), :] = o.astype(o_ref.dtype)
        return carry

    lax.fori_loop(0, n_rows // qrows, body, 0, unroll=2)


def _na_attn(p, batch, seq, cache_k, cache_v, layer, bias):
    n_rows = seq // GRID_W
    past = cache_k.shape[3]
    blk = lambda off: pl.BlockSpec((seq, HEAD_DIM), lambda b, h: (b, off + h))
    cspec = pl.BlockSpec((None, None, None, past, HEAD_DIM), lambda b, h: (b, layer, h, 0, 0))
    return pl.pallas_call(
        functools.partial(_na_kernel, n_rows=n_rows),
        grid=(batch, A_HEADS),
        in_specs=[blk(0), blk(A_HEADS), blk(2 * A_HEADS), cspec, cspec,
                  pl.BlockSpec((None, None) + bias.shape[2:], lambda b, h: (layer, h, 0, 0, 0))],
        out_specs=pl.BlockSpec((seq, HEAD_DIM), lambda b, h: (b, h)),
        out_shape=jax.ShapeDtypeStruct((batch * seq, A_HEADS * HEAD_DIM), BF16),
        scratch_shapes=[pltpu.VMEM((seq, HEAD_DIM), BF16), pltpu.VMEM((seq, HEAD_DIM), BF16),
                        pltpu.VMEM((past, HEAD_DIM), BF16), pltpu.VMEM((past, HEAD_DIM), BF16)],
        compiler_params=_cparams("arbitrary", "arbitrary"),
        name="na_attn",
    )(p, p, p, cache_k, cache_v, bias)


def _rope_tables(seq):
    t = np.arange(seq)
    rows = (t // GRID_W).astype(np.float32)
    cols = (t % GRID_W).astype(np.float32)
    half = B_QK // 2
    freqs = jnp.asarray(ROPE_BASE, F32) ** (-jnp.arange(0, half, 2, dtype=F32) / half)
    ar = jnp.asarray(rows)[:, None] * freqs
    ac = jnp.asarray(cols)[:, None] * freqs
    cr, sr, cc, sc = jnp.cos(ar), jnp.sin(ar), jnp.cos(ac), jnp.sin(ac)
    cos = jnp.concatenate([cr, cr, cc, cc] * 2, axis=-1)
    sin = jnp.concatenate([-sr, sr, -sc, sc] * 2, axis=-1)
    return cos, sin


def _rope(x, cos, sin):
    q = B_QK // 4
    lane = lax.broadcasted_iota(jnp.int32, x.shape, 1)
    first = (lane & (2 * q - 1)) < q
    partner = jnp.where(first, pltpu.roll(x, LANE - q, 1), pltpu.roll(x, q, 1))
    return x * cos + partner * sin


def _diff_lat_kernel(q_ref, k_ref, v_ref, kc_ref, vc_ref, cosq_ref, sinq_ref, cosk_ref, sink_ref, lam_ref, subln_ref,
                     o_ref, kr_scr, kcb_scr, vb_scr, vcb_scr, *, lam_init):
    scale = B_QK ** -0.5

    @pl.when(pl.program_id(2) == 0)
    def _():
        kr_scr[...] = _rope(k_ref[...], cosk_ref[...], sink_ref[...]).astype(BF16)
        kcb_scr[...] = kc_ref[...].astype(BF16)
        vb_scr[...] = v_ref[...].astype(BF16)
        vcb_scr[...] = vc_ref[...].astype(BF16)

    lam = _lambda_value(lam_ref, lam_init)
    q = q_ref[...] * scale
    qr1, qr2 = _split_maps(_rope(q, cosq_ref[...], sinq_ref[...]))
    q1, q2 = _split_maps(q)

    def probs(qr, qp):
        s_l = _dot_nt(qr.astype(BF16), kr_scr[...])
        s_c = _dot_nt(qp.astype(BF16), kcb_scr[...])
        mx = jnp.maximum(jnp.max(s_l, axis=-1, keepdims=True), jnp.max(s_c, axis=-1, keepdims=True))
        e_l = jnp.exp(s_l - mx)
        e_c = jnp.exp(s_c - mx)
        inv = 1.0 / (jnp.sum(e_l, axis=-1, keepdims=True) + jnp.sum(e_c, axis=-1, keepdims=True))
        return e_l * inv, e_c * inv

    p1_l, p1_c = probs(qr1, q1)
    p2_l, p2_c = probs(qr2, q2)
    o = (_dot((p1_l - lam * p2_l).astype(BF16), vb_scr[...])
         + _dot((p1_c - lam * p2_c).astype(BF16), vcb_scr[...]))
    o_ref[...] = _subnorm(o, subln_ref, lam_init).astype(o_ref.dtype)


def _diff_lat_attn(p, batch, seq, cache_k, cache_v, layer, cos, sin, b_lambda_l, b_subln_l, lam_init):
    past = cache_k.shape[3]
    tq = _tile(seq, 256, SUBLANE)
    nq = seq // tq
    base = 3 * A_HEADS
    cspec = pl.BlockSpec((None, None, None, past, HEAD_DIM), lambda b, h, i: (b, layer, h, 0, 0))
    kv = lambda off: pl.BlockSpec((seq, HEAD_DIM), lambda b, h, i: (b, off + h))
    return pl.pallas_call(
        functools.partial(_diff_lat_kernel, lam_init=lam_init),
        grid=(batch, B_HEADS, nq),
        in_specs=[pl.BlockSpec((tq, HEAD_DIM), lambda b, h, i: (b * nq + i, base + h)),
                  kv(base + B_HEADS), kv(base + 2 * B_HEADS), cspec, cspec,
                  pl.BlockSpec((tq, HEAD_DIM), lambda b, h, i: (i, 0)),
                  pl.BlockSpec((tq, HEAD_DIM), lambda b, h, i: (i, 0)),
                  pl.BlockSpec((seq, HEAD_DIM), lambda b, h, i: (0, 0)),
                  pl.BlockSpec((seq, HEAD_DIM), lambda b, h, i: (0, 0)),
                  pl.BlockSpec((4, B_QK), lambda b, h, i: (0, 0)),
                  pl.BlockSpec((1, HEAD_DIM), lambda b, h, i: (0, 0))],
        out_specs=pl.BlockSpec((tq, HEAD_DIM), lambda b, h, i: (b * nq + i, h)),
        out_shape=jax.ShapeDtypeStruct((batch * seq, B_HEADS * HEAD_DIM), BF16),
        scratch_shapes=[pltpu.VMEM((seq, HEAD_DIM), BF16), pltpu.VMEM((past, HEAD_DIM), BF16),
                        pltpu.VMEM((seq, HEAD_DIM), BF16), pltpu.VMEM((past, HEAD_DIM), BF16)],
        compiler_params=_cparams("arbitrary", "arbitrary", "arbitrary"),
        name="diff_lat_attn",
    )(p, p, p, cache_k, cache_v, cos, sin, cos, sin, b_lambda_l, b_subln_l.reshape(1, HEAD_DIM))


def _log_sigmoid(x):
    return jnp.minimum(x, 0.0) - jnp.log1p(jnp.exp(-jnp.abs(x)))


def _split3(x):
    hi = x.astype(BF16)
    r = x - hi.astype(F32)
    mid = r.astype(BF16)
    lo = (r - mid.astype(F32)).astype(BF16)
    return hi, mid, lo


def _mlstm_kernel(*refs, chunk, n_chunks):
    nh = C_HEADS
    q_refs, k_refs, v_refs, og_refs = (refs[i * nh:(i + 1) * nh] for i in range(4))
    g_ref, gb_ref, cn_ref, c0_ref, n0_ref, m0_ref = refs[4 * nh:4 * nh + 6]
    y_ref, cf_ref, nf_ref, mf_ref, hf_scr, hb_scr = refs[-6:]
    scale = HEAD_DIM ** -0.5
    ln = chunk
    n_sel = 4 * SUBLANE
    assert 4 * nh <= n_sel
    sel = jnp.where(lax.broadcasted_iota(jnp.int32, (n_sel, LANE), 0)
                    == lax.broadcasted_iota(jnp.int32, (n_sel, LANE), 1), 1.0, 0.0).astype(BF16)
    ti = lax.broadcasted_iota(jnp.int32, (ln, ln), 0)
    si = lax.broadcasted_iota(jnp.int32, (ln, ln), 1)
    lower = si <= ti
    upper = si >= ti
    lower_b = jnp.where(lower, 1.0, 0.0).astype(BF16)
    upper_b = jnp.where(upper, 1.0, 0.0).astype(BF16)

    def gate_terms(r0, directions):
        g = g_ref[pl.ds(r0, ln), :] + gb_ref[...]
        gp = _split3(g)
        rows = sum(_dot_nt(sel, p) for p in gp)
        cols = sum(_dot_nt(p, sel) for p in gp)
        lf_rows = _split3(_log_sigmoid(rows))
        lf_cols = _split3(_log_sigmoid(cols))
        t = dict(rows=rows, cols=cols)
        if 0 in directions:
            t["row0"] = sum(_dot(p, upper_b) for p in lf_rows)
            t["col0"] = sum(_dot(lower_b, p) for p in lf_cols)
        if 1 in directions:
            t["row1"] = sum(_dot(p, lower_b) for p in lf_rows)
            t["col1"] = sum(_dot(upper_b, p) for p in lf_cols)
        return t

    def head_terms(h, r0):
        qf = q_refs[h][pl.ds(r0, ln), :] * scale
        kf = k_refs[h][pl.ds(r0, ln), :]
        qb = qf.astype(BF16)
        vb = v_refs[h][pl.ds(r0, ln), :].astype(BF16)
        return dict(qf=qf, kf=kf, qb=qb, vb=vb, qk=_dot_nt(qb, kf.astype(BF16)))

    def chunk_step(gt, t, h, state, backward):
        c_st, n_st, m_st = state
        rows, cols, qf, kf, qb, vb = gt["rows"], gt["cols"], t["qf"], t["kf"], t["qb"], t["vb"]
        gi = (2 * nh if backward else 0) + h
        gf = gi + nh
        i_row, i_col = rows[gi:gi + 1], cols[:, gi:gi + 1]
        if backward:
            b_row, b_col, mask = gt["row1"][gf:gf + 1], gt["col1"][:, gf:gf + 1], upper
            b_last = b_col[0:1]
        else:
            b_row, b_col, mask = gt["row0"][gf:gf + 1], gt["col0"][:, gf:gf + 1], lower
            b_last = b_col[ln - 1:ln]
        d = jnp.where(mask, b_col - b_row + i_row, -jnp.inf)
        inter = b_col + m_st
        m_row = jnp.maximum(jnp.max(d, axis=-1, keepdims=True), inter)
        w_intra = jnp.exp(d - m_row)
        w_state = jnp.exp(inter - m_row)
        sc = t["qk"] * w_intra
        num = w_state * _dot(qb, c_st.astype(BF16)) + _dot(sc.astype(BF16), vb)
        den = w_state * jnp.sum(qf * n_st, axis=-1, keepdims=True) + jnp.sum(sc, axis=-1, keepdims=True)
        h_out = num / jnp.maximum(jnp.abs(den), jnp.exp(-m_row))
        gg = b_last - b_col + i_col
        m_new = jnp.maximum(b_last + m_st, jnp.max(gg, axis=0, keepdims=True))
        w_old = jnp.exp(b_last + m_st - m_new)
        w_s = jnp.exp(gg - m_new)
        kw = kf * w_s
        c_new = w_old * c_st + _dot_tn(kw.astype(BF16), vb)
        n_new = w_old * n_st + jnp.sum(kw, axis=0, keepdims=True)
        return h_out, (c_new, n_new, m_new)

    def start(c):
        return pl.multiple_of(c * ln, ln)

    def finish(h, r0, hs):
        ms = jnp.mean(hs * hs, axis=-1, keepdims=True)
        hn = hs * lax.rsqrt(ms + RMS_EPS) * cn_ref[...]
        og = og_refs[h][pl.ds(r0, ln), :]
        y_ref[pl.ds(r0, ln), h * HEAD_DIM:(h + 1) * HEAD_DIM] = (hn / (1.0 + jnp.exp(-og))).astype(y_ref.dtype)

    init = tuple(tuple((c0_ref[dr, h], n0_ref[dr, h], m0_ref[dr, h][:, 0:1]) for h in range(nh)) for dr in (0, 1))
    if n_chunks == 1:
        gt = gate_terms(0, (0, 1))
        st_f, st_b = [], []
        for h in range(nh):
            t = head_terms(h, 0)
            h_f, s_f = chunk_step(gt, t, h, init[0][h], False)
            h_b, s_b = chunk_step(gt, t, h, init[1][h], True)
            finish(h, 0, h_f + h_b)
            st_f.append(s_f)
            st_b.append(s_b)
        final = (st_f, st_b)
    else:
        def body(j, states):
            r_f = start(j)
            r_b = start(n_chunks - 1 - j)
            gt_f = gate_terms(r_f, (0,))
            gt_b = gate_terms(r_b, (1,))
            st_f, st_b = [], []
            for h in range(nh):
                sl = slice(h * HEAD_DIM, (h + 1) * HEAD_DIM)
                h_f, s_f = chunk_step(gt_f, head_terms(h, r_f), h, states[0][h], False)
                h_b, s_b = chunk_step(gt_b, head_terms(h, r_b), h, states[1][h], True)
                hf_scr[pl.ds(r_f, ln), sl] = h_f
                hb_scr[pl.ds(r_b, ln), sl] = h_b
                st_f.append(s_f)
                st_b.append(s_b)
            return tuple(st_f), tuple(st_b)

        final = lax.fori_loop(0, n_chunks, body, init)

        def fin_body(j, carry):
            r0 = start(j)
            for h in range(nh):
                sl = slice(h * HEAD_DIM, (h + 1) * HEAD_DIM)
                finish(h, r0, hf_scr[pl.ds(r0, ln), sl] + hb_scr[pl.ds(r0, ln), sl])
            return carry

        lax.fori_loop(0, n_chunks, fin_body, 0)
    for dr in (0, 1):
        for h in range(nh):
            c_f, n_f, m_f = final[dr][h]
            cf_ref[dr, h] = c_f
            nf_ref[dr, h] = n_f
            mf_ref[dr, h] = jnp.broadcast_to(m_f, (1, LANE))


def _mlstm(p, batch, seq, gate_bias, c_norm_l, c0, n0, m0, lin, out_depth, lout, carried):
    chunk = min(MLSTM_CHUNK, seq)
    n_chunks = seq // chunk
    base = 3 * A_HEADS + 3 * B_HEADS
    hd = HEAD_DIM
    nh = C_HEADS
    head_bytes = 4 * nh * seq * hd * 4
    mode = dict(pipeline_mode=pl.Buffered(1)) if 2 * head_bytes > VMEM_LIMIT_BYTES // 2 else {}
    blk = lambda col: pl.BlockSpec((seq, hd), functools.partial(lambda b, c: (b, c), c=col), **mode)
    st_c = lambda l: pl.BlockSpec((None, None, 2, nh, hd, hd), lambda b: (b, l, 0, 0, 0, 0))
    st_n = lambda l: pl.BlockSpec((None, None, 2, nh, 1, hd), lambda b: (b, l, 0, 0, 0, 0))
    n0r = n0.reshape(n0.shape[:4] + (1, hd))
    m0r = jnp.broadcast_to(m0[..., None, None], m0.shape + (1, LANE))
    in_specs = ([blk(base + seg * nh + h) for seg in range(4) for h in range(nh)]
                + [pl.BlockSpec((seq, LANE), lambda b: (b, base + 4 * nh)),
                   pl.BlockSpec((1, LANE), lambda b: (0, 0)),
                   pl.BlockSpec((1, hd), lambda b: (0, 0)),
                   st_c(lin), st_n(lin), st_n(lin)])
    args = [p] * (4 * nh + 1) + [gate_bias, c_norm_l.reshape(1, hd), c0, n0r, m0r]
    aliases = {}
    if carried is not None:
        for j, buf in enumerate(carried):
            aliases[len(args)] = 1 + j
            in_specs.append(pl.BlockSpec(memory_space=pl.ANY))
            args.append(buf)
    out = pl.pallas_call(
        functools.partial(_mlstm_kernel, chunk=chunk, n_chunks=n_chunks),
        grid=(batch,),
        in_specs=in_specs,
        out_specs=[pl.BlockSpec((seq, nh * hd), lambda b: (b, 0)), st_c(lout), st_n(lout), st_n(lout)],
        out_shape=[jax.ShapeDtypeStruct((batch * seq, nh * hd), BF16),
                   jax.ShapeDtypeStruct((batch, out_depth, 2, nh, hd, hd), F32),
                   jax.ShapeDtypeStruct((batch, out_depth, 2, nh, 1, hd), F32),
                   jax.ShapeDtypeStruct((batch, out_depth, 2, nh, 1, LANE), F32)],
        scratch_shapes=[pltpu.VMEM((seq, nh * hd), F32), pltpu.VMEM((seq, nh * hd), F32)],
        input_output_aliases=aliases,
        compiler_params=_cparams("arbitrary"),
        name="mlstm",
    )(*args)
    return out[0], tuple(out[1:])


def _layernorm(z, g_ref, b_ref):
    mu = jnp.mean(z, axis=-1, keepdims=True)
    zc = z - mu
    var = jnp.mean(zc * zc, axis=-1, keepdims=True)
    return zc * lax.rsqrt(var + LN_EPS) * g_ref[...] + b_ref[...]


def _outproj_kernel(ya_ref, yb_ref, yc_ref, w_ref, x_ref, mod_ref, g_ref, b_ref, x1_ref, h2_ref):
    ycat = jnp.concatenate([ya_ref[...], yb_ref[...], yc_ref[...]], axis=-1)
    y = _dot(ycat, w_ref[...])
    gate = mod_ref[0, 2:3, :]
    x1 = _layernorm(ALPHA * x_ref[...] + gate * y, g_ref, b_ref)
    x1_ref[...] = x1
    h2_ref[...] = (x1 * (1.0 + mod_ref[0, 4:5, :]) + mod_ref[0, 3:4, :]).astype(BF16)


def _outproj(ya, yb, yc, w_bf, layer, x, mods_l, ln_g, ln_b, row_base, rows_per_cond):
    m, d = x.shape
    tm = min(_tile(m, 512, SUBLANE), rows_per_cond)
    per = rows_per_cond // tm
    row = lambda w: pl.BlockSpec((tm, w), lambda i: (i, 0))
    vec = pl.BlockSpec((1, d), lambda i: (0, 0))
    return pl.pallas_call(
        _outproj_kernel,
        grid=(m // tm,),
        in_specs=[row(ya.shape[1]), row(yb.shape[1]), row(yc.shape[1]),
                  pl.BlockSpec((None, d, d), lambda i: (layer, 0, 0)), row(d),
                  pl.BlockSpec((1, 6, d), lambda i: (row_base + i // per, 0, 0)), vec, vec],
        out_specs=[row(d), row(d)],
        out_shape=[jax.ShapeDtypeStruct((m, d), F32), jax.ShapeDtypeStruct((m, d), BF16)],
        compiler_params=_cparams("arbitrary"),
        name="outproj_ln",
    )(ya, yb, yc, w_bf, x, mods_l, ln_g.reshape(1, d), ln_b.reshape(1, d))


def _ffn_kernel(h_ref, hp_ref, hn_ref, wa_ref, wg_ref, cwa_ref, cwg_ref, cba_ref, cbg_ref, wd_ref, x_ref, mod_ref,
                g_ref, b_ref, o_ref, hext_scr, acc_scr, *, seq):
    i = pl.program_id(0)
    f = pl.program_id(1)
    tm = h_ref.shape[0]
    halo = BF16_ROWS

    @pl.when(f == 0)
    def _():
        hext_scr[0:halo, :] = hp_ref[...]
        hext_scr[halo:halo + tm, :] = h_ref[...]
        hext_scr[halo + tm:, :] = hn_ref[...]
        acc_scr[...] = jnp.zeros_like(acc_scr)

    period = min(seq, tm)
    grp = lax.broadcasted_iota(jnp.int32, (SUBLANE, 1), 0)

    def mask_rows(x, first):
        pieces = []
        for r0 in range(0, tm, period):
            g0 = r0 if first else r0 + period - SUBLANE
            pos = (i * tm + g0 + grp) & (seq - 1)
            keep = (pos != 0) if first else (pos != seq - 1)
            masked = jnp.where(keep, x[g0:g0 + SUBLANE], 0.0)
            rest = x[r0 + SUBLANE:r0 + period] if first else x[r0:g0]
            pieces += [masked, rest] if first else [rest, masked]
        return jnp.concatenate(pieces, axis=0)

    def up_conv(w_ref, cw_ref, cb_ref):
        u = _dot(hext_scr[...], w_ref[...])
        rows = u.shape[0]
        u_prev = mask_rows(pltpu.roll(u, 1, 0)[halo:halo + tm], True)
        u_next = mask_rows(pltpu.roll(u, rows - 1, 0)[halo:halo + tm], False)
        return u_prev * cw_ref[0:1, :] + u[halo:halo + tm] * cw_ref[1:2, :] + u_next * cw_ref[2:3, :] + cb_ref[...]

    a = up_conv(wa_ref, cwa_ref, cba_ref)
    g = up_conv(wg_ref, cwg_ref, cbg_ref)
    act = (g / (1.0 + jnp.exp2(g * NEG_LOG2_E))) * a
    acc_scr[...] += _dot(act.astype(BF16), wd_ref[...])

    @pl.when(f == pl.num_programs(1) - 1)
    def _():
        z = ALPHA * x_ref[...] + mod_ref[0, 5:6, :] * acc_scr[...]
        o_ref[...] = _layernorm(z, g_ref, b_ref)


def _ffn(h2, x1, ffn_w, layer, mods_l, ln_g, ln_b, row_base, rows_per_cond, seq, tf):
    wu_bf, cw, cb, wd_bf = ffn_w
    m, d = x1.shape
    fp = wd_bf.shape[1]
    tm = min(_tile(m, 512, SUBLANE), rows_per_cond)
    per = rows_per_cond // tm
    assert seq & (seq - 1) == 0 and m % seq == 0
    nh = m // BF16_ROWS
    hb = tm // BF16_ROWS
    vec = pl.BlockSpec((1, d), lambda i, f: (0, 0))
    wcol = lambda rows, half: pl.BlockSpec((None, None, rows, tf),
                                           functools.partial(lambda i, f, hf: (layer, hf, 0, f), hf=half))
    return pl.pallas_call(
        functools.partial(_ffn_kernel, seq=seq),
        grid=(m // tm, fp // tf),
        in_specs=[pl.BlockSpec((tm, d), lambda i, f: (i, 0)),
                  pl.BlockSpec((BF16_ROWS, d), lambda i, f: (jnp.maximum(i * hb - 1, 0), 0)),
                  pl.BlockSpec((BF16_ROWS, d), lambda i, f: (jnp.minimum((i + 1) * hb, nh - 1), 0)),
                  wcol(d, 0), wcol(d, 1), wcol(CONV_W, 0), wcol(CONV_W, 1), wcol(1, 0), wcol(1, 1),
                  pl.BlockSpec((None, tf, d), lambda i, f: (layer, f, 0)),
                  pl.BlockSpec((tm, d), lambda i, f: (i, 0)),
                  pl.BlockSpec((1, 6, d), lambda i, f: (row_base + i // per, 0, 0)), vec, vec],
        out_specs=pl.BlockSpec((tm, d), lambda i, f: (i, 0)),
        out_shape=jax.ShapeDtypeStruct((m, d), F32),
        scratch_shapes=[pltpu.VMEM((tm + 2 * BF16_ROWS, d), BF16), pltpu.VMEM((tm, d), F32)],
        compiler_params=_cparams("arbitrary", "arbitrary"),
        name="ffn_ln",
    )(h2, h2, h2, wu_bf, wu_bf, cw, cw, cb, cb, wd_bf, x1, mods_l, ln_g.reshape(1, d), ln_b.reshape(1, d))


def _pad_cols(w, n):
    return jnp.pad(w, [(0, 0)] * (w.ndim - 1) + [(0, n - w.shape[-1])])


def _cast_kernel(x_ref, o_ref, *, rows_valid, cols_valid):
    tr, tc = x_ref.shape
    x = x_ref[...]
    if rows_valid is not None:
        r = pl.program_id(1) * tr + lax.broadcasted_iota(jnp.int32, (tr, tc), 0)
        x = jnp.where(r < rows_valid, x, 0.0)
    if cols_valid is not None:
        c = pl.program_id(2) * tc + lax.broadcasted_iota(jnp.int32, (tr, tc), 1)
        x = jnp.where(c < cols_valid, x, 0.0)
    o_ref[:, :tc] = x.astype(BF16)
    if o_ref.shape[1] > tc:
        o_ref[:, tc:] = jnp.zeros((tr, o_ref.shape[1] - tc), BF16)


def _cast_weights(w, tr, tc, rows_out, cols_out):
    depth, r, c = w.shape
    nr, nc = -(-rows_out // tr), -(-cols_out // tc)
    assert nr * tr == rows_out and nc * tc == cols_out
    return pl.pallas_call(
        functools.partial(_cast_kernel, rows_valid=r if rows_out > r else None, cols_valid=c if cols_out > c else None),
        grid=(depth, nr, nc),
        in_specs=[pl.BlockSpec((None, tr, tc), lambda l, i, j: (l, i, j))],
        out_specs=pl.BlockSpec((None, tr, tc), lambda l, i, j: (l, i, j)),
        out_shape=jax.ShapeDtypeStruct((depth, rows_out, cols_out), BF16),
        compiler_params=_cparams("arbitrary", "arbitrary", "arbitrary"),
        name="cast_weights",
    )(w)


def _cast_in_kernel(x_ref, o_ref, *, n_valid):
    tn = x_ref.shape[0]
    col = pl.program_id(0) * tn + lax.broadcasted_iota(jnp.int32, (o_ref.shape[1], tn), 1)
    for l in range(x_ref.shape[1]):
        o_ref[l] = jnp.where(col < n_valid, x_ref[:, l, :].T, 0.0).astype(BF16)


def _cast_in_weights(w_in, n_pad, tn):
    depth, d, n = w_in.shape
    return pl.pallas_call(
        functools.partial(_cast_in_kernel, n_valid=n),
        grid=(n_pad // tn,),
        in_specs=[pl.BlockSpec((tn, depth, d), lambda j: (j, 0, 0))],
        out_specs=pl.BlockSpec((depth, d, tn), lambda j: (0, 0, j)),
        out_shape=jax.ShapeDtypeStruct((depth, d, n_pad), BF16),
        compiler_params=_cparams("arbitrary"),
        name="cast_in_weights",
    )(jnp.transpose(w_in, (2, 0, 1)))


def _cast_up_weights(w_up, dff, fp, tr):
    depth, d, _ = w_up.shape
    return pl.pallas_call(
        functools.partial(_cast_kernel, rows_valid=None, cols_valid=None),
        grid=(depth, d // tr, 2),
        in_specs=[pl.BlockSpec((None, tr, dff), lambda l, i, j: (l, i, j))],
        out_specs=pl.BlockSpec((None, None, tr, fp), lambda l, i, j: (l, j, i, 0)),
        out_shape=jax.ShapeDtypeStruct((depth, 2, d, fp), BF16),
        compiler_params=_cparams("arbitrary", "arbitrary", "arbitrary"),
        name="cast_up_weights",
    )(w_up)


def _prep_weights(w_in, w_out, w_up, conv_w, conv_b, w_down, tf):
    depth, d, n_in = w_in.shape
    n_pad = -(-n_in // LANE) * LANE
    dff = w_down.shape[1]
    fp = -(-dff // tf) * tf
    w_in_bf = _cast_in_weights(w_in, n_pad, _tile(n_pad, 384))
    w_out_bf = _cast_weights(w_out, _tile(d, 512, SUBLANE), d, d, d)
    w_up_bf = _cast_up_weights(w_up, dff, fp, _tile(d, 256, SUBLANE))
    w_down_bf = _cast_weights(w_down, tf, d, fp, d)
    halves = lambda t: jnp.stack([_pad_cols(t[..., :dff], fp), _pad_cols(t[..., dff:], fp)], axis=1)
    return w_in_bf, w_out_bf, w_up_bf, halves(conv_w), halves(conv_b[:, None, :]), w_down_bf


def _gate_bias_row(c_gate_b_l):
    return _pad_cols(c_gate_b_l.reshape(1, -1), LANE)


def kernel(x_prompt, x_sample, cache_a_k, cache_a_v, cache_b_k, cache_b_v, state_c_C, state_c_n, state_c_m, c, c_ctx, w_mod, b_mod, w_in, c_gate_b, a_rpb, b_lambda, b_subln, c_norm, w_out, ln1_g, ln1_b, ln2_g, ln2_b, w_up, conv_w, conv_b, w_down):
    batch, seq, d = x_prompt.shape
    dec_batch, dec_seq, _ = x_sample.shape
    depth = w_in.shape[0]
    tf = 512

    cond = jnp.concatenate([c_ctx[None, :], c], axis=0)
    cond = jnp.pad(cond, ((0, SUBLANE - cond.shape[0]), (0, 0)))
    mods = _mods(cond, w_mod, b_mod).reshape(depth, SUBLANE, 6, d)

    xp = x_prompt.reshape(batch * seq, d)
    xs = x_sample.reshape(dec_batch * dec_seq, d)
    cos, sin = _rope_tables(dec_seq)
    na_bias = _na_bias_table(a_rpb, dec_seq // GRID_W)
    zeros_c = jnp.zeros((batch, 1, 2, C_HEADS, HEAD_DIM, HEAD_DIM), F32)
    zeros_n = jnp.zeros((batch, 1, 2, C_HEADS, HEAD_DIM), F32)
    zeros_m = jnp.zeros((batch, 1, 2, C_HEADS), F32)

    w_in_bf, w_out_bf, *ffn_w = _prep_weights(w_in, w_out, w_up, conv_w, conv_b, w_down, tf)
    new_kv = None
    new_state = None
    for l in range(depth):
        lam_init = 0.8 - 0.6 * math.exp(-0.3 * l)
        gate_bias = _gate_bias_row(c_gate_b[l])

        pp = _inproj(xp, mods[l], w_in_bf, l, 0, batch * seq)
        ya, yb, new_kv = _ctx_attn(pp, batch, seq, b_lambda[l], b_subln[l], lam_init, l, depth, new_kv)
        yc, new_state = _mlstm(pp, batch, seq, gate_bias, c_norm[l], zeros_c, zeros_n, zeros_m, 0, depth, l, new_state)
        x1, h2 = _outproj(ya, yb, yc, w_out_bf, l, xp, mods[l], ln1_g[l], ln1_b[l], 0, batch * seq)
        xp = _ffn(h2, x1, ffn_w, l, mods[l], ln2_g[l], ln2_b[l], 0, batch * seq, seq, tf)

        ps = _inproj(xs, mods[l], w_in_bf, l, 1, dec_seq)
        ya = _na_attn(ps, dec_batch, dec_seq, cache_a_k, cache_a_v, l, na_bias)
        yb = _diff_lat_attn(ps, dec_batch, dec_seq, cache_b_k, cache_b_v, l, cos, sin, b_lambda[l], b_subln[l], lam_init)
        yc, _ = _mlstm(ps, dec_batch, dec_seq, gate_bias, c_norm[l], state_c_C, state_c_n, state_c_m, l, 1, 0, None)
        x1, h2 = _outproj(ya, yb, yc, w_out_bf, l, xs, mods[l], ln1_g[l], ln1_b[l], 1, dec_seq)
        xs = _ffn(h2, x1, ffn_w, l, mods[l], ln2_g[l], ln2_b[l], 1, dec_seq, dec_seq, tf)

    c_f, n_f, m_f = new_state
    return (xp.reshape(batch, seq, d), xs.reshape(dec_batch, dec_seq, d), *new_kv,
            c_f, n_f[..., 0, :], m_f[..., 0, 0])
```

```python
import functools
import math

import jax
import jax.numpy as jnp
import numpy as np
from jax import lax
from jax.experimental import pallas as pl
from jax.experimental.pallas import tpu as pltpu

F32 = jnp.float32
BF16 = jnp.bfloat16

DEPTH = 2
GRID_W = 64
A_HEADS = 6
NA_ROWS = 8
NA_COLS = 16
NA_QROWS = 4
B_HEADS = 5
B_QK = 64
C_HEADS = 5
HEAD_DIM = 128
CONV_W = 3
ROPE_BASE = 10000.0
LN_EPS = 1e-5
RMS_EPS = 1e-6
ALPHA = (2 * DEPTH) ** 0.25
NEG_LOG2_E = -math.log2(math.e)

LANE = 128
SUBLANE = 8
BF16_ROWS = 16
VMEM_LIMIT_BYTES = 56 * 1024 * 1024

MLSTM_CHUNK = 256
HI = lax.Precision.HIGHEST


def _tile(n, target, unit=LANE):
    if n <= target:
        return n
    best = unit
    for t in range(unit, target + 1, unit):
        if n % t == 0:
            best = t
    assert n % best == 0, (n, target, unit)
    return best


def _cparams(*sem):
    return pltpu.CompilerParams(dimension_semantics=sem, vmem_limit_bytes=VMEM_LIMIT_BYTES)


def _dot(a, b):
    return jnp.dot(a, b, preferred_element_type=F32)


def _dot_nt(a, b, precision=None):
    return lax.dot_general(a, b, (((1,), (1,)), ((), ())), preferred_element_type=F32, precision=precision)


def _dot_tn(a, b):
    return lax.dot_general(a, b, (((0,), (0,)), ((), ())), preferred_element_type=F32)


def _mods_kernel(cond_ref, w_ref, b_ref, o_ref):
    c = cond_ref[...]
    s = c / (1.0 + jnp.exp(-c))
    o_ref[0] = _dot(s.astype(BF16), w_ref[0].astype(BF16)) + b_ref[0]


def _mods(cond, w_mod, b_mod):
    depth, d, n = w_mod.shape
    tn = _tile(n, 1536)
    return pl.pallas_call(
        _mods_kernel,
        grid=(depth, n // tn),
        in_specs=[pl.BlockSpec((SUBLANE, d), lambda l, j: (0, 0)),
                  pl.BlockSpec((1, d, tn), lambda l, j: (l, 0, j)),
                  pl.BlockSpec((1, 1, tn), lambda l, j: (l, 0, j))],
        out_specs=pl.BlockSpec((1, SUBLANE, tn), lambda l, j: (l, 0, j)),
        out_shape=jax.ShapeDtypeStruct((depth, SUBLANE, n), F32),
        compiler_params=_cparams("arbitrary", "arbitrary"),
        name="mods",
    )(cond, w_mod, b_mod.reshape(depth, 1, n))


def _inproj_kernel(x_ref, mod_ref, w_ref, o_ref, h_scr):
    @pl.when(pl.program_id(1) == 0)
    def _():
        sh = mod_ref[0, 0:1, :]
        sc = mod_ref[0, 1:2, :]
        h_scr[...] = (x_ref[...] * (1.0 + sc) + sh).astype(BF16)

    o_ref[...] = _dot(h_scr[...], w_ref[...])


def _inproj(x, mods_l, w_bf, layer, row_base, rows_per_cond):
    m, d = x.shape
    n = w_bf.shape[2]
    tm = _tile(m, 1024, SUBLANE)
    tm = min(tm, rows_per_cond)
    tn = _tile(n, 1152)
    per = rows_per_cond // tm
    return pl.pallas_call(
        _inproj_kernel,
        grid=(m // tm, n // tn),
        in_specs=[pl.BlockSpec((tm, d), lambda i, j: (i, 0)),
                  pl.BlockSpec((1, 6, d), lambda i, j: (row_base + i // per, 0, 0)),
                  pl.BlockSpec((None, d, tn), lambda i, j: (layer, 0, j))],
        out_specs=pl.BlockSpec((tm, tn), lambda i, j: (i, j)),
        out_shape=jax.ShapeDtypeStruct((m, n), F32),
        scratch_shapes=[pltpu.VMEM((tm, d), BF16)],
        compiler_params=_cparams("arbitrary", "arbitrary"),
        name="inproj",
    )(x, mods_l, w_bf)


def _lambda_value(lam_ref, lam_init):
    lf = lam_ref[...]
    t1 = jnp.sum(lf[0:1] * lf[1:2], axis=1, keepdims=True)
    t2 = jnp.sum(lf[2:3] * lf[3:4], axis=1, keepdims=True)
    return jnp.exp(t1) - jnp.exp(t2) + lam_init


def _split_maps(q):
    lane = lax.broadcasted_iota(jnp.int32, q.shape, 1)
    first = lane < B_QK
    return jnp.where(first, q, 0.0), jnp.where(first, 0.0, q)


def _subnorm(o, g_ref, lam_init):
    ms = jnp.mean(o * o, axis=-1, keepdims=True)
    return o * lax.rsqrt(ms + RMS_EPS) * g_ref[...] * (1.0 - lam_init)


def _ctx_attn_kernel(*refs, lam_init):
    qa_ref, ka_ref, va_ref = refs[0:3]
    b_refs = refs[3:3 + 3 * B_HEADS]
    lam_ref, subln_ref = refs[3 + 3 * B_HEADS:5 + 3 * B_HEADS]
    ya_ref, yb_ref, nak_ref, nav_ref, nbk_ref, nbv_ref = refs[-6:]
    a_scale = HEAD_DIM ** -0.5
    b_scale = B_QK ** -0.5
    for h in range(A_HEADS):
        sl = slice(h * HEAD_DIM, (h + 1) * HEAD_DIM)
        nak_ref[h] = ka_ref[:, sl]
        nav_ref[h] = va_ref[:, sl]
        q = qa_ref[:, sl].astype(BF16)
        k = ka_ref[:, sl].astype(BF16)
        v = va_ref[:, sl].astype(BF16)
        s = _dot_nt(q, k) * a_scale
        e = jnp.exp(s - jnp.max(s, axis=-1, keepdims=True))
        l = jnp.sum(e, axis=-1, keepdims=True)
        ya_ref[:, sl] = (_dot(e.astype(BF16), v) / l).astype(ya_ref.dtype)
    lam = _lambda_value(lam_ref, lam_init)
    for h in range(B_HEADS):
        sl = slice(h * HEAD_DIM, (h + 1) * HEAD_DIM)
        q1, q2 = _split_maps(b_refs[3 * h][...] * b_scale)
        nbk_ref[h] = b_refs[3 * h + 1][...]
        nbv_ref[h] = b_refs[3 * h + 2][...]
        k = b_refs[3 * h + 1][...].astype(BF16)
        v = b_refs[3 * h + 2][...].astype(BF16)
        s1 = _dot_nt(q1.astype(BF16), k)
        s2 = _dot_nt(q2.astype(BF16), k)
        e1 = jnp.exp(s1 - jnp.max(s1, axis=-1, keepdims=True))
        e2 = jnp.exp(s2 - jnp.max(s2, axis=-1, keepdims=True))
        p = e1 / jnp.sum(e1, axis=-1, keepdims=True) - lam * (e2 / jnp.sum(e2, axis=-1, keepdims=True))
        o = _dot(p.astype(BF16), v)
        yb_ref[:, sl] = _subnorm(o, subln_ref, lam_init).astype(yb_ref.dtype)


def _ctx_attn(p, batch, seq, b_lambda_l, b_subln_l, lam_init, layer, depth, carried):
    m = p.shape[0]
    aw = A_HEADS * HEAD_DIM
    nb = aw // HEAD_DIM
    in_specs = [pl.BlockSpec((seq, aw), lambda b: (b, 0)),
                pl.BlockSpec((seq, aw), lambda b: (b, 1)),
                pl.BlockSpec((seq, aw), lambda b: (b, 2))]
    args = [p, p, p]
    for h in range(B_HEADS):
        for seg in range(3):
            col = 3 * nb + seg * B_HEADS + h
            in_specs.append(pl.BlockSpec((seq, HEAD_DIM), functools.partial(lambda b, c: (b, c), c=col)))
            args.append(p)
    in_specs += [pl.BlockSpec((4, B_QK), lambda b: (0, 0)), pl.BlockSpec((1, HEAD_DIM), lambda b: (0, 0))]
    args += [b_lambda_l, b_subln_l.reshape(1, HEAD_DIM)]
    aliases = {}
    if carried is not None:
        for j, buf in enumerate(carried):
            aliases[len(args)] = 2 + j
            in_specs.append(pl.BlockSpec(memory_space=pl.ANY))
            args.append(buf)
    kv_spec = lambda nh: pl.BlockSpec((None, None, nh, seq, HEAD_DIM), lambda b: (b, layer, 0, 0, 0))
    kv_shape = lambda nh: jax.ShapeDtypeStruct((batch, depth, nh, seq, HEAD_DIM), F32)
    out = pl.pallas_call(
        functools.partial(_ctx_attn_kernel, lam_init=lam_init),
        grid=(batch,),
        in_specs=in_specs,
        out_specs=[pl.BlockSpec((seq, aw), lambda b: (b, 0)),
                   pl.BlockSpec((seq, B_HEADS * HEAD_DIM), lambda b: (b, 0)),
                   kv_spec(A_HEADS), kv_spec(A_HEADS), kv_spec(B_HEADS), kv_spec(B_HEADS)],
        out_shape=[jax.ShapeDtypeStruct((m, aw), BF16),
                   jax.ShapeDtypeStruct((m, B_HEADS * HEAD_DIM), BF16),
                   kv_shape(A_HEADS), kv_shape(A_HEADS), kv_shape(B_HEADS), kv_shape(B_HEADS)],
        input_output_aliases=aliases,
        compiler_params=_cparams("arbitrary"),
        name="ctx_attn",
    )(*args)
    return out[0], out[1], tuple(out[2:])


def _na_plan(n_rows):
    wr = min(NA_ROWS, n_rows)
    qrows = NA_QROWS
    uw = -(-(wr + qrows - 1) // 2) * 2
    if n_rows % qrows or n_rows < uw:
        qrows, uw = 1, wr
    rs = lambda r: int(np.clip(r - wr // 2, 0, n_rows - wr))
    starts, var_of_block, variants = [], [], []
    for blk in range(n_rows // qrows):
        q0 = blk * qrows
        ws = int(np.clip(rs(q0), 0, n_rows - uw))
        valid = tuple(tuple(rs(q0 + i) <= ws + j < rs(q0 + i) + wr for j in range(uw)) for i in range(qrows))
        var = (q0 - ws, valid)
        if var not in variants:
            variants.append(var)
        starts.append(ws)
        var_of_block.append(variants.index(var))
    return qrows, uw, starts, var_of_block, variants


def _na_bias_kernel(toe_ref, o_ref, *, n_rows):
    qrows, uw, _, _, variants = _na_plan(n_rows)
    w = GRID_W
    neg = jnp.full((w, w), -jnp.inf, F32)
    for v, (off, valid) in enumerate(variants):
        for i in range(qrows):
            for j in range(uw):
                dr = NA_ROWS - 1 + j - off - i
                tile = toe_ref[dr] if valid[i][j] else neg
                o_ref[v, i * w:(i + 1) * w, j * w:(j + 1) * w] = tile


def _na_bias_table(a_rpb, n_rows):
    depth, heads, nr, _ = a_rpb.shape
    qrows, uw, _, _, variants = _na_plan(n_rows)
    cols = np.arange(GRID_W)
    cs = np.clip(cols - NA_COLS // 2, 0, GRID_W - NA_COLS)
    col_mask = (cols[None, :] >= cs[:, None]) & (cols[None, :] < cs[:, None] + NA_COLS)
    col_idx = np.clip(cols[None, :] - cols[:, None] + NA_COLS - 1, 0, 2 * NA_COLS - 2)
    onehot = (col_idx[None] == np.arange(2 * NA_COLS - 1)[:, None, None]).astype(np.float32)
    toe = jnp.einsum('lhrd,dqk->lhrqk', a_rpb.astype(F32), jnp.asarray(onehot), precision=HI)
    toe = jnp.where(jnp.asarray(col_mask), toe, -jnp.inf)
    shape = (len(variants), qrows * GRID_W, uw * GRID_W)
    return pl.pallas_call(
        functools.partial(_na_bias_kernel, n_rows=n_rows),
        grid=(depth, heads),
        in_specs=[pl.BlockSpec((None, None, nr, GRID_W, GRID_W), lambda l, h: (l, h, 0, 0, 0))],
        out_specs=pl.BlockSpec((None, None) + shape, lambda l, h: (l, h, 0, 0, 0)),
        out_shape=jax.ShapeDtypeStruct((depth, heads) + shape, F32),
        compiler_params=_cparams("arbitrary", "arbitrary"),
        name="na_bias",
    )(toe)


def _select_by_block(blk, values):
    out = jnp.int32(values[0])
    for k in range(1, len(values)):
        out = jnp.where(blk == k, jnp.int32(values[k]), out)
    return out


def _na_kernel(q_ref, k_ref, v_ref, kc_ref, vc_ref, bias_ref, o_ref, kb_scr, vb_scr, kcb_scr, vcb_scr, *, n_rows):
    scale = HEAD_DIM ** -0.5
    qrows, uw, starts, var_of_block, _ = _na_plan(n_rows)
    nq = qrows * GRID_W
    kb_scr[...] = k_ref[...].astype(BF16)
    vb_scr[...] = v_ref[...].astype(BF16)
    kcb_scr[...] = kc_ref[...].astype(BF16)
    vcb_scr[...] = vc_ref[...].astype(BF16)

    def body(blk, carry):
        q0 = pl.multiple_of(blk * nq, nq)
        q = q_ref[pl.ds(q0, nq), :].astype(BF16)
        w0 = pl.multiple_of(_select_by_block(blk, starts) * GRID_W, GRID_W)
        kw = kb_scr[pl.ds(w0, uw * GRID_W), :]
        vw = vb_scr[pl.ds(w0, uw * GRID_W), :]
        s_nb = _dot_nt(q, kw) * scale + bias_ref[_select_by_block(blk, var_of_block)]
        s_c = _dot_nt(q, kcb_scr[...]) * scale
        mx = jnp.maximum(jnp.max(s_nb, axis=-1, keepdims=True), jnp.max(s_c, axis=-1, keepdims=True))
        p_nb = jnp.exp(s_nb - mx)
        p_c = jnp.exp(s_c - mx)
        l = jnp.sum(p_nb, axis=-1, keepdims=True) + jnp.sum(p_c, axis=-1, keepdims=True)
        o = (_dot(p_nb.astype(BF16), vw) + _dot(p_c.astype(BF16), vcb_scr[...])) / l
        o_ref[pl.ds(q0, nq), :] = o.astype(o_ref.dtype)
        return carry

    lax.fori_loop(0, n_rows // qrows, body, 0, unroll=2)


def _na_attn(p, batch, seq, cache_k, cache_v, layer, bias):
    n_rows = seq // GRID_W
    past = cache_k.shape[3]
    blk = lambda off: pl.BlockSpec((seq, HEAD_DIM), lambda b, h: (b, off + h))
    cspec = pl.BlockSpec((None, None, None, past, HEAD_DIM), lambda b, h: (b, layer, h, 0, 0))
    return pl.pallas_call(
        functools.partial(_na_kernel, n_rows=n_rows),
        grid=(batch, A_HEADS),
        in_specs=[blk(0), blk(A_HEADS), blk(2 * A_HEADS), cspec, cspec,
                  pl.BlockSpec((None, None) + bias.shape[2:], lambda b, h: (layer, h, 0, 0, 0))],
        out_specs=pl.BlockSpec((seq, HEAD_DIM), lambda b, h: (b, h)),
        out_shape=jax.ShapeDtypeStruct((batch * seq, A_HEADS * HEAD_DIM), BF16),
        scratch_shapes=[pltpu.VMEM((seq, HEAD_DIM), BF16), pltpu.VMEM((seq, HEAD_DIM), BF16),
                        pltpu.VMEM((past, HEAD_DIM), BF16), pltpu.VMEM((past, HEAD_DIM), BF16)],
        compiler_params=_cparams("arbitrary", "arbitrary"),
        name="na_attn",
    )(p, p, p, cache_k, cache_v, bias)


def _rope_tables(seq):
    t = np.arange(seq)
    rows = (t // GRID_W).astype(np.float32)
    cols = (t % GRID_W).astype(np.float32)
    half = B_QK // 2
    freqs = jnp.asarray(ROPE_BASE, F32) ** (-jnp.arange(0, half, 2, dtype=F32) / half)
    ar = jnp.asarray(rows)[:, None] * freqs
    ac = jnp.asarray(cols)[:, None] * freqs
    cr, sr, cc, sc = jnp.cos(ar), jnp.sin(ar), jnp.cos(ac), jnp.sin(ac)
    cos = jnp.concatenate([cr, cr, cc, cc] * 2, axis=-1)
    sin = jnp.concatenate([-sr, sr, -sc, sc] * 2, axis=-1)
    return cos, sin


def _rope(x, cos, sin):
    q = B_QK // 4
    lane = lax.broadcasted_iota(jnp.int32, x.shape, 1)
    first = (lane & (2 * q - 1)) < q
    partner = jnp.where(first, pltpu.roll(x, LANE - q, 1), pltpu.roll(x, q, 1))
    return x * cos + partner * sin


def _diff_lat_kernel(q_ref, k_ref, v_ref, kc_ref, vc_ref, cosq_ref, sinq_ref, cosk_ref, sink_ref, lam_ref, subln_ref,
                     o_ref, kr_scr, kcb_scr, vb_scr, vcb_scr, *, lam_init):
    scale = B_QK ** -0.5

    @pl.when(pl.program_id(2) == 0)
    def _():
        kr_scr[...] = _rope(k_ref[...], cosk_ref[...], sink_ref[...]).astype(BF16)
        kcb_scr[...] = kc_ref[...].astype(BF16)
        vb_scr[...] = v_ref[...].astype(BF16)
        vcb_scr[...] = vc_ref[...].astype(BF16)

    lam = _lambda_value(lam_ref, lam_init)
    q = q_ref[...] * scale
    qr1, qr2 = _split_maps(_rope(q, cosq_ref[...], sinq_ref[...]))
    q1, q2 = _split_maps(q)

    def probs(qr, qp):
        s_l = _dot_nt(qr.astype(BF16), kr_scr[...])
        s_c = _dot_nt(qp.astype(BF16), kcb_scr[...])
        mx = jnp.maximum(jnp.max(s_l, axis=-1, keepdims=True), jnp.max(s_c, axis=-1, keepdims=True))
        e_l = jnp.exp(s_l - mx)
        e_c = jnp.exp(s_c - mx)
        inv = 1.0 / (jnp.sum(e_l, axis=-1, keepdims=True) + jnp.sum(e_c, axis=-1, keepdims=True))
        return e_l * inv, e_c * inv

    p1_l, p1_c = probs(qr1, q1)
    p2_l, p2_c = probs(qr2, q2)
    o = (_dot((p1_l - lam * p2_l).astype(BF16), vb_scr[...])
         + _dot((p1_c - lam * p2_c).astype(BF16), vcb_scr[...]))
    o_ref[...] = _subnorm(o, subln_ref, lam_init).astype(o_ref.dtype)


def _diff_lat_attn(p, batch, seq, cache_k, cache_v, layer, cos, sin, b_lambda_l, b_subln_l, lam_init):
    past = cache_k.shape[3]
    tq = _tile(seq, 256, SUBLANE)
    nq = seq // tq
    base = 3 * A_HEADS
    cspec = pl.BlockSpec((None, None, None, past, HEAD_DIM), lambda b, h, i: (b, layer, h, 0, 0))
    kv = lambda off: pl.BlockSpec((seq, HEAD_DIM), lambda b, h, i: (b, off + h))
    return pl.pallas_call(
        functools.partial(_diff_lat_kernel, lam_init=lam_init),
        grid=(batch, B_HEADS, nq),
        in_specs=[pl.BlockSpec((tq, HEAD_DIM), lambda b, h, i: (b * nq + i, base + h)),
                  kv(base + B_HEADS), kv(base + 2 * B_HEADS), cspec, cspec,
                  pl.BlockSpec((tq, HEAD_DIM), lambda b, h, i: (i, 0)),
                  pl.BlockSpec((tq, HEAD_DIM), lambda b, h, i: (i, 0)),
                  pl.BlockSpec((seq, HEAD_DIM), lambda b, h, i: (0, 0)),
                  pl.BlockSpec((seq, HEAD_DIM), lambda b, h, i: (0, 0)),
                  pl.BlockSpec((4, B_QK), lambda b, h, i: (0, 0)),
                  pl.BlockSpec((1, HEAD_DIM), lambda b, h, i: (0, 0))],
        out_specs=pl.BlockSpec((tq, HEAD_DIM), lambda b, h, i: (b * nq + i, h)),
        out_shape=jax.ShapeDtypeStruct((batch * seq, B_HEADS * HEAD_DIM), BF16),
        scratch_shapes=[pltpu.VMEM((seq, HEAD_DIM), BF16), pltpu.VMEM((past, HEAD_DIM), BF16),
                        pltpu.VMEM((seq, HEAD_DIM), BF16), pltpu.VMEM((past, HEAD_DIM), BF16)],
        compiler_params=_cparams("arbitrary", "arbitrary", "arbitrary"),
        name="diff_lat_attn",
    )(p, p, p, cache_k, cache_v, cos, sin, cos, sin, b_lambda_l, b_subln_l.reshape(1, HEAD_DIM))


def _log_sigmoid(x):
    return jnp.minimum(x, 0.0) - jnp.log1p(jnp.exp(-jnp.abs(x)))


def _split3(x):
    hi = x.astype(BF16)
    r = x - hi.astype(F32)
    mid = r.astype(BF16)
    lo = (r - mid.astype(F32)).astype(BF16)
    return hi, mid, lo


def _mlstm_kernel(*refs, chunk, n_chunks, zero_init):
    nh = C_HEADS
    q_refs, k_refs, v_refs, og_refs = (refs[i * nh:(i + 1) * nh] for i in range(4))
    g_ref, gb_ref, cn_ref = refs[4 * nh:4 * nh + 3]
    if not zero_init:
        c0_ref, n0_ref, m0_ref = refs[4 * nh + 3:4 * nh + 6]
    y_ref, cf_ref, nf_ref, mf_ref, hf_scr, hb_scr = refs[-6:]
    scale = HEAD_DIM ** -0.5
    ln = chunk
    n_sel = 4 * SUBLANE
    assert 4 * nh <= n_sel
    sel = jnp.where(lax.broadcasted_iota(jnp.int32, (n_sel, LANE), 0)
                    == lax.broadcasted_iota(jnp.int32, (n_sel, LANE), 1), 1.0, 0.0).astype(BF16)
    ti = lax.broadcasted_iota(jnp.int32, (ln, ln), 0)
    si = lax.broadcasted_iota(jnp.int32, (ln, ln), 1)
    lower = si <= ti
    upper = si >= ti
    lower_b = jnp.where(lower, 1.0, 0.0).astype(BF16)
    upper_b = jnp.where(upper, 1.0, 0.0).astype(BF16)

    def gate_terms(r0, directions):
        g = g_ref[pl.ds(r0, ln), :] + gb_ref[...]
        gp = _split3(g)
        rows = sum(_dot_nt(sel, p) for p in gp)
        cols = sum(_dot_nt(p, sel) for p in gp)
        lf_rows = _split3(_log_sigmoid(rows))
        lf_cols = _split3(_log_sigmoid(cols))
        t = dict(rows=rows, cols=cols)
        if 0 in directions:
            t["row0"] = sum(_dot(p, upper_b) for p in lf_rows)
            t["col0"] = sum(_dot(lower_b, p) for p in lf_cols)
        if 1 in directions:
            t["row1"] = sum(_dot(p, lower_b) for p in lf_rows)
            t["col1"] = sum(_dot(upper_b, p) for p in lf_cols)
        return t

    def head_terms(h, r0):
        qf = q_refs[h][pl.ds(r0, ln), :] * scale
        kf = k_refs[h][pl.ds(r0, ln), :]
        qb = qf.astype(BF16)
        vb = v_refs[h][pl.ds(r0, ln), :].astype(BF16)
        return dict(qf=qf, kf=kf, qb=qb, vb=vb, qk=_dot_nt(qb, kf.astype(BF16)))

    def chunk_step(gt, t, h, state, backward):
        rows, cols, qf, kf, qb, vb = gt["rows"], gt["cols"], t["qf"], t["kf"], t["qb"], t["vb"]
        gi = (2 * nh if backward else 0) + h
        gf = gi + nh
        i_row, i_col = rows[gi:gi + 1], cols[:, gi:gi + 1]
        if backward:
            b_row, b_col, mask = gt["row1"][gf:gf + 1], gt["col1"][:, gf:gf + 1], upper
            b_last = b_col[0:1]
        else:
            b_row, b_col, mask = gt["row0"][gf:gf + 1], gt["col0"][:, gf:gf + 1], lower
            b_last = b_col[ln - 1:ln]
        d = jnp.where(mask, b_col - b_row + i_row, -jnp.inf)
        gg = b_last - b_col + i_col
        if state is None:
            m_row = jnp.maximum(jnp.max(d, axis=-1, keepdims=True), b_col)
            sc = t["qk"] * jnp.exp(d - m_row)
            num = _dot(sc.astype(BF16), vb)
            den = jnp.sum(sc, axis=-1, keepdims=True)
            m_new = jnp.maximum(b_last, jnp.max(gg, axis=0, keepdims=True))
            kw = kf * jnp.exp(gg - m_new)
            c_new = _dot_tn(kw.astype(BF16), vb)
            n_new = jnp.sum(kw, axis=0, keepdims=True)
        else:
            c_st, n_st, m_st = state
            inter = b_col + m_st
            m_row = jnp.maximum(jnp.max(d, axis=-1, keepdims=True), inter)
            w_state = jnp.exp(inter - m_row)
            sc = t["qk"] * jnp.exp(d - m_row)
            num = w_state * _dot(qb, c_st.astype(BF16)) + _dot(sc.astype(BF16), vb)
            den = w_state * jnp.sum(qf * n_st, axis=-1, keepdims=True) + jnp.sum(sc, axis=-1, keepdims=True)
            m_new = jnp.maximum(b_last + m_st, jnp.max(gg, axis=0, keepdims=True))
            w_old = jnp.exp(b_last + m_st - m_new)
            kw = kf * jnp.exp(gg - m_new)
            c_new = w_old * c_st + _dot_tn(kw.astype(BF16), vb)
            n_new = w_old * n_st + jnp.sum(kw, axis=0, keepdims=True)
        h_out = num / jnp.maximum(jnp.abs(den), jnp.exp(-m_row))
        return h_out, (c_new, n_new, m_new)

    def start(c):
        return pl.multiple_of(c * ln, ln)

    def finish(h, r0, hs):
        ms = jnp.mean(hs * hs, axis=-1, keepdims=True)
        hn = hs * lax.rsqrt(ms + RMS_EPS) * cn_ref[...]
        og = og_refs[h][pl.ds(r0, ln), :]
        y_ref[pl.ds(r0, ln), h * HEAD_DIM:(h + 1) * HEAD_DIM] = (hn / (1.0 + jnp.exp(-og))).astype(y_ref.dtype)

    if zero_init:
        assert n_chunks == 1
        init = ((None,) * nh,) * 2
    else:
        init = tuple(tuple((c0_ref[dr, h], n0_ref[dr, h], m0_ref[dr, h][:, 0:1]) for h in range(nh))
                     for dr in (0, 1))
    if n_chunks == 1:
        gt = gate_terms(0, (0, 1))
        st_f, st_b = [], []
        for h in range(nh):
            t = head_terms(h, 0)
            h_f, s_f = chunk_step(gt, t, h, init[0][h], False)
            h_b, s_b = chunk_step(gt, t, h, init[1][h], True)
            finish(h, 0, h_f + h_b)
            st_f.append(s_f)
            st_b.append(s_b)
        final = (st_f, st_b)
    else:
        def body(j, states):
            r_f = start(j)
            r_b = start(n_chunks - 1 - j)
            gt_f = gate_terms(r_f, (0,))
            gt_b = gate_terms(r_b, (1,))
            st_f, st_b = [], []
            for h in range(nh):
                sl = slice(h * HEAD_DIM, (h + 1) * HEAD_DIM)
                h_f, s_f = chunk_step(gt_f, head_terms(h, r_f), h, states[0][h], False)
                h_b, s_b = chunk_step(gt_b, head_terms(h, r_b), h, states[1][h], True)
                hf_scr[pl.ds(r_f, ln), sl] = h_f
                hb_scr[pl.ds(r_b, ln), sl] = h_b
                st_f.append(s_f)
                st_b.append(s_b)
            return tuple(st_f), tuple(st_b)

        final = lax.fori_loop(0, n_chunks, body, init)

        def fin_body(j, carry):
            r0 = start(j)
            for h in range(nh):
                sl = slice(h * HEAD_DIM, (h + 1) * HEAD_DIM)
                finish(h, r0, hf_scr[pl.ds(r0, ln), sl] + hb_scr[pl.ds(r0, ln), sl])
            return carry

        lax.fori_loop(0, n_chunks, fin_body, 0)
    for dr in (0, 1):
        for h in range(nh):
            c_f, n_f, m_f = final[dr][h]
            cf_ref[dr, h] = c_f
            nf_ref[dr, h] = n_f
            mf_ref[dr, h] = jnp.broadcast_to(m_f, (1, LANE))


def _mlstm(p, batch, seq, gate_bias, c_norm_l, c0, n0, m0, lin, out_depth, lout, carried):
    chunk = min(MLSTM_CHUNK, seq)
    n_chunks = seq // chunk
    base = 3 * A_HEADS + 3 * B_HEADS
    hd = HEAD_DIM
    nh = C_HEADS
    head_bytes = 4 * nh * seq * hd * 4
    mode = dict(pipeline_mode=pl.Buffered(1)) if 2 * head_bytes > VMEM_LIMIT_BYTES // 2 else {}
    blk = lambda col: pl.BlockSpec((seq, hd), functools.partial(lambda b, c: (b, c), c=col), **mode)
    st_c = lambda l: pl.BlockSpec((None, None, 2, nh, hd, hd), lambda b: (b, l, 0, 0, 0, 0))
    st_n = lambda l: pl.BlockSpec((None, None, 2, nh, 1, hd), lambda b: (b, l, 0, 0, 0, 0))
    if c0 is None and n_chunks > 1:
        c0 = jnp.zeros((batch, 1, 2, nh, hd, hd), F32)
        n0 = jnp.zeros((batch, 1, 2, nh, hd), F32)
        m0 = jnp.zeros((batch, 1, 2, nh), F32)
    zero_init = c0 is None
    in_specs = ([blk(base + seg * nh + h) for seg in range(4) for h in range(nh)]
                + [pl.BlockSpec((seq, LANE), lambda b: (b, base + 4 * nh)),
                   pl.BlockSpec((1, LANE), lambda b: (0, 0)),
                   pl.BlockSpec((1, hd), lambda b: (0, 0))])
    args = [p] * (4 * nh + 1) + [gate_bias, c_norm_l.reshape(1, hd)]
    if not zero_init:
        in_specs += [st_c(lin), st_n(lin), st_n(lin)]
        args += [c0, n0.reshape(n0.shape[:4] + (1, hd)), jnp.broadcast_to(m0[..., None, None], m0.shape + (1, LANE))]
    aliases = {}
    if carried is not None:
        for j, buf in enumerate(carried):
            aliases[len(args)] = 1 + j
            in_specs.append(pl.BlockSpec(memory_space=pl.ANY))
            args.append(buf)
    out = pl.pallas_call(
        functools.partial(_mlstm_kernel, chunk=chunk, n_chunks=n_chunks, zero_init=zero_init),
        grid=(batch,),
        in_specs=in_specs,
        out_specs=[pl.BlockSpec((seq, nh * hd), lambda b: (b, 0)), st_c(lout), st_n(lout), st_n(lout)],
        out_shape=[jax.ShapeDtypeStruct((batch * seq, nh * hd), BF16),
                   jax.ShapeDtypeStruct((batch, out_depth, 2, nh, hd, hd), F32),
                   jax.ShapeDtypeStruct((batch, out_depth, 2, nh, 1, hd), F32),
                   jax.ShapeDtypeStruct((batch, out_depth, 2, nh, 1, LANE), F32)],
        scratch_shapes=[pltpu.VMEM((seq, nh * hd), F32), pltpu.VMEM((seq, nh * hd), F32)],
        input_output_aliases=aliases,
        compiler_params=_cparams("arbitrary"),
        name="mlstm",
    )(*args)
    return out[0], tuple(out[1:])


def _layernorm(z, g_ref, b_ref):
    mu = jnp.mean(z, axis=-1, keepdims=True)
    zc = z - mu
    var = jnp.mean(zc * zc, axis=-1, keepdims=True)
    return zc * lax.rsqrt(var + LN_EPS) * g_ref[...] + b_ref[...]


def _outproj_kernel(ya_ref, yb_ref, yc_ref, w_ref, x_ref, mod_ref, g_ref, b_ref, x1_ref, h2_ref):
    ycat = jnp.concatenate([ya_ref[...], yb_ref[...], yc_ref[...]], axis=-1)
    y = _dot(ycat, w_ref[...])
    gate = mod_ref[0, 2:3, :]
    x1 = _layernorm(ALPHA * x_ref[...] + gate * y, g_ref, b_ref)
    x1_ref[...] = x1
    h2_ref[...] = (x1 * (1.0 + mod_ref[0, 4:5, :]) + mod_ref[0, 3:4, :]).astype(BF16)


def _outproj(ya, yb, yc, w_bf, layer, x, mods_l, ln_g, ln_b, row_base, rows_per_cond):
    m, d = x.shape
    tm = min(_tile(m, 512, SUBLANE), rows_per_cond)
    per = rows_per_cond // tm
    row = lambda w: pl.BlockSpec((tm, w), lambda i: (i, 0))
    vec = pl.BlockSpec((1, d), lambda i: (0, 0))
    return pl.pallas_call(
        _outproj_kernel,
        grid=(m // tm,),
        in_specs=[row(ya.shape[1]), row(yb.shape[1]), row(yc.shape[1]),
                  pl.BlockSpec((None, d, d), lambda i: (layer, 0, 0)), row(d),
                  pl.BlockSpec((1, 6, d), lambda i: (row_base + i // per, 0, 0)), vec, vec],
        out_specs=[row(d), row(d)],
        out_shape=[jax.ShapeDtypeStruct((m, d), F32), jax.ShapeDtypeStruct((m, d), BF16)],
        compiler_params=_cparams("arbitrary"),
        name="outproj_ln",
    )(ya, yb, yc, w_bf, x, mods_l, ln_g.reshape(1, d), ln_b.reshape(1, d))


def _ffn_kernel(h_ref, hp_ref, hn_ref, wa_ref, wg_ref, cwa_ref, cwg_ref, cba_ref, cbg_ref, wd_ref, x_ref, mod_ref,
                g_ref, b_ref, o_ref, hext_scr, acc_scr, *, seq):
    i = pl.program_id(0)
    f = pl.program_id(1)
    tm = h_ref.shape[0]
    halo = BF16_ROWS

    @pl.when(f == 0)
    def _():
        hext_scr[0:halo, :] = hp_ref[...]
        hext_scr[halo:halo + tm, :] = h_ref[...]
        hext_scr[halo + tm:, :] = hn_ref[...]
        acc_scr[...] = jnp.zeros_like(acc_scr)

    period = min(seq, tm)
    grp = lax.broadcasted_iota(jnp.int32, (SUBLANE, 1), 0)

    def mask_rows(x, first):
        pieces = []
        for r0 in range(0, tm, period):
            g0 = r0 if first else r0 + period - SUBLANE
            pos = (i * tm + g0 + grp) & (seq - 1)
            keep = (pos != 0) if first else (pos != seq - 1)
            masked = jnp.where(keep, x[g0:g0 + SUBLANE], 0.0)
            rest = x[r0 + SUBLANE:r0 + period] if first else x[r0:g0]
            pieces += [masked, rest] if first else [rest, masked]
        return jnp.concatenate(pieces, axis=0)

    def up_conv(w_ref, cw_ref, cb_ref):
        u = _dot(hext_scr[...], w_ref[...])
        rows = u.shape[0]
        u_prev = mask_rows(pltpu.roll(u, 1, 0)[halo:halo + tm], True)
        u_next = mask_rows(pltpu.roll(u, rows - 1, 0)[halo:halo + tm], False)
        return u_prev * cw_ref[0:1, :] + u[halo:halo + tm] * cw_ref[1:2, :] + u_next * cw_ref[2:3, :] + cb_ref[...]

    a = up_conv(wa_ref, cwa_ref, cba_ref)
    g = up_conv(wg_ref, cwg_ref, cbg_ref)
    act = (g / (1.0 + jnp.exp2(g * NEG_LOG2_E))) * a
    acc_scr[...] += _dot(act.astype(BF16), wd_ref[...])

    @pl.when(f == pl.num_programs(1) - 1)
    def _():
        z = ALPHA * x_ref[...] + mod_ref[0, 5:6, :] * acc_scr[...]
        o_ref[...] = _layernorm(z, g_ref, b_ref)


def _ffn(h2, x1, ffn_w, layer, mods_l, ln_g, ln_b, row_base, rows_per_cond, seq, tf):
    wu_bf, cw, cb, wd_bf = ffn_w
    m, d = x1.shape
    fp = wd_bf.shape[1]
    tm = min(_tile(m, 512, SUBLANE), rows_per_cond)
    per = rows_per_cond // tm
    assert seq & (seq - 1) == 0 and m % seq == 0
    nh = m // BF16_ROWS
    hb = tm // BF16_ROWS
    vec = pl.BlockSpec((1, d), lambda i, f: (0, 0))
    wcol = lambda rows, half: pl.BlockSpec((None, None, rows, tf),
                                           functools.partial(lambda i, f, hf: (layer, hf, 0, f), hf=half))
    return pl.pallas_call(
        functools.partial(_ffn_kernel, seq=seq),
        grid=(m // tm, fp // tf),
        in_specs=[pl.BlockSpec((tm, d), lambda i, f: (i, 0)),
                  pl.BlockSpec((BF16_ROWS, d), lambda i, f: (jnp.maximum(i * hb - 1, 0), 0)),
                  pl.BlockSpec((BF16_ROWS, d), lambda i, f: (jnp.minimum((i + 1) * hb, nh - 1), 0)),
                  wcol(d, 0), wcol(d, 1), wcol(CONV_W, 0), wcol(CONV_W, 1), wcol(1, 0), wcol(1, 1),
                  pl.BlockSpec((None, tf, d), lambda i, f: (layer, f, 0)),
                  pl.BlockSpec((tm, d), lambda i, f: (i, 0)),
                  pl.BlockSpec((1, 6, d), lambda i, f: (row_base + i // per, 0, 0)), vec, vec],
        out_specs=pl.BlockSpec((tm, d), lambda i, f: (i, 0)),
        out_shape=jax.ShapeDtypeStruct((m, d), F32),
        scratch_shapes=[pltpu.VMEM((tm + 2 * BF16_ROWS, d), BF16), pltpu.VMEM((tm, d), F32)],
        compiler_params=_cparams("arbitrary", "arbitrary"),
        name="ffn_ln",
    )(h2, h2, h2, wu_bf, wu_bf, cw, cw, cb, cb, wd_bf, x1, mods_l, ln_g.reshape(1, d), ln_b.reshape(1, d))


def _pad_cols(w, n):
    return jnp.pad(w, [(0, 0)] * (w.ndim - 1) + [(0, n - w.shape[-1])])


def _cast_kernel(x_ref, o_ref, *, rows_valid, cols_valid):
    tr, tc = x_ref.shape
    x = x_ref[...]
    if rows_valid is not None:
        r = pl.program_id(1) * tr + lax.broadcasted_iota(jnp.int32, (tr, tc), 0)
        x = jnp.where(r < rows_valid, x, 0.0)
    if cols_valid is not None:
        c = pl.program_id(2) * tc + lax.broadcasted_iota(jnp.int32, (tr, tc), 1)
        x = jnp.where(c < cols_valid, x, 0.0)
    o_ref[:, :tc] = x.astype(BF16)
    if o_ref.shape[1] > tc:
        o_ref[:, tc:] = jnp.zeros((tr, o_ref.shape[1] - tc), BF16)


def _cast_weights(w, tr, tc, rows_out, cols_out):
    depth, r, c = w.shape
    nr, nc = -(-rows_out // tr), -(-cols_out // tc)
    assert nr * tr == rows_out and nc * tc == cols_out
    return pl.pallas_call(
        functools.partial(_cast_kernel, rows_valid=r if rows_out > r else None, cols_valid=c if cols_out > c else None),
        grid=(depth, nr, nc),
        in_specs=[pl.BlockSpec((None, tr, tc), lambda l, i, j: (l, i, j))],
        out_specs=pl.BlockSpec((None, tr, tc), lambda l, i, j: (l, i, j)),
        out_shape=jax.ShapeDtypeStruct((depth, rows_out, cols_out), BF16),
        compiler_params=_cparams("arbitrary", "arbitrary", "arbitrary"),
        name="cast_weights",
    )(w)


def _cast_in_kernel(x_ref, o_ref, *, n_valid):
    tn = x_ref.shape[0]
    col = pl.program_id(0) * tn + lax.broadcasted_iota(jnp.int32, (o_ref.shape[1], tn), 1)
    for l in range(x_ref.shape[1]):
        o_ref[l] = jnp.where(col < n_valid, x_ref[:, l, :].T, 0.0).astype(BF16)


def _cast_in_weights(w_in, n_pad, tn):
    depth, d, n = w_in.shape
    return pl.pallas_call(
        functools.partial(_cast_in_kernel, n_valid=n),
        grid=(n_pad // tn,),
        in_specs=[pl.BlockSpec((tn, depth, d), lambda j: (j, 0, 0))],
        out_specs=pl.BlockSpec((depth, d, tn), lambda j: (0, 0, j)),
        out_shape=jax.ShapeDtypeStruct((depth, d, n_pad), BF16),
        compiler_params=_cparams("arbitrary"),
        name="cast_in_weights",
    )(jnp.transpose(w_in, (2, 0, 1)))


def _cast_up_weights(w_up, dff, fp, tr):
    depth, d, _ = w_up.shape
    return pl.pallas_call(
        functools.partial(_cast_kernel, rows_valid=None, cols_valid=None),
        grid=(depth, d // tr, 2),
        in_specs=[pl.BlockSpec((None, tr, dff), lambda l, i, j: (l, i, j))],
        out_specs=pl.BlockSpec((None, None, tr, fp), lambda l, i, j: (l, j, i, 0)),
        out_shape=jax.ShapeDtypeStruct((depth, 2, d, fp), BF16),
        compiler_params=_cparams("arbitrary", "arbitrary", "arbitrary"),
        name="cast_up_weights",
    )(w_up)


def _prep_weights(w_in, w_out, w_up, conv_w, conv_b, w_down, tf):
    depth, d, n_in = w_in.shape
    n_pad = -(-n_in // LANE) * LANE
    dff = w_down.shape[1]
    fp = -(-dff // tf) * tf
    w_in_bf = _cast_in_weights(w_in, n_pad, _tile(n_pad, 384))
    w_out_bf = _cast_weights(w_out, _tile(d, 1024, SUBLANE), d, d, d)
    w_up_bf = _cast_up_weights(w_up, dff, fp, _tile(d, 512, SUBLANE))
    w_down_bf = _cast_weights(w_down, _tile(fp, 1536, tf), d, fp, d)
    halves = lambda t: jnp.stack([_pad_cols(t[..., :dff], fp), _pad_cols(t[..., dff:], fp)], axis=1)
    return w_in_bf, w_out_bf, w_up_bf, halves(conv_w), halves(conv_b[:, None, :]), w_down_bf


def _gate_bias_row(c_gate_b_l):
    return _pad_cols(c_gate_b_l.reshape(1, -1), LANE)


def kernel(x_prompt, x_sample, cache_a_k, cache_a_v, cache_b_k, cache_b_v, state_c_C, state_c_n, state_c_m, c, c_ctx, w_mod, b_mod, w_in, c_gate_b, a_rpb, b_lambda, b_subln, c_norm, w_out, ln1_g, ln1_b, ln2_g, ln2_b, w_up, conv_w, conv_b, w_down):
    batch, seq, d = x_prompt.shape
    dec_batch, dec_seq, _ = x_sample.shape
    depth = w_in.shape[0]
    tf = 512

    cond = jnp.concatenate([c_ctx[None, :], c], axis=0)
    cond = jnp.pad(cond, ((0, SUBLANE - cond.shape[0]), (0, 0)))
    mods = _mods(cond, w_mod, b_mod).reshape(depth, SUBLANE, 6, d)

    xp = x_prompt.reshape(batch * seq, d)
    xs = x_sample.reshape(dec_batch * dec_seq, d)
    cos, sin = _rope_tables(dec_seq)
    na_bias = _na_bias_table(a_rpb, dec_seq // GRID_W)

    w_in_bf, w_out_bf, *ffn_w = _prep_weights(w_in, w_out, w_up, conv_w, conv_b, w_down, tf)
    new_kv = None
    new_state = None
    for l in range(depth):
        lam_init = 0.8 - 0.6 * math.exp(-0.3 * l)
        gate_bias = _gate_bias_row(c_gate_b[l])

        pp = _inproj(xp, mods[l], w_in_bf, l, 0, batch * seq)
        ya, yb, new_kv = _ctx_attn(pp, batch, seq, b_lambda[l], b_subln[l], lam_init, l, depth, new_kv)
        yc, new_state = _mlstm(pp, batch, seq, gate_bias, c_norm[l], None, None, None, 0, depth, l, new_state)
        x1, h2 = _outproj(ya, yb, yc, w_out_bf, l, xp, mods[l], ln1_g[l], ln1_b[l], 0, batch * seq)
        xp = _ffn(h2, x1, ffn_w, l, mods[l], ln2_g[l], ln2_b[l], 0, batch * seq, seq, tf)

        ps = _inproj(xs, mods[l], w_in_bf, l, 1, dec_seq)
        ya = _na_attn(ps, dec_batch, dec_seq, cache_a_k, cache_a_v, l, na_bias)
        yb = _diff_lat_attn(ps, dec_batch, dec_seq, cache_b_k, cache_b_v, l, cos, sin, b_lambda[l], b_subln[l], lam_init)
        yc, _ = _mlstm(ps, dec_batch, dec_seq, gate_bias, c_norm[l], state_c_C, state_c_n, state_c_m, l, 1, 0, None)
        x1, h2 = _outproj(ya, yb, yc, w_out_bf, l, xs, mods[l], ln1_g[l], ln1_b[l], 1, dec_seq)
        xs = _ffn(h2, x1, ffn_w, l, mods[l], ln2_g[l], ln2_b[l], 1, dec_seq, dec_seq, tf)

    c_f, n_f, m_f = new_state
    return (xp.reshape(batch, seq, d), xs.reshape(dec_batch, dec_seq, d), *new_kv,
            c_f, n_f[..., 0, :], m_f[..., 0, 0])
```

```python
import functools
import math

import jax
import jax.numpy as jnp
import numpy as np
from jax import lax
from jax.experimental import pallas as pl
from jax.experimental.pallas import tpu as pltpu

F32 = jnp.float32
BF16 = jnp.bfloat16

DEPTH = 2
GRID_W = 64
A_HEADS = 6
NA_ROWS = 8
NA_COLS = 16
NA_QROWS = 4
B_HEADS = 5
B_QK = 64
C_HEADS = 5
HEAD_DIM = 128
CONV_W = 3
ROPE_BASE = 10000.0
LN_EPS = 1e-5
RMS_EPS = 1e-6
ALPHA = (2 * DEPTH) ** 0.25
NEG_LOG2_E = -math.log2(math.e)

LANE = 128
SUBLANE = 8
BF16_ROWS = 16
VMEM_LIMIT_BYTES = 56 * 1024 * 1024

MLSTM_CHUNK = 256
FFN_ROW_CHUNK = 256
OUT_ROW_CHUNK = 128
HI = lax.Precision.HIGHEST


def _tile(n, target, unit=LANE):
    if n <= target:
        return n
    best = unit
    for t in range(unit, target + 1, unit):
        if n % t == 0:
            best = t
    assert n % best == 0, (n, target, unit)
    return best


def _cparams(*sem):
    return pltpu.CompilerParams(dimension_semantics=sem, vmem_limit_bytes=VMEM_LIMIT_BYTES)


def _dot(a, b):
    return jnp.dot(a, b, preferred_element_type=F32)


def _dot_nt(a, b, precision=None):
    return lax.dot_general(a, b, (((1,), (1,)), ((), ())), preferred_element_type=F32, precision=precision)


def _dot_tn(a, b):
    return lax.dot_general(a, b, (((0,), (0,)), ((), ())), preferred_element_type=F32)


def _mods_kernel(cond_ref, w_ref, b_ref, o_ref):
    c = cond_ref[...]
    s = c / (1.0 + jnp.exp(-c))
    o_ref[0] = _dot(s.astype(BF16), w_ref[0].astype(BF16)) + b_ref[0]


def _mods(cond, w_mod, b_mod):
    depth, d, n = w_mod.shape
    tn = _tile(n, 1536)
    return pl.pallas_call(
        _mods_kernel,
        grid=(depth, n // tn),
        in_specs=[pl.BlockSpec((SUBLANE, d), lambda l, j: (0, 0)),
                  pl.BlockSpec((1, d, tn), lambda l, j: (l, 0, j)),
                  pl.BlockSpec((1, 1, tn), lambda l, j: (l, 0, j))],
        out_specs=pl.BlockSpec((1, SUBLANE, tn), lambda l, j: (l, 0, j)),
        out_shape=jax.ShapeDtypeStruct((depth, SUBLANE, n), F32),
        compiler_params=_cparams("arbitrary", "arbitrary"),
        name="mods",
    )(cond, w_mod, b_mod.reshape(depth, 1, n))


def _inproj_kernel(x_ref, mod_ref, w_ref, o_ref, h_scr):
    @pl.when(pl.program_id(1) == 0)
    def _():
        sh = mod_ref[0, 0:1, :]
        sc = mod_ref[0, 1:2, :]
        h_scr[...] = (x_ref[...] * (1.0 + sc) + sh).astype(BF16)

    o_ref[...] = _dot(h_scr[...], w_ref[...])


def _inproj(x, mods_l, w_bf, layer, row_base, rows_per_cond):
    m, d = x.shape
    n = w_bf.shape[2]
    tm = _tile(m, 1024, SUBLANE)
    tm = min(tm, rows_per_cond)
    tn = _tile(n, 1152)
    per = rows_per_cond // tm
    return pl.pallas_call(
        _inproj_kernel,
        grid=(m // tm, n // tn),
        in_specs=[pl.BlockSpec((tm, d), lambda i, j: (i, 0)),
                  pl.BlockSpec((1, 6, d), lambda i, j: (row_base + i // per, 0, 0)),
                  pl.BlockSpec((None, d, tn), lambda i, j: (layer, 0, j))],
        out_specs=pl.BlockSpec((tm, tn), lambda i, j: (i, j)),
        out_shape=jax.ShapeDtypeStruct((m, n), F32),
        scratch_shapes=[pltpu.VMEM((tm, d), BF16)],
        compiler_params=_cparams("arbitrary", "arbitrary"),
        name="inproj",
    )(x, mods_l, w_bf)


def _lambda_value(lam_ref, lam_init):
    lf = lam_ref[...]
    t1 = jnp.sum(lf[0:1] * lf[1:2], axis=1, keepdims=True)
    t2 = jnp.sum(lf[2:3] * lf[3:4], axis=1, keepdims=True)
    return jnp.exp(t1) - jnp.exp(t2) + lam_init


def _split_maps(q):
    lane = lax.broadcasted_iota(jnp.int32, q.shape, 1)
    first = lane < B_QK
    return jnp.where(first, q, 0.0), jnp.where(first, 0.0, q)


def _subnorm(o, g_ref, lam_init):
    ms = jnp.mean(o * o, axis=-1, keepdims=True)
    return o * lax.rsqrt(ms + RMS_EPS) * g_ref[...] * (1.0 - lam_init)


def _ctx_attn_kernel(*refs, lam_init):
    qa_ref, ka_ref, va_ref = refs[0:3]
    b_refs = refs[3:3 + 3 * B_HEADS]
    lam_ref, subln_ref = refs[3 + 3 * B_HEADS:5 + 3 * B_HEADS]
    ya_ref, yb_ref, nak_ref, nav_ref, nbk_ref, nbv_ref = refs[-6:]
    a_scale = HEAD_DIM ** -0.5
    b_scale = B_QK ** -0.5
    for h in range(A_HEADS):
        sl = slice(h * HEAD_DIM, (h + 1) * HEAD_DIM)
        nak_ref[h] = ka_ref[:, sl]
        nav_ref[h] = va_ref[:, sl]
        q = qa_ref[:, sl].astype(BF16)
        k = ka_ref[:, sl].astype(BF16)
        v = va_ref[:, sl].astype(BF16)
        s = _dot_nt(q, k) * a_scale
        e = jnp.exp(s - jnp.max(s, axis=-1, keepdims=True))
        l = jnp.sum(e, axis=-1, keepdims=True)
        ya_ref[:, sl] = (_dot(e.astype(BF16), v) / l).astype(ya_ref.dtype)
    lam = _lambda_value(lam_ref, lam_init)
    for h in range(B_HEADS):
        sl = slice(h * HEAD_DIM, (h + 1) * HEAD_DIM)
        q1, q2 = _split_maps(b_refs[3 * h][...] * b_scale)
        nbk_ref[h] = b_refs[3 * h + 1][...]
        nbv_ref[h] = b_refs[3 * h + 2][...]
        k = b_refs[3 * h + 1][...].astype(BF16)
        v = b_refs[3 * h + 2][...].astype(BF16)
        s1 = _dot_nt(q1.astype(BF16), k)
        s2 = _dot_nt(q2.astype(BF16), k)
        e1 = jnp.exp(s1 - jnp.max(s1, axis=-1, keepdims=True))
        e2 = jnp.exp(s2 - jnp.max(s2, axis=-1, keepdims=True))
        p = e1 / jnp.sum(e1, axis=-1, keepdims=True) - lam * (e2 / jnp.sum(e2, axis=-1, keepdims=True))
        o = _dot(p.astype(BF16), v)
        yb_ref[:, sl] = _subnorm(o, subln_ref, lam_init).astype(yb_ref.dtype)


def _ctx_attn(p, batch, seq, b_lambda_l, b_subln_l, lam_init, layer, depth, carried):
    m = p.shape[0]
    aw = A_HEADS * HEAD_DIM
    nb = aw // HEAD_DIM
    in_specs = [pl.BlockSpec((seq, aw), lambda b: (b, 0)),
                pl.BlockSpec((seq, aw), lambda b: (b, 1)),
                pl.BlockSpec((seq, aw), lambda b: (b, 2))]
    args = [p, p, p]
    for h in range(B_HEADS):
        for seg in range(3):
            col = 3 * nb + seg * B_HEADS + h
            in_specs.append(pl.BlockSpec((seq, HEAD_DIM), functools.partial(lambda b, c: (b, c), c=col)))
            args.append(p)
    in_specs += [pl.BlockSpec((4, B_QK), lambda b: (0, 0)), pl.BlockSpec((1, HEAD_DIM), lambda b: (0, 0))]
    args += [b_lambda_l, b_subln_l.reshape(1, HEAD_DIM)]
    aliases = {}
    if carried is not None:
        for j, buf in enumerate(carried):
            aliases[len(args)] = 2 + j
            in_specs.append(pl.BlockSpec(memory_space=pl.ANY))
            args.append(buf)
    kv_spec = lambda nh: pl.BlockSpec((None, None, nh, seq, HEAD_DIM), lambda b: (b, layer, 0, 0, 0))
    kv_shape = lambda nh: jax.ShapeDtypeStruct((batch, depth, nh, seq, HEAD_DIM), F32)
    out = pl.pallas_call(
        functools.partial(_ctx_attn_kernel, lam_init=lam_init),
        grid=(batch,),
        in_specs=in_specs,
        out_specs=[pl.BlockSpec((seq, aw), lambda b: (b, 0)),
                   pl.BlockSpec((seq, B_HEADS * HEAD_DIM), lambda b: (b, 0)),
                   kv_spec(A_HEADS), kv_spec(A_HEADS), kv_spec(B_HEADS), kv_spec(B_HEADS)],
        out_shape=[jax.ShapeDtypeStruct((m, aw), BF16),
                   jax.ShapeDtypeStruct((m, B_HEADS * HEAD_DIM), BF16),
                   kv_shape(A_HEADS), kv_shape(A_HEADS), kv_shape(B_HEADS), kv_shape(B_HEADS)],
        input_output_aliases=aliases,
        compiler_params=_cparams("arbitrary"),
        name="ctx_attn",
    )(*args)
    return out[0], out[1], tuple(out[2:])


def _na_plan(n_rows):
    wr = min(NA_ROWS, n_rows)
    qrows = NA_QROWS
    uw = -(-(wr + qrows - 1) // 2) * 2
    if n_rows % qrows or n_rows < uw:
        qrows, uw = 1, wr
    rs = lambda r: int(np.clip(r - wr // 2, 0, n_rows - wr))
    starts, var_of_block, variants = [], [], []
    for blk in range(n_rows // qrows):
        q0 = blk * qrows
        ws = int(np.clip(rs(q0), 0, n_rows - uw))
        valid = tuple(tuple(rs(q0 + i) <= ws + j < rs(q0 + i) + wr for j in range(uw)) for i in range(qrows))
        var = (q0 - ws, valid)
        if var not in variants:
            variants.append(var)
        starts.append(ws)
        var_of_block.append(variants.index(var))
    return qrows, uw, starts, var_of_block, variants


def _na_bias_kernel(toe_ref, o_ref, *, n_rows):
    qrows, uw, _, _, variants = _na_plan(n_rows)
    w = GRID_W
    neg = jnp.full((w, w), -jnp.inf, F32)
    for v, (off, valid) in enumerate(variants):
        for i in range(qrows):
            for j in range(uw):
                dr = NA_ROWS - 1 + j - off - i
                tile = toe_ref[dr] if valid[i][j] else neg
                o_ref[v, i * w:(i + 1) * w, j * w:(j + 1) * w] = tile


def _na_bias_table(a_rpb, n_rows):
    depth, heads, nr, _ = a_rpb.shape
    qrows, uw, _, _, variants = _na_plan(n_rows)
    cols = np.arange(GRID_W)
    cs = np.clip(cols - NA_COLS // 2, 0, GRID_W - NA_COLS)
    col_mask = (cols[None, :] >= cs[:, None]) & (cols[None, :] < cs[:, None] + NA_COLS)
    col_idx = np.clip(cols[None, :] - cols[:, None] + NA_COLS - 1, 0, 2 * NA_COLS - 2)
    onehot = (col_idx[None] == np.arange(2 * NA_COLS - 1)[:, None, None]).astype(np.float32)
    toe = jnp.einsum('lhrd,dqk->lhrqk', a_rpb.astype(F32), jnp.asarray(onehot), precision=HI)
    toe = jnp.where(jnp.asarray(col_mask), toe, -jnp.inf)
    shape = (len(variants), qrows * GRID_W, uw * GRID_W)
    return pl.pallas_call(
        functools.partial(_na_bias_kernel, n_rows=n_rows),
        grid=(depth, heads),
        in_specs=[pl.BlockSpec((None, None, nr, GRID_W, GRID_W), lambda l, h: (l, h, 0, 0, 0))],
        out_specs=pl.BlockSpec((None, None) + shape, lambda l, h: (l, h, 0, 0, 0)),
        out_shape=jax.ShapeDtypeStruct((depth, heads) + shape, F32),
        compiler_params=_cparams("arbitrary", "arbitrary"),
        name="na_bias",
    )(toe)


def _select_by_block(blk, values):
    out = jnp.int32(values[0])
    for k in range(1, len(values)):
        out = jnp.where(blk == k, jnp.int32(values[k]), out)
    return out


def _na_kernel(q_ref, k_ref, v_ref, kc_ref, vc_ref, bias_ref, o_ref, kb_scr, vb_scr, kcb_scr, vcb_scr, *, n_rows):
    scale = HEAD_DIM ** -0.5
    qrows, uw, starts, var_of_block, _ = _na_plan(n_rows)
    nq = qrows * GRID_W
    kb_scr[...] = k_ref[...].astype(BF16)
    vb_scr[...] = v_ref[...].astype(BF16)
    kcb_scr[...] = kc_ref[...].astype(BF16)
    vcb_scr[...] = vc_ref[...].astype(BF16)

    def scores(blk):
        q0 = pl.multiple_of(blk * nq, nq)
        q = q_ref[pl.ds(q0, nq), :].astype(BF16)
        w0 = pl.multiple_of(_select_by_block(blk, starts) * GRID_W, GRID_W)
        s_nb = _dot_nt(q, kb_scr[pl.ds(w0, uw * GRID_W), :])
        s_c = _dot_nt(q, kcb_scr[...])
        return q0, w0, s_nb, s_c

    def finish(blk, q0, w0, s_nb, s_c):
        s_nb = s_nb * scale + bias_ref[_select_by_block(blk, var_of_block)]
        s_c = s_c * scale
        mx = jnp.maximum(jnp.max(s_nb, axis=-1, keepdims=True), jnp.max(s_c, axis=-1, keepdims=True))
        p_nb = jnp.exp(s_nb - mx)
        p_c = jnp.exp(s_c - mx)
        l = jnp.sum(p_nb, axis=-1, keepdims=True) + jnp.sum(p_c, axis=-1, keepdims=True)
        vw = vb_scr[pl.ds(w0, uw * GRID_W), :]
        o = (_dot(p_nb.astype(BF16), vw) + _dot(p_c.astype(BF16), vcb_scr[...])) / l
        o_ref[pl.ds(q0, nq), :] = o.astype(o_ref.dtype)

    n_blocks = n_rows // qrows
    group = 2 if n_blocks % 2 == 0 else 1

    def body(j, carry):
        blks = [j * group + g for g in range(group)]
        pending = [scores(b) for b in blks]
        for b, s in zip(blks, pending):
            finish(b, *s)
        return carry

    lax.fori_loop(0, n_blocks // group, body, 0)


def _na_attn(p, batch, seq, cache_k, cache_v, layer, bias):
    n_rows = seq // GRID_W
    past = cache_k.shape[3]
    blk = lambda off: pl.BlockSpec((seq, HEAD_DIM), lambda b, h: (b, off + h))
    cspec = pl.BlockSpec((None, None, None, past, HEAD_DIM), lambda b, h: (b, layer, h, 0, 0))
    return pl.pallas_call(
        functools.partial(_na_kernel, n_rows=n_rows),
        grid=(batch, A_HEADS),
        in_specs=[blk(0), blk(A_HEADS), blk(2 * A_HEADS), cspec, cspec,
                  pl.BlockSpec((None, None) + bias.shape[2:], lambda b, h: (layer, h, 0, 0, 0))],
        out_specs=pl.BlockSpec((seq, HEAD_DIM), lambda b, h: (b, h)),
        out_shape=jax.ShapeDtypeStruct((batch * seq, A_HEADS * HEAD_DIM), BF16),
        scratch_shapes=[pltpu.VMEM((seq, HEAD_DIM), BF16), pltpu.VMEM((seq, HEAD_DIM), BF16),
                        pltpu.VMEM((past, HEAD_DIM), BF16), pltpu.VMEM((past, HEAD_DIM), BF16)],
        compiler_params=_cparams("arbitrary", "arbitrary"),
        name="na_attn",
    )(p, p, p, cache_k, cache_v, bias)


def _rope_tables(seq):
    t = np.arange(seq)
    rows = (t // GRID_W).astype(np.float32)
    cols = (t % GRID_W).astype(np.float32)
    half = B_QK // 2
    freqs = jnp.asarray(ROPE_BASE, F32) ** (-jnp.arange(0, half, 2, dtype=F32) / half)
    ar = jnp.asarray(rows)[:, None] * freqs
    ac = jnp.asarray(cols)[:, None] * freqs
    cr, sr, cc, sc = jnp.cos(ar), jnp.sin(ar), jnp.cos(ac), jnp.sin(ac)
    cos = jnp.concatenate([cr, cr, cc, cc] * 2, axis=-1)
    sin = jnp.concatenate([-sr, sr, -sc, sc] * 2, axis=-1)
    return cos, sin


def _rope(x, cos, sin):
    q = B_QK // 4
    lane = lax.broadcasted_iota(jnp.int32, x.shape, 1)
    first = (lane & (2 * q - 1)) < q
    partner = jnp.where(first, pltpu.roll(x, LANE - q, 1), pltpu.roll(x, q, 1))
    return x * cos + partner * sin


def _diff_lat_kernel(q_ref, k_ref, v_ref, kc_ref, vc_ref, cosq_ref, sinq_ref, cosk_ref, sink_ref, lam_ref, subln_ref,
                     o_ref, kr_scr, kcb_scr, vb_scr, vcb_scr, *, lam_init):
    scale = B_QK ** -0.5

    @pl.when(pl.program_id(2) == 0)
    def _():
        kr_scr[...] = _rope(k_ref[...], cosk_ref[...], sink_ref[...]).astype(BF16)
        kcb_scr[...] = kc_ref[...].astype(BF16)
        vb_scr[...] = v_ref[...].astype(BF16)
        vcb_scr[...] = vc_ref[...].astype(BF16)

    lam = _lambda_value(lam_ref, lam_init)
    q = q_ref[...] * scale
    qr1, qr2 = _split_maps(_rope(q, cosq_ref[...], sinq_ref[...]))
    q1, q2 = _split_maps(q)

    def probs(qr, qp):
        s_l = _dot_nt(qr.astype(BF16), kr_scr[...])
        s_c = _dot_nt(qp.astype(BF16), kcb_scr[...])
        mx = jnp.maximum(jnp.max(s_l, axis=-1, keepdims=True), jnp.max(s_c, axis=-1, keepdims=True))
        e_l = jnp.exp(s_l - mx)
        e_c = jnp.exp(s_c - mx)
        inv = 1.0 / (jnp.sum(e_l, axis=-1, keepdims=True) + jnp.sum(e_c, axis=-1, keepdims=True))
        return e_l * inv, e_c * inv

    p1_l, p1_c = probs(qr1, q1)
    p2_l, p2_c = probs(qr2, q2)
    o = (_dot((p1_l - lam * p2_l).astype(BF16), vb_scr[...])
         + _dot((p1_c - lam * p2_c).astype(BF16), vcb_scr[...]))
    o_ref[...] = _subnorm(o, subln_ref, lam_init).astype(o_ref.dtype)


def _diff_lat_attn(p, batch, seq, cache_k, cache_v, layer, cos, sin, b_lambda_l, b_subln_l, lam_init):
    past = cache_k.shape[3]
    tq = _tile(seq, 256, SUBLANE)
    nq = seq // tq
    base = 3 * A_HEADS
    cspec = pl.BlockSpec((None, None, None, past, HEAD_DIM), lambda b, h, i: (b, layer, h, 0, 0))
    kv = lambda off: pl.BlockSpec((seq, HEAD_DIM), lambda b, h, i: (b, off + h))
    return pl.pallas_call(
        functools.partial(_diff_lat_kernel, lam_init=lam_init),
        grid=(batch, B_HEADS, nq),
        in_specs=[pl.BlockSpec((tq, HEAD_DIM), lambda b, h, i: (b * nq + i, base + h)),
                  kv(base + B_HEADS), kv(base + 2 * B_HEADS), cspec, cspec,
                  pl.BlockSpec((tq, HEAD_DIM), lambda b, h, i: (i, 0)),
                  pl.BlockSpec((tq, HEAD_DIM), lambda b, h, i: (i, 0)),
                  pl.BlockSpec((seq, HEAD_DIM), lambda b, h, i: (0, 0)),
                  pl.BlockSpec((seq, HEAD_DIM), lambda b, h, i: (0, 0)),
                  pl.BlockSpec((4, B_QK), lambda b, h, i: (0, 0)),
                  pl.BlockSpec((1, HEAD_DIM), lambda b, h, i: (0, 0))],
        out_specs=pl.BlockSpec((tq, HEAD_DIM), lambda b, h, i: (b * nq + i, h)),
        out_shape=jax.ShapeDtypeStruct((batch * seq, B_HEADS * HEAD_DIM), BF16),
        scratch_shapes=[pltpu.VMEM((seq, HEAD_DIM), BF16), pltpu.VMEM((past, HEAD_DIM), BF16),
                        pltpu.VMEM((seq, HEAD_DIM), BF16), pltpu.VMEM((past, HEAD_DIM), BF16)],
        compiler_params=_cparams("arbitrary", "arbitrary", "arbitrary"),
        name="diff_lat_attn",
    )(p, p, p, cache_k, cache_v, cos, sin, cos, sin, b_lambda_l, b_subln_l.reshape(1, HEAD_DIM))


def _log_sigmoid(x):
    return jnp.minimum(x, 0.0) - jnp.log1p(jnp.exp(-jnp.abs(x)))


def _split3(x):
    hi = x.astype(BF16)
    r = x - hi.astype(F32)
    mid = r.astype(BF16)
    lo = (r - mid.astype(F32)).astype(BF16)
    return hi, mid, lo


def _mlstm_kernel(*refs, chunk, n_chunks, zero_init):
    nh = C_HEADS
    q_refs, k_refs, v_refs, og_refs = (refs[i * nh:(i + 1) * nh] for i in range(4))
    g_ref, gb_ref, cn_ref = refs[4 * nh:4 * nh + 3]
    if not zero_init:
        c0_ref, n0_ref, m0_ref = refs[4 * nh + 3:4 * nh + 6]
    y_ref, cf_ref, nf_ref, mf_ref, hf_scr, hb_scr = refs[-6:]
    scale = HEAD_DIM ** -0.5
    ln = chunk
    n_sel = 4 * SUBLANE
    assert 4 * nh <= n_sel
    sel = jnp.where(lax.broadcasted_iota(jnp.int32, (n_sel, LANE), 0)
                    == lax.broadcasted_iota(jnp.int32, (n_sel, LANE), 1), 1.0, 0.0).astype(BF16)
    ti = lax.broadcasted_iota(jnp.int32, (ln, ln), 0)
    si = lax.broadcasted_iota(jnp.int32, (ln, ln), 1)
    lower = si <= ti
    upper = si >= ti
    lower_b = jnp.where(lower, 1.0, 0.0).astype(BF16)
    upper_b = jnp.where(upper, 1.0, 0.0).astype(BF16)

    def gate_terms(r0, directions):
        g = g_ref[pl.ds(r0, ln), :] + gb_ref[...]
        gp = _split3(g)
        rows = sum(_dot_nt(sel, p) for p in gp)
        cols = sum(_dot_nt(p, sel) for p in gp)
        lf_rows = _split3(_log_sigmoid(rows))
        lf_cols = _split3(_log_sigmoid(cols))
        t = dict(rows=rows, cols=cols)
        if 0 in directions:
            t["row0"] = sum(_dot(p, upper_b) for p in lf_rows)
            t["col0"] = sum(_dot(lower_b, p) for p in lf_cols)
        if 1 in directions:
            t["row1"] = sum(_dot(p, lower_b) for p in lf_rows)
            t["col1"] = sum(_dot(upper_b, p) for p in lf_cols)
        return t

    def head_terms(h, r0):
        qf = q_refs[h][pl.ds(r0, ln), :] * scale
        kf = k_refs[h][pl.ds(r0, ln), :]
        qb = qf.astype(BF16)
        vb = v_refs[h][pl.ds(r0, ln), :].astype(BF16)
        return dict(qf=qf, kf=kf, qb=qb, vb=vb, qk=_dot_nt(qb, kf.astype(BF16)))

    def chunk_step(gt, t, h, state, backward):
        rows, cols, qf, kf, qb, vb = gt["rows"], gt["cols"], t["qf"], t["kf"], t["qb"], t["vb"]
        gi = (2 * nh if backward else 0) + h
        gf = gi + nh
        i_row, i_col = rows[gi:gi + 1], cols[:, gi:gi + 1]
        if backward:
            b_row, b_col, mask = gt["row1"][gf:gf + 1], gt["col1"][:, gf:gf + 1], upper
            b_last = b_col[0:1]
        else:
            b_row, b_col, mask = gt["row0"][gf:gf + 1], gt["col0"][:, gf:gf + 1], lower
            b_last = b_col[ln - 1:ln]
        d = jnp.where(mask, b_col - b_row + i_row, -jnp.inf)
        gg = b_last - b_col + i_col
        if state is None:
            m_row = jnp.maximum(jnp.max(d, axis=-1, keepdims=True), b_col)
            sc = t["qk"] * jnp.exp(d - m_row)
            num = _dot(sc.astype(BF16), vb)
            den = jnp.sum(sc, axis=-1, keepdims=True)
            m_new = jnp.maximum(b_last, jnp.max(gg, axis=0, keepdims=True))
            kw = kf * jnp.exp(gg - m_new)
            c_new = _dot_tn(kw.astype(BF16), vb)
            n_new = jnp.sum(kw, axis=0, keepdims=True)
        else:
            c_st, n_st, m_st = state
            inter = b_col + m_st
            m_row = jnp.maximum(jnp.max(d, axis=-1, keepdims=True), inter)
            w_state = jnp.exp(inter - m_row)
            sc = t["qk"] * jnp.exp(d - m_row)
            num = w_state * _dot(qb, c_st.astype(BF16)) + _dot(sc.astype(BF16), vb)
            den = w_state * jnp.sum(qf * n_st, axis=-1, keepdims=True) + jnp.sum(sc, axis=-1, keepdims=True)
            m_new = jnp.maximum(b_last + m_st, jnp.max(gg, axis=0, keepdims=True))
            w_old = jnp.exp(b_last + m_st - m_new)
            kw = kf * jnp.exp(gg - m_new)
            c_new = w_old * c_st + _dot_tn(kw.astype(BF16), vb)
            n_new = w_old * n_st + jnp.sum(kw, axis=0, keepdims=True)
        h_out = num / jnp.maximum(jnp.abs(den), jnp.exp(-m_row))
        return h_out, (c_new, n_new, m_new)

    def start(c):
        return pl.multiple_of(c * ln, ln)

    def finish(h, r0, hs):
        ms = jnp.mean(hs * hs, axis=-1, keepdims=True)
        hn = hs * lax.rsqrt(ms + RMS_EPS) * cn_ref[...]
        og = og_refs[h][pl.ds(r0, ln), :]
        y_ref[pl.ds(r0, ln), h * HEAD_DIM:(h + 1) * HEAD_DIM] = (hn / (1.0 + jnp.exp(-og))).astype(y_ref.dtype)

    if zero_init:
        assert n_chunks == 1
        init = ((None,) * nh,) * 2
    else:
        init = tuple(tuple((c0_ref[dr, h], n0_ref[dr, h], m0_ref[dr, h][:, 0:1]) for h in range(nh))
                     for dr in (0, 1))
    if n_chunks == 1:
        gt = gate_terms(0, (0, 1))
        st_f, st_b = [], []
        for h in range(nh):
            t = head_terms(h, 0)
            h_f, s_f = chunk_step(gt, t, h, init[0][h], False)
            h_b, s_b = chunk_step(gt, t, h, init[1][h], True)
            finish(h, 0, h_f + h_b)
            st_f.append(s_f)
            st_b.append(s_b)
        final = (st_f, st_b)
    else:
        def body(j, states):
            r_f = start(j)
            r_b = start(n_chunks - 1 - j)
            gt_f = gate_terms(r_f, (0,))
            gt_b = gate_terms(r_b, (1,))
            st_f, st_b = [], []
            for h in range(nh):
                sl = slice(h * HEAD_DIM, (h + 1) * HEAD_DIM)
                h_f, s_f = chunk_step(gt_f, head_terms(h, r_f), h, states[0][h], False)
                h_b, s_b = chunk_step(gt_b, head_terms(h, r_b), h, states[1][h], True)
                hf_scr[pl.ds(r_f, ln), sl] = h_f
                hb_scr[pl.ds(r_b, ln), sl] = h_b
                st_f.append(s_f)
                st_b.append(s_b)
            return tuple(st_f), tuple(st_b)

        final = lax.fori_loop(0, n_chunks, body, init)

        def fin_body(j, carry):
            r0 = start(j)
            for h in range(nh):
                sl = slice(h * HEAD_DIM, (h + 1) * HEAD_DIM)
                finish(h, r0, hf_scr[pl.ds(r0, ln), sl] + hb_scr[pl.ds(r0, ln), sl])
            return carry

        lax.fori_loop(0, n_chunks, fin_body, 0)
    for dr in (0, 1):
        for h in range(nh):
            c_f, n_f, m_f = final[dr][h]
            cf_ref[dr, h] = c_f
            nf_ref[dr, h] = n_f
            mf_ref[dr, h] = jnp.broadcast_to(m_f, (1, LANE))


def _mlstm(p, batch, seq, gate_bias, c_norm_l, c0, n0, m0, lin, out_depth, lout, carried):
    chunk = min(MLSTM_CHUNK, seq)
    n_chunks = seq // chunk
    base = 3 * A_HEADS + 3 * B_HEADS
    hd = HEAD_DIM
    nh = C_HEADS
    head_bytes = 4 * nh * seq * hd * 4
    mode = dict(pipeline_mode=pl.Buffered(1)) if 2 * head_bytes > VMEM_LIMIT_BYTES // 2 else {}
    blk = lambda col: pl.BlockSpec((seq, hd), functools.partial(lambda b, c: (b, c), c=col), **mode)
    st_c = lambda l: pl.BlockSpec((None, None, 2, nh, hd, hd), lambda b: (b, l, 0, 0, 0, 0))
    st_n = lambda l: pl.BlockSpec((None, None, 2, nh, 1, hd), lambda b: (b, l, 0, 0, 0, 0))
    if c0 is None and n_chunks > 1:
        c0 = jnp.zeros((batch, 1, 2, nh, hd, hd), F32)
        n0 = jnp.zeros((batch, 1, 2, nh, hd), F32)
        m0 = jnp.zeros((batch, 1, 2, nh), F32)
    zero_init = c0 is None
    in_specs = ([blk(base + seg * nh + h) for seg in range(4) for h in range(nh)]
                + [pl.BlockSpec((seq, LANE), lambda b: (b, base + 4 * nh)),
                   pl.BlockSpec((1, LANE), lambda b: (0, 0)),
                   pl.BlockSpec((1, hd), lambda b: (0, 0))])
    args = [p] * (4 * nh + 1) + [gate_bias, c_norm_l.reshape(1, hd)]
    if not zero_init:
        in_specs += [st_c(lin), st_n(lin), st_n(lin)]
        args += [c0, n0.reshape(n0.shape[:4] + (1, hd)), jnp.broadcast_to(m0[..., None, None], m0.shape + (1, LANE))]
    aliases = {}
    if carried is not None:
        for j, buf in enumerate(carried):
            aliases[len(args)] = 1 + j
            in_specs.append(pl.BlockSpec(memory_space=pl.ANY))
            args.append(buf)
    out = pl.pallas_call(
        functools.partial(_mlstm_kernel, chunk=chunk, n_chunks=n_chunks, zero_init=zero_init),
        grid=(batch,),
        in_specs=in_specs,
        out_specs=[pl.BlockSpec((seq, nh * hd), lambda b: (b, 0)), st_c(lout), st_n(lout), st_n(lout)],
        out_shape=[jax.ShapeDtypeStruct((batch * seq, nh * hd), BF16),
                   jax.ShapeDtypeStruct((batch, out_depth, 2, nh, hd, hd), F32),
                   jax.ShapeDtypeStruct((batch, out_depth, 2, nh, 1, hd), F32),
                   jax.ShapeDtypeStruct((batch, out_depth, 2, nh, 1, LANE), F32)],
        scratch_shapes=[pltpu.VMEM((seq, nh * hd), F32), pltpu.VMEM((seq, nh * hd), F32)],
        input_output_aliases=aliases,
        compiler_params=_cparams("arbitrary"),
        name="mlstm",
    )(*args)
    return out[0], tuple(out[1:])


def _layernorm(z, g_ref, b_ref):
    mu = jnp.mean(z, axis=-1, keepdims=True)
    zc = z - mu
    var = jnp.mean(zc * zc, axis=-1, keepdims=True)
    return zc * lax.rsqrt(var + LN_EPS) * g_ref[...] + b_ref[...]


def _outproj_kernel(ya_ref, yb_ref, yc_ref, w_ref, x_ref, mod_ref, g_ref, b_ref, x1_ref, h2_ref):
    tm = x_ref.shape[0]
    rc = min(tm, OUT_ROW_CHUNK)

    def proj(c):
        rows = slice(c * rc, (c + 1) * rc)
        ycat = jnp.concatenate([ya_ref[rows, :], yb_ref[rows, :], yc_ref[rows, :]], axis=-1)
        return _dot(ycat, w_ref[...])

    def finish(c, y):
        rows = slice(c * rc, (c + 1) * rc)
        x1 = _layernorm(ALPHA * x_ref[rows, :] + mod_ref[0, 2:3, :] * y, g_ref, b_ref)
        x1_ref[rows, :] = x1
        h2_ref[rows, :] = (x1 * (1.0 + mod_ref[0, 4:5, :]) + mod_ref[0, 3:4, :]).astype(BF16)

    y = proj(0)
    for c in range(tm // rc):
        y_next = proj(c + 1) if c + 1 < tm // rc else None
        finish(c, y)
        y = y_next


def _outproj(ya, yb, yc, w_bf, layer, x, mods_l, ln_g, ln_b, row_base, rows_per_cond):
    m, d = x.shape
    tm = min(_tile(m, 512, SUBLANE), rows_per_cond)
    per = rows_per_cond // tm
    row = lambda w: pl.BlockSpec((tm, w), lambda i: (i, 0))
    vec = pl.BlockSpec((1, d), lambda i: (0, 0))
    return pl.pallas_call(
        _outproj_kernel,
        grid=(m // tm,),
        in_specs=[row(ya.shape[1]), row(yb.shape[1]), row(yc.shape[1]),
                  pl.BlockSpec((None, d, d), lambda i: (layer, 0, 0)), row(d),
                  pl.BlockSpec((1, 6, d), lambda i: (row_base + i // per, 0, 0)), vec, vec],
        out_specs=[row(d), row(d)],
        out_shape=[jax.ShapeDtypeStruct((m, d), F32), jax.ShapeDtypeStruct((m, d), BF16)],
        compiler_params=_cparams("arbitrary"),
        name="outproj_ln",
    )(ya, yb, yc, w_bf, x, mods_l, ln_g.reshape(1, d), ln_b.reshape(1, d))


def _ffn_kernel(h_ref, hp_ref, hn_ref, wa_ref, wg_ref, cwa_ref, cwg_ref, cba_ref, cbg_ref, wd_ref, x_ref, mod_ref,
                g_ref, b_ref, o_ref, hext_scr, acc_scr, ua_scr, ug_scr, *, seq):
    i = pl.program_id(0)
    f = pl.program_id(1)
    tm = h_ref.shape[0]
    halo = BF16_ROWS

    @pl.when(f == 0)
    def _():
        hext_scr[0:halo, :] = hp_ref[...]
        hext_scr[halo:halo + tm, :] = h_ref[...]
        hext_scr[halo + tm:, :] = hn_ref[...]
        acc_scr[...] = jnp.zeros_like(acc_scr)

    period = min(seq, tm)
    grp = lax.broadcasted_iota(jnp.int32, (SUBLANE, 1), 0)

    rc = min(tm, FFN_ROW_CHUNK)
    ext = tm + 2 * halo
    up_bounds = [0] + [min(ext, -(-(halo + (k + 1) * rc + 1) // BF16_ROWS) * BF16_ROWS) for k in range(tm // rc)]
    up_bounds[-1] = ext

    def up_chunk(k):
        lo, hi = up_bounds[k], up_bounds[k + 1]
        ua_scr[lo:hi, :] = _dot(hext_scr[lo:hi, :], wa_ref[...])
        ug_scr[lo:hi, :] = _dot(hext_scr[lo:hi, :], wg_ref[...])

    def mask_group(x, r0, first):
        g0 = 0 if first else rc - SUBLANE
        pos = (i * tm + r0 + g0 + grp) & (seq - 1)
        keep = (pos != 0) if first else (pos != seq - 1)
        masked = jnp.where(keep, x[g0:g0 + SUBLANE], 0.0)
        return jnp.concatenate([masked, x[SUBLANE:]] if first else [x[:g0], masked], axis=0)

    def down_chunk(k):
        r0 = k * rc

        def conv(u_scr, cw_ref, cb_ref):
            u_prev = u_scr[halo - 1 + r0:halo - 1 + r0 + rc, :]
            u_next = u_scr[halo + 1 + r0:halo + 1 + r0 + rc, :]
            if r0 % period == 0:
                u_prev = mask_group(u_prev, r0, True)
            if (r0 + rc) % period == 0:
                u_next = mask_group(u_next, r0, False)
            return (u_prev * cw_ref[0:1, :] + u_scr[halo + r0:halo + r0 + rc, :] * cw_ref[1:2, :]
                    + u_next * cw_ref[2:3, :] + cb_ref[...])

        a = conv(ua_scr, cwa_ref, cba_ref)
        g = conv(ug_scr, cwg_ref, cbg_ref)
        act = (g / (1.0 + jnp.exp2(g * NEG_LOG2_E))) * a
        acc_scr[r0:r0 + rc, :] += _dot(act.astype(BF16), wd_ref[...])

    up_chunk(0)
    for k in range(tm // rc):
        if k + 1 < tm // rc:
            up_chunk(k + 1)
        down_chunk(k)

    @pl.when(f == pl.num_programs(1) - 1)
    def _():
        z = ALPHA * x_ref[...] + mod_ref[0, 5:6, :] * acc_scr[...]
        o_ref[...] = _layernorm(z, g_ref, b_ref)


def _ffn(h2, x1, ffn_w, layer, mods_l, ln_g, ln_b, row_base, rows_per_cond, seq, tf):
    wu_bf, cw, cb, wd_bf = ffn_w
    m, d = x1.shape
    fp = wd_bf.shape[1]
    tm = min(_tile(m, 512, SUBLANE), rows_per_cond)
    per = rows_per_cond // tm
    assert seq & (seq - 1) == 0 and m % seq == 0
    nh = m // BF16_ROWS
    hb = tm // BF16_ROWS
    vec = pl.BlockSpec((1, d), lambda i, f: (0, 0))
    wcol = lambda rows, half: pl.BlockSpec((None, None, rows, tf),
                                           functools.partial(lambda i, f, hf: (layer, hf, 0, f), hf=half))
    return pl.pallas_call(
        functools.partial(_ffn_kernel, seq=seq),
        grid=(m // tm, fp // tf),
        in_specs=[pl.BlockSpec((tm, d), lambda i, f: (i, 0)),
                  pl.BlockSpec((BF16_ROWS, d), lambda i, f: (jnp.maximum(i * hb - 1, 0), 0)),
                  pl.BlockSpec((BF16_ROWS, d), lambda i, f: (jnp.minimum((i + 1) * hb, nh - 1), 0)),
                  wcol(d, 0), wcol(d, 1), wcol(CONV_W, 0), wcol(CONV_W, 1), wcol(1, 0), wcol(1, 1),
                  pl.BlockSpec((None, tf, d), lambda i, f: (layer, f, 0)),
                  pl.BlockSpec((tm, d), lambda i, f: (i, 0)),
                  pl.BlockSpec((1, 6, d), lambda i, f: (row_base + i // per, 0, 0)), vec, vec],
        out_specs=pl.BlockSpec((tm, d), lambda i, f: (i, 0)),
        out_shape=jax.ShapeDtypeStruct((m, d), F32),
        scratch_shapes=[pltpu.VMEM((tm + 2 * BF16_ROWS, d), BF16), pltpu.VMEM((tm, d), F32),
                        pltpu.VMEM((tm + 2 * BF16_ROWS, tf), F32), pltpu.VMEM((tm + 2 * BF16_ROWS, tf), F32)],
        compiler_params=_cparams("arbitrary", "arbitrary"),
        name="ffn_ln",
    )(h2, h2, h2, wu_bf, wu_bf, cw, cw, cb, cb, wd_bf, x1, mods_l, ln_g.reshape(1, d), ln_b.reshape(1, d))


def _pad_cols(w, n):
    return jnp.pad(w, [(0, 0)] * (w.ndim - 1) + [(0, n - w.shape[-1])])


def _cast_kernel(x_ref, o_ref, *, rows_valid, cols_valid):
    tr, tc = x_ref.shape
    x = x_ref[...]
    if rows_valid is not None:
        r = pl.program_id(1) * tr + lax.broadcasted_iota(jnp.int32, (tr, tc), 0)
        x = jnp.where(r < rows_valid, x, 0.0)
    if cols_valid is not None:
        c = pl.program_id(2) * tc + lax.broadcasted_iota(jnp.int32, (tr, tc), 1)
        x = jnp.where(c < cols_valid, x, 0.0)
    o_ref[:, :tc] = x.astype(BF16)
    if o_ref.shape[1] > tc:
        o_ref[:, tc:] = jnp.zeros((tr, o_ref.shape[1] - tc), BF16)


def _cast_weights(w, tr, tc, rows_out, cols_out):
    depth, r, c = w.shape
    nr, nc = -(-rows_out // tr), -(-cols_out // tc)
    assert nr * tr == rows_out and nc * tc == cols_out
    return pl.pallas_call(
        functools.partial(_cast_kernel, rows_valid=r if rows_out > r else None, cols_valid=c if cols_out > c else None),
        grid=(depth, nr, nc),
        in_specs=[pl.BlockSpec((None, tr, tc), lambda l, i, j: (l, i, j))],
        out_specs=pl.BlockSpec((None, tr, tc), lambda l, i, j: (l, i, j)),
        out_shape=jax.ShapeDtypeStruct((depth, rows_out, cols_out), BF16),
        compiler_params=_cparams("arbitrary", "arbitrary", "arbitrary"),
        name="cast_weights",
    )(w)


def _cast_in_kernel(x_ref, o_ref, *, n_valid):
    tn = x_ref.shape[0]
    col = pl.program_id(0) * tn + lax.broadcasted_iota(jnp.int32, (o_ref.shape[1], tn), 1)
    for l in range(x_ref.shape[1]):
        o_ref[l] = jnp.where(col < n_valid, x_ref[:, l, :].T, 0.0).astype(BF16)


def _cast_in_weights(w_in, n_pad, tn):
    depth, d, n = w_in.shape
    return pl.pallas_call(
        functools.partial(_cast_in_kernel, n_valid=n),
        grid=(n_pad // tn,),
        in_specs=[pl.BlockSpec((tn, depth, d), lambda j: (j, 0, 0))],
        out_specs=pl.BlockSpec((depth, d, tn), lambda j: (0, 0, j)),
        out_shape=jax.ShapeDtypeStruct((depth, d, n_pad), BF16),
        compiler_params=_cparams("arbitrary"),
        name="cast_in_weights",
    )(jnp.transpose(w_in, (2, 0, 1)))


def _cast_up_weights(w_up, dff, fp, tr):
    depth, d, _ = w_up.shape
    return pl.pallas_call(
        functools.partial(_cast_kernel, rows_valid=None, cols_valid=None),
        grid=(depth, d // tr, 2),
        in_specs=[pl.BlockSpec((None, tr, dff), lambda l, i, j: (l, i, j))],
        out_specs=pl.BlockSpec((None, None, tr, fp), lambda l, i, j: (l, j, i, 0)),
        out_shape=jax.ShapeDtypeStruct((depth, 2, d, fp), BF16),
        compiler_params=_cparams("arbitrary", "arbitrary", "arbitrary"),
        name="cast_up_weights",
    )(w_up)


def _prep_weights(w_in, w_out, w_up, conv_w, conv_b, w_down, tf):
    depth, d, n_in = w_in.shape
    n_pad = -(-n_in // LANE) * LANE
    dff = w_down.shape[1]
    fp = -(-dff // tf) * tf
    w_in_bf = _cast_in_weights(w_in, n_pad, _tile(n_pad, 384))
    w_out_bf = _cast_weights(w_out, _tile(d, 1024, SUBLANE), d, d, d)
    w_up_bf = _cast_up_weights(w_up, dff, fp, _tile(d, 512, SUBLANE))
    w_down_bf = _cast_weights(w_down, _tile(fp, 1536, tf), d, fp, d)
    halves = lambda t: jnp.stack([_pad_cols(t[..., :dff], fp), _pad_cols(t[..., dff:], fp)], axis=1)
    return w_in_bf, w_out_bf, w_up_bf, halves(conv_w), halves(conv_b[:, None, :]), w_down_bf


def _gate_bias_row(c_gate_b_l):
    return _pad_cols(c_gate_b_l.reshape(1, -1), LANE)


def kernel(x_prompt, x_sample, cache_a_k, cache_a_v, cache_b_k, cache_b_v, state_c_C, state_c_n, state_c_m, c, c_ctx, w_mod, b_mod, w_in, c_gate_b, a_rpb, b_lambda, b_subln, c_norm, w_out, ln1_g, ln1_b, ln2_g, ln2_b, w_up, conv_w, conv_b, w_down):
    batch, seq, d = x_prompt.shape
    dec_batch, dec_seq, _ = x_sample.shape
    depth = w_in.shape[0]
    tf = 512

    cond = jnp.concatenate([c_ctx[None, :], c], axis=0)
    cond = jnp.pad(cond, ((0, SUBLANE - cond.shape[0]), (0, 0)))
    mods = _mods(cond, w_mod, b_mod).reshape(depth, SUBLANE, 6, d)

    xp = x_prompt.reshape(batch * seq, d)
    xs = x_sample.reshape(dec_batch * dec_seq, d)
    cos, sin = _rope_tables(dec_seq)
    na_bias = _na_bias_table(a_rpb, dec_seq // GRID_W)

    w_in_bf, w_out_bf, *ffn_w = _prep_weights(w_in, w_out, w_up, conv_w, conv_b, w_down, tf)
    new_kv = None
    new_state = None
    for l in range(depth):
        lam_init = 0.8 - 0.6 * math.exp(-0.3 * l)
        gate_bias = _gate_bias_row(c_gate_b[l])

        pp = _inproj(xp, mods[l], w_in_bf, l, 0, batch * seq)
        ya, yb, new_kv = _ctx_attn(pp, batch, seq, b_lambda[l], b_subln[l], lam_init, l, depth, new_kv)
        yc, new_state = _mlstm(pp, batch, seq, gate_bias, c_norm[l], None, None, None, 0, depth, l, new_state)
        x1, h2 = _outproj(ya, yb, yc, w_out_bf, l, xp, mods[l], ln1_g[l], ln1_b[l], 0, batch * seq)
        xp = _ffn(h2, x1, ffn_w, l, mods[l], ln2_g[l], ln2_b[l], 0, batch * seq, seq, tf)

        ps = _inproj(xs, mods[l], w_in_bf, l, 1, dec_seq)
        ya = _na_attn(ps, dec_batch, dec_seq, cache_a_k, cache_a_v, l, na_bias)
        yb = _diff_lat_attn(ps, dec_batch, dec_seq, cache_b_k, cache_b_v, l, cos, sin, b_lambda[l], b_subln[l], lam_init)
        yc, _ = _mlstm(ps, dec_batch, dec_seq, gate_bias, c_norm[l], state_c_C, state_c_n, state_c_m, l, 1, 0, None)
        x1, h2 = _outproj(ya, yb, yc, w_out_bf, l, xs, mods[l], ln1_g[l], ln1_b[l], 1, dec_seq)
        xs = _ffn(h2, x1, ffn_w, l, mods[l], ln2_g[l], ln2_b[l], 1, dec_seq, dec_seq, tf)

    c_f, n_f, m_f = new_state
    return (xp.reshape(batch, seq, d), xs.reshape(dec_batch, dec_seq, d), *new_kv,
            c_f, n_f[..., 0, :], m_f[..., 0, 0])
```

```python
import functools
import math

import jax
import jax.numpy as jnp
import numpy as np
from jax import lax
from jax.experimental import pallas as pl
from jax.experimental.pallas import tpu as pltpu

F32 = jnp.float32
BF16 = jnp.bfloat16

DEPTH = 2
GRID_W = 64
A_HEADS = 6
NA_ROWS = 8
NA_COLS = 16
NA_QROWS = 4
B_HEADS = 5
B_QK = 64
C_HEADS = 5
HEAD_DIM = 128
CONV_W = 3
ROPE_BASE = 10000.0
LN_EPS = 1e-5
RMS_EPS = 1e-6
ALPHA = (2 * DEPTH) ** 0.25
NEG_LOG2_E = -math.log2(math.e)

LANE = 128
SUBLANE = 8
BF16_ROWS = 16
VMEM_LIMIT_BYTES = 56 * 1024 * 1024

MLSTM_CHUNK = 256
FFN_ROW_CHUNK = 256
OUT_ROW_CHUNK = 128
HI = lax.Precision.HIGHEST


def _tile(n, target, unit=LANE):
    if n <= target:
        return n
    best = unit
    for t in range(unit, target + 1, unit):
        if n % t == 0:
            best = t
    assert n % best == 0, (n, target, unit)
    return best


def _cparams(*sem):
    return pltpu.CompilerParams(dimension_semantics=sem, vmem_limit_bytes=VMEM_LIMIT_BYTES)


def _dot(a, b):
    return jnp.dot(a, b, preferred_element_type=F32)


def _dot_nt(a, b, precision=None):
    return lax.dot_general(a, b, (((1,), (1,)), ((), ())), preferred_element_type=F32, precision=precision)


def _dot_tn(a, b):
    return lax.dot_general(a, b, (((0,), (0,)), ((), ())), preferred_element_type=F32)


def _mods_kernel(cond_ref, w_ref, b_ref, o_ref):
    c = cond_ref[...]
    s = c / (1.0 + jnp.exp(-c))
    o_ref[0] = _dot(s.astype(BF16), w_ref[0].astype(BF16)) + b_ref[0]


def _mods(cond, w_mod, b_mod):
    depth, d, n = w_mod.shape
    tn = _tile(n, 1536)
    return pl.pallas_call(
        _mods_kernel,
        grid=(depth, n // tn),
        in_specs=[pl.BlockSpec((SUBLANE, d), lambda l, j: (0, 0)),
                  pl.BlockSpec((1, d, tn), lambda l, j: (l, 0, j)),
                  pl.BlockSpec((1, 1, tn), lambda l, j: (l, 0, j))],
        out_specs=pl.BlockSpec((1, SUBLANE, tn), lambda l, j: (l, 0, j)),
        out_shape=jax.ShapeDtypeStruct((depth, SUBLANE, n), F32),
        compiler_params=_cparams("arbitrary", "arbitrary"),
        name="mods",
    )(cond, w_mod, b_mod.reshape(depth, 1, n))


def _inproj_kernel(x_ref, mod_ref, w_ref, o_ref, h_scr):
    @pl.when(pl.program_id(1) == 0)
    def _():
        sh = mod_ref[0, 0:1, :]
        sc = mod_ref[0, 1:2, :]
        h_scr[...] = (x_ref[...] * (1.0 + sc) + sh).astype(BF16)

    o_ref[...] = _dot(h_scr[...], w_ref[...])


def _inproj(x, mods_l, w_bf, layer, row_base, rows_per_cond):
    m, d = x.shape
    n = w_bf.shape[2]
    tm = _tile(m, 1024, SUBLANE)
    tm = min(tm, rows_per_cond)
    tn = _tile(n, 1152)
    per = rows_per_cond // tm
    return pl.pallas_call(
        _inproj_kernel,
        grid=(m // tm, n // tn),
        in_specs=[pl.BlockSpec((tm, d), lambda i, j: (i, 0)),
                  pl.BlockSpec((1, 6, d), lambda i, j: (row_base + i // per, 0, 0)),
                  pl.BlockSpec((None, d, tn), lambda i, j: (layer, 0, j))],
        out_specs=pl.BlockSpec((tm, tn), lambda i, j: (i, j)),
        out_shape=jax.ShapeDtypeStruct((m, n), F32),
        scratch_shapes=[pltpu.VMEM((tm, d), BF16)],
        compiler_params=_cparams("arbitrary", "arbitrary"),
        name="inproj",
    )(x, mods_l, w_bf)


def _lambda_value(lam_ref, lam_init):
    lf = lam_ref[...]
    t1 = jnp.sum(lf[0:1] * lf[1:2], axis=1, keepdims=True)
    t2 = jnp.sum(lf[2:3] * lf[3:4], axis=1, keepdims=True)
    return jnp.exp(t1) - jnp.exp(t2) + lam_init


def _split_maps(q):
    lane = lax.broadcasted_iota(jnp.int32, q.shape, 1)
    first = lane < B_QK
    return jnp.where(first, q, 0.0), jnp.where(first, 0.0, q)


def _subnorm(o, g_ref, lam_init):
    ms = jnp.mean(o * o, axis=-1, keepdims=True)
    return o * lax.rsqrt(ms + RMS_EPS) * g_ref[...] * (1.0 - lam_init)


def _ctx_attn_kernel(*refs, lam_init):
    qa_ref, ka_ref, va_ref = refs[0:3]
    b_refs = refs[3:3 + 3 * B_HEADS]
    lam_ref, subln_ref = refs[3 + 3 * B_HEADS:5 + 3 * B_HEADS]
    ya_ref, yb_ref, nak_ref, nav_ref, nbk_ref, nbv_ref = refs[-6:]
    a_scale = HEAD_DIM ** -0.5
    b_scale = B_QK ** -0.5
    lam = _lambda_value(lam_ref, lam_init)
    head = lambda h: slice(h * HEAD_DIM, (h + 1) * HEAD_DIM)

    def a_scores(h):
        nak_ref[h] = ka_ref[:, head(h)]
        nav_ref[h] = va_ref[:, head(h)]
        return (_dot_nt(qa_ref[:, head(h)].astype(BF16), ka_ref[:, head(h)].astype(BF16)),)

    def a_finish(h, s):
        s = s * a_scale
        e = jnp.exp(s - jnp.max(s, axis=-1, keepdims=True))
        l = jnp.sum(e, axis=-1, keepdims=True)
        ya_ref[:, head(h)] = (_dot(e.astype(BF16), va_ref[:, head(h)].astype(BF16)) / l).astype(ya_ref.dtype)

    def b_scores(h):
        q1, q2 = _split_maps(b_refs[3 * h][...] * b_scale)
        nbk_ref[h] = b_refs[3 * h + 1][...]
        nbv_ref[h] = b_refs[3 * h + 2][...]
        k = b_refs[3 * h + 1][...].astype(BF16)
        return _dot_nt(q1.astype(BF16), k), _dot_nt(q2.astype(BF16), k)

    def b_finish(h, s1, s2):
        e1 = jnp.exp(s1 - jnp.max(s1, axis=-1, keepdims=True))
        e2 = jnp.exp(s2 - jnp.max(s2, axis=-1, keepdims=True))
        p = e1 / jnp.sum(e1, axis=-1, keepdims=True) - lam * (e2 / jnp.sum(e2, axis=-1, keepdims=True))
        o = _dot(p.astype(BF16), b_refs[3 * h + 2][...].astype(BF16))
        yb_ref[:, head(h)] = _subnorm(o, subln_ref, lam_init).astype(yb_ref.dtype)

    stages = [(a_scores, a_finish, h) for h in range(A_HEADS)] + [(b_scores, b_finish, h) for h in range(B_HEADS)]
    pending = stages[0][0](stages[0][2])
    for idx, (_, finish, h) in enumerate(stages):
        nxt = stages[idx + 1][0](stages[idx + 1][2]) if idx + 1 < len(stages) else None
        finish(h, *pending)
        pending = nxt


def _ctx_attn(p, batch, seq, b_lambda_l, b_subln_l, lam_init, layer, depth, carried):
    m = p.shape[0]
    aw = A_HEADS * HEAD_DIM
    nb = aw // HEAD_DIM
    in_specs = [pl.BlockSpec((seq, aw), lambda b: (b, 0)),
                pl.BlockSpec((seq, aw), lambda b: (b, 1)),
                pl.BlockSpec((seq, aw), lambda b: (b, 2))]
    args = [p, p, p]
    for h in range(B_HEADS):
        for seg in range(3):
            col = 3 * nb + seg * B_HEADS + h
            in_specs.append(pl.BlockSpec((seq, HEAD_DIM), functools.partial(lambda b, c: (b, c), c=col)))
            args.append(p)
    in_specs += [pl.BlockSpec((4, B_QK), lambda b: (0, 0)), pl.BlockSpec((1, HEAD_DIM), lambda b: (0, 0))]
    args += [b_lambda_l, b_subln_l.reshape(1, HEAD_DIM)]
    aliases = {}
    if carried is not None:
        for j, buf in enumerate(carried):
            aliases[len(args)] = 2 + j
            in_specs.append(pl.BlockSpec(memory_space=pl.ANY))
            args.append(buf)
    kv_spec = lambda nh: pl.BlockSpec((None, None, nh, seq, HEAD_DIM), lambda b: (b, layer, 0, 0, 0))
    kv_shape = lambda nh: jax.ShapeDtypeStruct((batch, depth, nh, seq, HEAD_DIM), F32)
    out = pl.pallas_call(
        functools.partial(_ctx_attn_kernel, lam_init=lam_init),
        grid=(batch,),
        in_specs=in_specs,
        out_specs=[pl.BlockSpec((seq, aw), lambda b: (b, 0)),
                   pl.BlockSpec((seq, B_HEADS * HEAD_DIM), lambda b: (b, 0)),
                   kv_spec(A_HEADS), kv_spec(A_HEADS), kv_spec(B_HEADS), kv_spec(B_HEADS)],
        out_shape=[jax.ShapeDtypeStruct((m, aw), BF16),
                   jax.ShapeDtypeStruct((m, B_HEADS * HEAD_DIM), BF16),
                   kv_shape(A_HEADS), kv_shape(A_HEADS), kv_shape(B_HEADS), kv_shape(B_HEADS)],
        input_output_aliases=aliases,
        compiler_params=_cparams("arbitrary"),
        name="ctx_attn",
    )(*args)
    return out[0], out[1], tuple(out[2:])


def _na_plan(n_rows):
    wr = min(NA_ROWS, n_rows)
    qrows = NA_QROWS
    uw = -(-(wr + qrows - 1) // 2) * 2
    if n_rows % qrows or n_rows < uw:
        qrows, uw = 1, wr
    rs = lambda r: int(np.clip(r - wr // 2, 0, n_rows - wr))
    starts, var_of_block, variants = [], [], []
    for blk in range(n_rows // qrows):
        q0 = blk * qrows
        ws = int(np.clip(rs(q0), 0, n_rows - uw))
        valid = tuple(tuple(rs(q0 + i) <= ws + j < rs(q0 + i) + wr for j in range(uw)) for i in range(qrows))
        var = (q0 - ws, valid)
        if var not in variants:
            variants.append(var)
        starts.append(ws)
        var_of_block.append(variants.index(var))
    return qrows, uw, starts, var_of_block, variants


def _na_bias_kernel(toe_ref, o_ref, *, n_rows):
    qrows, uw, _, _, variants = _na_plan(n_rows)
    w = GRID_W
    neg = jnp.full((w, w), -jnp.inf, F32)
    for v, (off, valid) in enumerate(variants):
        for i in range(qrows):
            for j in range(uw):
                dr = NA_ROWS - 1 + j - off - i
                tile = toe_ref[dr] if valid[i][j] else neg
                o_ref[v, i * w:(i + 1) * w, j * w:(j + 1) * w] = tile


def _na_bias_table(a_rpb, n_rows):
    depth, heads, nr, _ = a_rpb.shape
    qrows, uw, _, _, variants = _na_plan(n_rows)
    cols = np.arange(GRID_W)
    cs = np.clip(cols - NA_COLS // 2, 0, GRID_W - NA_COLS)
    col_mask = (cols[None, :] >= cs[:, None]) & (cols[None, :] < cs[:, None] + NA_COLS)
    col_idx = np.clip(cols[None, :] - cols[:, None] + NA_COLS - 1, 0, 2 * NA_COLS - 2)
    onehot = (col_idx[None] == np.arange(2 * NA_COLS - 1)[:, None, None]).astype(np.float32)
    toe = jnp.einsum('lhrd,dqk->lhrqk', a_rpb.astype(F32), jnp.asarray(onehot), precision=HI)
    toe = jnp.where(jnp.asarray(col_mask), toe, -jnp.inf)
    shape = (len(variants), qrows * GRID_W, uw * GRID_W)
    return pl.pallas_call(
        functools.partial(_na_bias_kernel, n_rows=n_rows),
        grid=(depth, heads),
        in_specs=[pl.BlockSpec((None, None, nr, GRID_W, GRID_W), lambda l, h: (l, h, 0, 0, 0))],
        out_specs=pl.BlockSpec((None, None) + shape, lambda l, h: (l, h, 0, 0, 0)),
        out_shape=jax.ShapeDtypeStruct((depth, heads) + shape, F32),
        compiler_params=_cparams("arbitrary", "arbitrary"),
        name="na_bias",
    )(toe)


def _select_by_block(blk, values):
    out = jnp.int32(values[0])
    for k in range(1, len(values)):
        out = jnp.where(blk == k, jnp.int32(values[k]), out)
    return out


def _na_kernel(q_ref, k_ref, v_ref, kc_ref, vc_ref, bias_ref, o_ref, kb_scr, vb_scr, kcb_scr, vcb_scr, *, n_rows):
    scale = HEAD_DIM ** -0.5
    qrows, uw, starts, var_of_block, _ = _na_plan(n_rows)
    nq = qrows * GRID_W
    kb_scr[...] = k_ref[...].astype(BF16)
    vb_scr[...] = v_ref[...].astype(BF16)
    kcb_scr[...] = kc_ref[...].astype(BF16)
    vcb_scr[...] = vc_ref[...].astype(BF16)

    def scores(blk):
        q0 = pl.multiple_of(blk * nq, nq)
        q = q_ref[pl.ds(q0, nq), :].astype(BF16)
        w0 = pl.multiple_of(_select_by_block(blk, starts) * GRID_W, GRID_W)
        s_nb = _dot_nt(q, kb_scr[pl.ds(w0, uw * GRID_W), :])
        s_c = _dot_nt(q, kcb_scr[...])
        return q0, w0, s_nb, s_c

    def finish(blk, q0, w0, s_nb, s_c):
        s_nb = s_nb * scale + bias_ref[_select_by_block(blk, var_of_block)]
        s_c = s_c * scale
        mx = jnp.maximum(jnp.max(s_nb, axis=-1, keepdims=True), jnp.max(s_c, axis=-1, keepdims=True))
        p_nb = jnp.exp(s_nb - mx)
        p_c = jnp.exp(s_c - mx)
        l = jnp.sum(p_nb, axis=-1, keepdims=True) + jnp.sum(p_c, axis=-1, keepdims=True)
        vw = vb_scr[pl.ds(w0, uw * GRID_W), :]
        o = (_dot(p_nb.astype(BF16), vw) + _dot(p_c.astype(BF16), vcb_scr[...])) / l
        o_ref[pl.ds(q0, nq), :] = o.astype(o_ref.dtype)

    n_blocks = n_rows // qrows
    group = 2 if n_blocks % 2 == 0 else 1

    def body(j, carry):
        blks = [j * group + g for g in range(group)]
        pending = [scores(b) for b in blks]
        for b, s in zip(blks, pending):
            finish(b, *s)
        return carry

    lax.fori_loop(0, n_blocks // group, body, 0)


def _na_attn(p, batch, seq, cache_k, cache_v, layer, bias):
    n_rows = seq // GRID_W
    past = cache_k.shape[3]
    blk = lambda off: pl.BlockSpec((seq, HEAD_DIM), lambda b, h: (b, off + h))
    cspec = pl.BlockSpec((None, None, None, past, HEAD_DIM), lambda b, h: (b, layer, h, 0, 0))
    return pl.pallas_call(
        functools.partial(_na_kernel, n_rows=n_rows),
        grid=(batch, A_HEADS),
        in_specs=[blk(0), blk(A_HEADS), blk(2 * A_HEADS), cspec, cspec,
                  pl.BlockSpec((None, None) + bias.shape[2:], lambda b, h: (layer, h, 0, 0, 0))],
        out_specs=pl.BlockSpec((seq, HEAD_DIM), lambda b, h: (b, h)),
        out_shape=jax.ShapeDtypeStruct((batch * seq, A_HEADS * HEAD_DIM), BF16),
        scratch_shapes=[pltpu.VMEM((seq, HEAD_DIM), BF16), pltpu.VMEM((seq, HEAD_DIM), BF16),
                        pltpu.VMEM((past, HEAD_DIM), BF16), pltpu.VMEM((past, HEAD_DIM), BF16)],
        compiler_params=_cparams("arbitrary", "arbitrary"),
        name="na_attn",
    )(p, p, p, cache_k, cache_v, bias)


def _rope_tables(seq):
    t = np.arange(seq)
    rows = (t // GRID_W).astype(np.float32)
    cols = (t % GRID_W).astype(np.float32)
    half = B_QK // 2
    freqs = jnp.asarray(ROPE_BASE, F32) ** (-jnp.arange(0, half, 2, dtype=F32) / half)
    ar = jnp.asarray(rows)[:, None] * freqs
    ac = jnp.asarray(cols)[:, None] * freqs
    cr, sr, cc, sc = jnp.cos(ar), jnp.sin(ar), jnp.cos(ac), jnp.sin(ac)
    cos = jnp.concatenate([cr, cr, cc, cc] * 2, axis=-1)
    sin = jnp.concatenate([-sr, sr, -sc, sc] * 2, axis=-1)
    return cos, sin


def _rope(x, cos, sin):
    q = B_QK // 4
    lane = lax.broadcasted_iota(jnp.int32, x.shape, 1)
    first = (lane & (2 * q - 1)) < q
    partner = jnp.where(first, pltpu.roll(x, LANE - q, 1), pltpu.roll(x, q, 1))
    return x * cos + partner * sin


def _diff_lat_kernel(q_ref, k_ref, v_ref, kc_ref, vc_ref, cosq_ref, sinq_ref, cosk_ref, sink_ref, lam_ref, subln_ref,
                     o_ref, kr_scr, kcb_scr, vb_scr, vcb_scr, *, lam_init):
    scale = B_QK ** -0.5

    @pl.when(pl.program_id(2) == 0)
    def _():
        kr_scr[...] = _rope(k_ref[...], cosk_ref[...], sink_ref[...]).astype(BF16)
        kcb_scr[...] = kc_ref[...].astype(BF16)
        vb_scr[...] = v_ref[...].astype(BF16)
        vcb_scr[...] = vc_ref[...].astype(BF16)

    lam = _lambda_value(lam_ref, lam_init)
    q = q_ref[...] * scale
    qr1, qr2 = _split_maps(_rope(q, cosq_ref[...], sinq_ref[...]))
    q1, q2 = _split_maps(q)

    def probs(qr, qp):
        s_l = _dot_nt(qr.astype(BF16), kr_scr[...])
        s_c = _dot_nt(qp.astype(BF16), kcb_scr[...])
        mx = jnp.maximum(jnp.max(s_l, axis=-1, keepdims=True), jnp.max(s_c, axis=-1, keepdims=True))
        e_l = jnp.exp(s_l - mx)
        e_c = jnp.exp(s_c - mx)
        inv = 1.0 / (jnp.sum(e_l, axis=-1, keepdims=True) + jnp.sum(e_c, axis=-1, keepdims=True))
        return e_l * inv, e_c * inv

    p1_l, p1_c = probs(qr1, q1)
    p2_l, p2_c = probs(qr2, q2)
    o = (_dot((p1_l - lam * p2_l).astype(BF16), vb_scr[...])
         + _dot((p1_c - lam * p2_c).astype(BF16), vcb_scr[...]))
    o_ref[...] = _subnorm(o, subln_ref, lam_init).astype(o_ref.dtype)


def _diff_lat_attn(p, batch, seq, cache_k, cache_v, layer, cos, sin, b_lambda_l, b_subln_l, lam_init):
    past = cache_k.shape[3]
    tq = _tile(seq, 256, SUBLANE)
    nq = seq // tq
    base = 3 * A_HEADS
    cspec = pl.BlockSpec((None, None, None, past, HEAD_DIM), lambda b, h, i: (b, layer, h, 0, 0))
    kv = lambda off: pl.BlockSpec((seq, HEAD_DIM), lambda b, h, i: (b, off + h))
    return pl.pallas_call(
        functools.partial(_diff_lat_kernel, lam_init=lam_init),
        grid=(batch, B_HEADS, nq),
        in_specs=[pl.BlockSpec((tq, HEAD_DIM), lambda b, h, i: (b * nq + i, base + h)),
                  kv(base + B_HEADS), kv(base + 2 * B_HEADS), cspec, cspec,
                  pl.BlockSpec((tq, HEAD_DIM), lambda b, h, i: (i, 0)),
                  pl.BlockSpec((tq, HEAD_DIM), lambda b, h, i: (i, 0)),
                  pl.BlockSpec((seq, HEAD_DIM), lambda b, h, i: (0, 0)),
                  pl.BlockSpec((seq, HEAD_DIM), lambda b, h, i: (0, 0)),
                  pl.BlockSpec((4, B_QK), lambda b, h, i: (0, 0)),
                  pl.BlockSpec((1, HEAD_DIM), lambda b, h, i: (0, 0))],
        out_specs=pl.BlockSpec((tq, HEAD_DIM), lambda b, h, i: (b * nq + i, h)),
        out_shape=jax.ShapeDtypeStruct((batch * seq, B_HEADS * HEAD_DIM), BF16),
        scratch_shapes=[pltpu.VMEM((seq, HEAD_DIM), BF16), pltpu.VMEM((past, HEAD_DIM), BF16),
                        pltpu.VMEM((seq, HEAD_DIM), BF16), pltpu.VMEM((past, HEAD_DIM), BF16)],
        compiler_params=_cparams("arbitrary", "arbitrary", "arbitrary"),
        name="diff_lat_attn",
    )(p, p, p, cache_k, cache_v, cos, sin, cos, sin, b_lambda_l, b_subln_l.reshape(1, HEAD_DIM))


def _log_sigmoid(x):
    return jnp.minimum(x, 0.0) - jnp.log1p(jnp.exp(-jnp.abs(x)))


def _split3(x):
    hi = x.astype(BF16)
    r = x - hi.astype(F32)
    mid = r.astype(BF16)
    lo = (r - mid.astype(F32)).astype(BF16)
    return hi, mid, lo


def _mlstm_kernel(*refs, chunk, n_chunks, zero_init):
    nh = C_HEADS
    q_refs, k_refs, v_refs, og_refs = (refs[i * nh:(i + 1) * nh] for i in range(4))
    g_ref, gb_ref, cn_ref = refs[4 * nh:4 * nh + 3]
    if not zero_init:
        c0_ref, n0_ref, m0_ref = refs[4 * nh + 3:4 * nh + 6]
    y_ref, cf_ref, nf_ref, mf_ref, hf_scr, hb_scr = refs[-6:]
    scale = HEAD_DIM ** -0.5
    ln = chunk
    n_sel = 4 * SUBLANE
    assert 4 * nh <= n_sel
    sel = jnp.where(lax.broadcasted_iota(jnp.int32, (n_sel, LANE), 0)
                    == lax.broadcasted_iota(jnp.int32, (n_sel, LANE), 1), 1.0, 0.0).astype(BF16)
    ti = lax.broadcasted_iota(jnp.int32, (ln, ln), 0)
    si = lax.broadcasted_iota(jnp.int32, (ln, ln), 1)
    lower = si <= ti
    upper = si >= ti
    lower_b = jnp.where(lower, 1.0, 0.0).astype(BF16)
    upper_b = jnp.where(upper, 1.0, 0.0).astype(BF16)

    def gate_terms(r0, directions):
        g = g_ref[pl.ds(r0, ln), :] + gb_ref[...]
        gp = _split3(g)
        rows = sum(_dot_nt(sel, p) for p in gp)
        cols = sum(_dot_nt(p, sel) for p in gp)
        lf_rows = _split3(_log_sigmoid(rows))
        lf_cols = _split3(_log_sigmoid(cols))
        t = dict(rows=rows, cols=cols)
        if 0 in directions:
            t["row0"] = sum(_dot(p, upper_b) for p in lf_rows)
            t["col0"] = sum(_dot(lower_b, p) for p in lf_cols)
        if 1 in directions:
            t["row1"] = sum(_dot(p, lower_b) for p in lf_rows)
            t["col1"] = sum(_dot(upper_b, p) for p in lf_cols)
        return t

    def head_terms(h, r0):
        qf = q_refs[h][pl.ds(r0, ln), :] * scale
        kf = k_refs[h][pl.ds(r0, ln), :]
        qb = qf.astype(BF16)
        vb = v_refs[h][pl.ds(r0, ln), :].astype(BF16)
        return dict(qf=qf, kf=kf, qb=qb, vb=vb, qk=_dot_nt(qb, kf.astype(BF16)))

    def chunk_step(gt, t, h, state, backward):
        rows, cols, qf, kf, qb, vb = gt["rows"], gt["cols"], t["qf"], t["kf"], t["qb"], t["vb"]
        gi = (2 * nh if backward else 0) + h
        gf = gi + nh
        i_row, i_col = rows[gi:gi + 1], cols[:, gi:gi + 1]
        if backward:
            b_row, b_col, mask = gt["row1"][gf:gf + 1], gt["col1"][:, gf:gf + 1], upper
            b_last = b_col[0:1]
        else:
            b_row, b_col, mask = gt["row0"][gf:gf + 1], gt["col0"][:, gf:gf + 1], lower
            b_last = b_col[ln - 1:ln]
        d = jnp.where(mask, b_col - b_row + i_row, -jnp.inf)
        gg = b_last - b_col + i_col
        if state is None:
            m_row = jnp.maximum(jnp.max(d, axis=-1, keepdims=True), b_col)
            sc = t["qk"] * jnp.exp(d - m_row)
            num = _dot(sc.astype(BF16), vb)
            den = jnp.sum(sc, axis=-1, keepdims=True)
            m_new = jnp.maximum(b_last, jnp.max(gg, axis=0, keepdims=True))
            kw = kf * jnp.exp(gg - m_new)
            c_new = _dot_tn(kw.astype(BF16), vb)
            n_new = jnp.sum(kw, axis=0, keepdims=True)
        else:
            c_st, n_st, m_st = state
            inter = b_col + m_st
            m_row = jnp.maximum(jnp.max(d, axis=-1, keepdims=True), inter)
            w_state = jnp.exp(inter - m_row)
            sc = t["qk"] * jnp.exp(d - m_row)
            num = w_state * _dot(qb, c_st.astype(BF16)) + _dot(sc.astype(BF16), vb)
            den = w_state * jnp.sum(qf * n_st, axis=-1, keepdims=True) + jnp.sum(sc, axis=-1, keepdims=True)
            m_new = jnp.maximum(b_last + m_st, jnp.max(gg, axis=0, keepdims=True))
            w_old = jnp.exp(b_last + m_st - m_new)
            kw = kf * jnp.exp(gg - m_new)
            c_new = w_old * c_st + _dot_tn(kw.astype(BF16), vb)
            n_new = w_old * n_st + jnp.sum(kw, axis=0, keepdims=True)
        h_out = num / jnp.maximum(jnp.abs(den), jnp.exp(-m_row))
        return h_out, (c_new, n_new, m_new)

    def start(c):
        return pl.multiple_of(c * ln, ln)

    def finish(h, r0, hs):
        ms = jnp.mean(hs * hs, axis=-1, keepdims=True)
        hn = hs * lax.rsqrt(ms + RMS_EPS) * cn_ref[...]
        og = og_refs[h][pl.ds(r0, ln), :]
        y_ref[pl.ds(r0, ln), h * HEAD_DIM:(h + 1) * HEAD_DIM] = (hn / (1.0 + jnp.exp(-og))).astype(y_ref.dtype)

    if zero_init:
        assert n_chunks == 1
        init = ((None,) * nh,) * 2
    else:
        init = tuple(tuple((c0_ref[dr, h], n0_ref[dr, h], m0_ref[dr, h][:, 0:1]) for h in range(nh))
                     for dr in (0, 1))
    if n_chunks == 1:
        gt = gate_terms(0, (0, 1))
        st_f, st_b = [], []
        for h in range(nh):
            t = head_terms(h, 0)
            h_f, s_f = chunk_step(gt, t, h, init[0][h], False)
            h_b, s_b = chunk_step(gt, t, h, init[1][h], True)
            finish(h, 0, h_f + h_b)
            st_f.append(s_f)
            st_b.append(s_b)
        final = (st_f, st_b)
    else:
        def body(j, states):
            r_f = start(j)
            r_b = start(n_chunks - 1 - j)
            gt_f = gate_terms(r_f, (0,))
            gt_b = gate_terms(r_b, (1,))
            terms_f = [head_terms(h, r_f) for h in range(nh)]
            terms_b = [head_terms(h, r_b) for h in range(nh)]
            st_f, st_b = [], []
            for h in range(nh):
                sl = slice(h * HEAD_DIM, (h + 1) * HEAD_DIM)
                h_f, s_f = chunk_step(gt_f, terms_f[h], h, states[0][h], False)
                h_b, s_b = chunk_step(gt_b, terms_b[h], h, states[1][h], True)
                hf_scr[pl.ds(r_f, ln), sl] = h_f
                hb_scr[pl.ds(r_b, ln), sl] = h_b
                st_f.append(s_f)
                st_b.append(s_b)
            return tuple(st_f), tuple(st_b)

        final = lax.fori_loop(0, n_chunks, body, init)

        def fin_body(j, carry):
            r0 = start(j)
            for h in range(nh):
                sl = slice(h * HEAD_DIM, (h + 1) * HEAD_DIM)
                finish(h, r0, hf_scr[pl.ds(r0, ln), sl] + hb_scr[pl.ds(r0, ln), sl])
            return carry

        lax.fori_loop(0, n_chunks, fin_body, 0)
    for dr in (0, 1):
        for h in range(nh):
            c_f, n_f, m_f = final[dr][h]
            cf_ref[dr, h] = c_f
            nf_ref[dr, h] = n_f
            mf_ref[dr, h] = jnp.broadcast_to(m_f, (1, LANE))


def _mlstm(p, batch, seq, gate_bias, c_norm_l, c0, n0, m0, lin, out_depth, lout, carried):
    chunk = min(MLSTM_CHUNK, seq)
    n_chunks = seq // chunk
    base = 3 * A_HEADS + 3 * B_HEADS
    hd = HEAD_DIM
    nh = C_HEADS
    head_bytes = 4 * nh * seq * hd * 4
    mode = dict(pipeline_mode=pl.Buffered(1)) if 2 * head_bytes > VMEM_LIMIT_BYTES // 2 else {}
    blk = lambda col: pl.BlockSpec((seq, hd), functools.partial(lambda b, c: (b, c), c=col), **mode)
    st_c = lambda l: pl.BlockSpec((None, None, 2, nh, hd, hd), lambda b: (b, l, 0, 0, 0, 0))
    st_n = lambda l: pl.BlockSpec((None, None, 2, nh, 1, hd), lambda b: (b, l, 0, 0, 0, 0))
    if c0 is None and n_chunks > 1:
        c0 = jnp.zeros((batch, 1, 2, nh, hd, hd), F32)
        n0 = jnp.zeros((batch, 1, 2, nh, hd), F32)
        m0 = jnp.zeros((batch, 1, 2, nh), F32)
    zero_init = c0 is None
    in_specs = ([blk(base + seg * nh + h) for seg in range(4) for h in range(nh)]
                + [pl.BlockSpec((seq, LANE), lambda b: (b, base + 4 * nh)),
                   pl.BlockSpec((1, LANE), lambda b: (0, 0)),
                   pl.BlockSpec((1, hd), lambda b: (0, 0))])
    args = [p] * (4 * nh + 1) + [gate_bias, c_norm_l.reshape(1, hd)]
    if not zero_init:
        in_specs += [st_c(lin), st_n(lin), st_n(lin)]
        args += [c0, n0.reshape(n0.shape[:4] + (1, hd)), jnp.broadcast_to(m0[..., None, None], m0.shape + (1, LANE))]
    aliases = {}
    if carried is not None:
        for j, buf in enumerate(carried):
            aliases[len(args)] = 1 + j
            in_specs.append(pl.BlockSpec(memory_space=pl.ANY))
            args.append(buf)
    out = pl.pallas_call(
        functools.partial(_mlstm_kernel, chunk=chunk, n_chunks=n_chunks, zero_init=zero_init),
        grid=(batch,),
        in_specs=in_specs,
        out_specs=[pl.BlockSpec((seq, nh * hd), lambda b: (b, 0)), st_c(lout), st_n(lout), st_n(lout)],
        out_shape=[jax.ShapeDtypeStruct((batch * seq, nh * hd), BF16),
                   jax.ShapeDtypeStruct((batch, out_depth, 2, nh, hd, hd), F32),
                   jax.ShapeDtypeStruct((batch, out_depth, 2, nh, 1, hd), F32),
                   jax.ShapeDtypeStruct((batch, out_depth, 2, nh, 1, LANE), F32)],
        scratch_shapes=[pltpu.VMEM((seq, nh * hd), F32), pltpu.VMEM((seq, nh * hd), F32)],
        input_output_aliases=aliases,
        compiler_params=_cparams("arbitrary"),
        name="mlstm",
    )(*args)
    return out[0], tuple(out[1:])


def _layernorm(z, g_ref, b_ref):
    mu = jnp.mean(z, axis=-1, keepdims=True)
    zc = z - mu
    var = jnp.mean(zc * zc, axis=-1, keepdims=True)
    return zc * lax.rsqrt(var + LN_EPS) * g_ref[...] + b_ref[...]


def _outproj_kernel(ya_ref, yb_ref, yc_ref, w_ref, x_ref, mod_ref, g_ref, b_ref, x1_ref, h2_ref):
    tm = x_ref.shape[0]
    rc = min(tm, OUT_ROW_CHUNK)

    def proj(c):
        rows = slice(c * rc, (c + 1) * rc)
        ycat = jnp.concatenate([ya_ref[rows, :], yb_ref[rows, :], yc_ref[rows, :]], axis=-1)
        return _dot(ycat, w_ref[...])

    def finish(c, y):
        rows = slice(c * rc, (c + 1) * rc)
        x1 = _layernorm(ALPHA * x_ref[rows, :] + mod_ref[0, 2:3, :] * y, g_ref, b_ref)
        x1_ref[rows, :] = x1
        h2_ref[rows, :] = (x1 * (1.0 + mod_ref[0, 4:5, :]) + mod_ref[0, 3:4, :]).astype(BF16)

    y = proj(0)
    for c in range(tm // rc):
        y_next = proj(c + 1) if c + 1 < tm // rc else None
        finish(c, y)
        y = y_next


def _outproj(ya, yb, yc, w_bf, layer, x, mods_l, ln_g, ln_b, row_base, rows_per_cond):
    m, d = x.shape
    tm = min(_tile(m, 512, SUBLANE), rows_per_cond)
    per = rows_per_cond // tm
    row = lambda w: pl.BlockSpec((tm, w), lambda i: (i, 0))
    vec = pl.BlockSpec((1, d), lambda i: (0, 0))
    return pl.pallas_call(
        _outproj_kernel,
        grid=(m // tm,),
        in_specs=[row(ya.shape[1]), row(yb.shape[1]), row(yc.shape[1]),
                  pl.BlockSpec((None, d, d), lambda i: (layer, 0, 0)), row(d),
                  pl.BlockSpec((1, 6, d), lambda i: (row_base + i // per, 0, 0)), vec, vec],
        out_specs=[row(d), row(d)],
        out_shape=[jax.ShapeDtypeStruct((m, d), F32), jax.ShapeDtypeStruct((m, d), BF16)],
        compiler_params=_cparams("arbitrary"),
        name="outproj_ln",
    )(ya, yb, yc, w_bf, x, mods_l, ln_g.reshape(1, d), ln_b.reshape(1, d))


def _ffn_kernel(h_ref, hp_ref, hn_ref, wa_ref, wg_ref, cwa_ref, cwg_ref, cba_ref, cbg_ref, wd_ref, x_ref, mod_ref,
                g_ref, b_ref, o_ref, hext_scr, acc_scr, ua_scr, ug_scr, *, seq):
    i = pl.program_id(0)
    f = pl.program_id(1)
    tm = h_ref.shape[0]
    halo = BF16_ROWS

    @pl.when(f == 0)
    def _():
        hext_scr[0:halo, :] = hp_ref[...]
        hext_scr[halo:halo + tm, :] = h_ref[...]
        hext_scr[halo + tm:, :] = hn_ref[...]
        acc_scr[...] = jnp.zeros_like(acc_scr)

    period = min(seq, tm)
    grp = lax.broadcasted_iota(jnp.int32, (SUBLANE, 1), 0)

    rc = min(tm, FFN_ROW_CHUNK)
    ext = tm + 2 * halo
    up_bounds = [0] + [min(ext, -(-(halo + (k + 1) * rc + 1) // BF16_ROWS) * BF16_ROWS) for k in range(tm // rc)]
    up_bounds[-1] = ext

    def up_chunk(k):
        lo, hi = up_bounds[k], up_bounds[k + 1]
        ua_scr[lo:hi, :] = _dot(hext_scr[lo:hi, :], wa_ref[...])
        ug_scr[lo:hi, :] = _dot(hext_scr[lo:hi, :], wg_ref[...])

    def mask_group(x, r0, first):
        g0 = 0 if first else rc - SUBLANE
        pos = (i * tm + r0 + g0 + grp) & (seq - 1)
        keep = (pos != 0) if first else (pos != seq - 1)
        masked = jnp.where(keep, x[g0:g0 + SUBLANE], 0.0)
        return jnp.concatenate([masked, x[SUBLANE:]] if first else [x[:g0], masked], axis=0)

    def down_chunk(k):
        r0 = k * rc

        def conv(u_scr, cw_ref, cb_ref):
            u_prev = u_scr[halo - 1 + r0:halo - 1 + r0 + rc, :]
            u_next = u_scr[halo + 1 + r0:halo + 1 + r0 + rc, :]
            if r0 % period == 0:
                u_prev = mask_group(u_prev, r0, True)
            if (r0 + rc) % period == 0:
                u_next = mask_group(u_next, r0, False)
            return (u_prev * cw_ref[0:1, :] + u_scr[halo + r0:halo + r0 + rc, :] * cw_ref[1:2, :]
                    + u_next * cw_ref[2:3, :] + cb_ref[...])

        a = conv(ua_scr, cwa_ref, cba_ref)
        g = conv(ug_scr, cwg_ref, cbg_ref)
        act = (g / (1.0 + jnp.exp2(g * NEG_LOG2_E))) * a
        acc_scr[r0:r0 + rc, :] += _dot(act.astype(BF16), wd_ref[...])

    up_chunk(0)
    for k in range(tm // rc):
        if k + 1 < tm // rc:
            up_chunk(k + 1)
        down_chunk(k)

    @pl.when(f == pl.num_programs(1) - 1)
    def _():
        z = ALPHA * x_ref[...] + mod_ref[0, 5:6, :] * acc_scr[...]
        o_ref[...] = _layernorm(z, g_ref, b_ref)


def _ffn(h2, x1, ffn_w, layer, mods_l, ln_g, ln_b, row_base, rows_per_cond, seq, tf):
    wu_bf, cw, cb, wd_bf = ffn_w
    m, d = x1.shape
    fp = wd_bf.shape[1]
    tm = min(_tile(m, 512, SUBLANE), rows_per_cond)
    per = rows_per_cond // tm
    assert seq & (seq - 1) == 0 and m % seq == 0
    nh = m // BF16_ROWS
    hb = tm // BF16_ROWS
    vec = pl.BlockSpec((1, d), lambda i, f: (0, 0))
    wcol = lambda rows, half: pl.BlockSpec((None, None, rows, tf),
                                           functools.partial(lambda i, f, hf: (layer, hf, 0, f), hf=half))
    return pl.pallas_call(
        functools.partial(_ffn_kernel, seq=seq),
        grid=(m // tm, fp // tf),
        in_specs=[pl.BlockSpec((tm, d), lambda i, f: (i, 0)),
                  pl.BlockSpec((BF16_ROWS, d), lambda i, f: (jnp.maximum(i * hb - 1, 0), 0)),
                  pl.BlockSpec((BF16_ROWS, d), lambda i, f: (jnp.minimum((i + 1) * hb, nh - 1), 0)),
                  wcol(d, 0), wcol(d, 1), wcol(CONV_W, 0), wcol(CONV_W, 1), wcol(1, 0), wcol(1, 1),
                  pl.BlockSpec((None, tf, d), lambda i, f: (layer, f, 0)),
                  pl.BlockSpec((tm, d), lambda i, f: (i, 0)),
                  pl.BlockSpec((1, 6, d), lambda i, f: (row_base + i // per, 0, 0)), vec, vec],
        out_specs=pl.BlockSpec((tm, d), lambda i, f: (i, 0)),
        out_shape=jax.ShapeDtypeStruct((m, d), F32),
        scratch_shapes=[pltpu.VMEM((tm + 2 * BF16_ROWS, d), BF16), pltpu.VMEM((tm, d), F32),
                        pltpu.VMEM((tm + 2 * BF16_ROWS, tf), F32), pltpu.VMEM((tm + 2 * BF16_ROWS, tf), F32)],
        compiler_params=_cparams("arbitrary", "arbitrary"),
        name="ffn_ln",
    )(h2, h2, h2, wu_bf, wu_bf, cw, cw, cb, cb, wd_bf, x1, mods_l, ln_g.reshape(1, d), ln_b.reshape(1, d))


def _pad_cols(w, n):
    return jnp.pad(w, [(0, 0)] * (w.ndim - 1) + [(0, n - w.shape[-1])])


def _cast_kernel(x_ref, o_ref, *, rows_valid, cols_valid):
    tr, tc = x_ref.shape
    x = x_ref[...]
    if rows_valid is not None:
        r = pl.program_id(1) * tr + lax.broadcasted_iota(jnp.int32, (tr, tc), 0)
        x = jnp.where(r < rows_valid, x, 0.0)
    if cols_valid is not None:
        c = pl.program_id(2) * tc + lax.broadcasted_iota(jnp.int32, (tr, tc), 1)
        x = jnp.where(c < cols_valid, x, 0.0)
    o_ref[:, :tc] = x.astype(BF16)
    if o_ref.shape[1] > tc:
        o_ref[:, tc:] = jnp.zeros((tr, o_ref.shape[1] - tc), BF16)


def _cast_weights(w, tr, tc, rows_out, cols_out):
    depth, r, c = w.shape
    nr, nc = -(-rows_out // tr), -(-cols_out // tc)
    assert nr * tr == rows_out and nc * tc == cols_out
    return pl.pallas_call(
        functools.partial(_cast_kernel, rows_valid=r if rows_out > r else None, cols_valid=c if cols_out > c else None),
        grid=(depth, nr, nc),
        in_specs=[pl.BlockSpec((None, tr, tc), lambda l, i, j: (l, i, j))],
        out_specs=pl.BlockSpec((None, tr, tc), lambda l, i, j: (l, i, j)),
        out_shape=jax.ShapeDtypeStruct((depth, rows_out, cols_out), BF16),
        compiler_params=_cparams("arbitrary", "arbitrary", "arbitrary"),
        name="cast_weights",
    )(w)


def _cast_in_kernel(x_ref, o_ref, *, n_valid):
    tn = x_ref.shape[0]
    col = pl.program_id(0) * tn + lax.broadcasted_iota(jnp.int32, (o_ref.shape[1], tn), 1)
    for l in range(x_ref.shape[1]):
        o_ref[l] = jnp.where(col < n_valid, x_ref[:, l, :].T, 0.0).astype(BF16)


def _cast_in_weights(w_in, n_pad, tn):
    depth, d, n = w_in.shape
    return pl.pallas_call(
        functools.partial(_cast_in_kernel, n_valid=n),
        grid=(n_pad // tn,),
        in_specs=[pl.BlockSpec((tn, depth, d), lambda j: (j, 0, 0))],
        out_specs=pl.BlockSpec((depth, d, tn), lambda j: (0, 0, j)),
        out_shape=jax.ShapeDtypeStruct((depth, d, n_pad), BF16),
        compiler_params=_cparams("arbitrary"),
        name="cast_in_weights",
    )(jnp.transpose(w_in, (2, 0, 1)))


def _cast_up_weights(w_up, dff, fp, tr):
    depth, d, _ = w_up.shape
    return pl.pallas_call(
        functools.partial(_cast_kernel, rows_valid=None, cols_valid=None),
        grid=(depth, d // tr, 2),
        in_specs=[pl.BlockSpec((None, tr, dff), lambda l, i, j: (l, i, j))],
        out_specs=pl.BlockSpec((None, None, tr, fp), lambda l, i, j: (l, j, i, 0)),
        out_shape=jax.ShapeDtypeStruct((depth, 2, d, fp), BF16),
        compiler_params=_cparams("arbitrary", "arbitrary", "arbitrary"),
        name="cast_up_weights",
    )(w_up)


def _prep_weights(w_in, w_out, w_up, conv_w, conv_b, w_down, tf):
    depth, d, n_in = w_in.shape
    n_pad = -(-n_in // LANE) * LANE
    dff = w_down.shape[1]
    fp = -(-dff // tf) * tf
    w_in_bf = _cast_in_weights(w_in, n_pad, _tile(n_pad, 384))
    w_out_bf = _cast_weights(w_out, _tile(d, 1024, SUBLANE), d, d, d)
    w_up_bf = _cast_up_weights(w_up, dff, fp, _tile(d, 512, SUBLANE))
    w_down_bf = _cast_weights(w_down, _tile(fp, 1536, tf), d, fp, d)
    halves = lambda t: jnp.stack([_pad_cols(t[..., :dff], fp), _pad_cols(t[..., dff:], fp)], axis=1)
    return w_in_bf, w_out_bf, w_up_bf, halves(conv_w), halves(conv_b[:, None, :]), w_down_bf


def _gate_bias_row(c_gate_b_l):
    return _pad_cols(c_gate_b_l.reshape(1, -1), LANE)


def kernel(x_prompt, x_sample, cache_a_k, cache_a_v, cache_b_k, cache_b_v, state_c_C, state_c_n, state_c_m, c, c_ctx, w_mod, b_mod, w_in, c_gate_b, a_rpb, b_lambda, b_subln, c_norm, w_out, ln1_g, ln1_b, ln2_g, ln2_b, w_up, conv_w, conv_b, w_down):
    batch, seq, d = x_prompt.shape
    dec_batch, dec_seq, _ = x_sample.shape
    depth = w_in.shape[0]
    tf = 512

    cond = jnp.concatenate([c_ctx[None, :], c], axis=0)
    cond = jnp.pad(cond, ((0, SUBLANE - cond.shape[0]), (0, 0)))
    mods = _mods(cond, w_mod, b_mod).reshape(depth, SUBLANE, 6, d)

    xp = x_prompt.reshape(batch * seq, d)
    xs = x_sample.reshape(dec_batch * dec_seq, d)
    cos, sin = _rope_tables(dec_seq)
    na_bias = _na_bias_table(a_rpb, dec_seq // GRID_W)

    w_in_bf, w_out_bf, *ffn_w = _prep_weights(w_in, w_out, w_up, conv_w, conv_b, w_down, tf)
    new_kv = None
    new_state = None
    for l in range(depth):
        lam_init = 0.8 - 0.6 * math.exp(-0.3 * l)
        gate_bias = _gate_bias_row(c_gate_b[l])

        pp = _inproj(xp, mods[l], w_in_bf, l, 0, batch * seq)
        ya, yb, new_kv = _ctx_attn(pp, batch, seq, b_lambda[l], b_subln[l], lam_init, l, depth, new_kv)
        yc, new_state = _mlstm(pp, batch, seq, gate_bias, c_norm[l], None, None, None, 0, depth, l, new_state)
        x1, h2 = _outproj(ya, yb, yc, w_out_bf, l, xp, mods[l], ln1_g[l], ln1_b[l], 0, batch * seq)
        xp = _ffn(h2, x1, ffn_w, l, mods[l], ln2_g[l], ln2_b[l], 0, batch * seq, seq, tf)

        ps = _inproj(xs, mods[l], w_in_bf, l, 1, dec_seq)
        ya = _na_attn(ps, dec_batch, dec_seq, cache_a_k, cache_a_v, l, na_bias)
        yb = _diff_lat_attn(ps, dec_batch, dec_seq, cache_b_k, cache_b_v, l, cos, sin, b_lambda[l], b_subln[l], lam_init)
        yc, _ = _mlstm(ps, dec_batch, dec_seq, gate_bias, c_norm[l], state_c_C, state_c_n, state_c_m, l, 1, 0, None)
        x1, h2 = _outproj(ya, yb, yc, w_out_bf, l, xs, mods[l], ln1_g[l], ln1_b[l], 1, dec_seq)
        xs = _ffn(h2, x1, ffn_w, l, mods[l], ln2_g[l], ln2_b[l], 1, dec_seq, dec_seq, tf)

    c_f, n_f, m_f = new_state
    return (xp.reshape(batch, seq, d), xs.reshape(dec_batch, dec_seq, d), *new_kv,
            c_f, n_f[..., 0, :], m_f[..., 0, 0])
```

```python
import functools
import math

import jax
import jax.numpy as jnp
import numpy as np
from jax import lax
from jax.experimental import pallas as pl
from jax.experimental.pallas import tpu as pltpu

F32 = jnp.float32
BF16 = jnp.bfloat16

DEPTH = 2
GRID_W = 64
A_HEADS = 6
NA_ROWS = 8
NA_COLS = 16
NA_QROWS = 4
B_HEADS = 5
B_QK = 64
C_HEADS = 5
HEAD_DIM = 128
CONV_W = 3
ROPE_BASE = 10000.0
LN_EPS = 1e-5
RMS_EPS = 1e-6
ALPHA = (2 * DEPTH) ** 0.25
LOG2_E = math.log2(math.e)
NEG_LOG2_E = -LOG2_E

LANE = 128
SUBLANE = 8
BF16_ROWS = 16
VMEM_LIMIT_BYTES = 56 * 1024 * 1024

MLSTM_CHUNK = 256
FFN_ROW_CHUNK = 256
OUT_ROW_CHUNK = 128
HI = lax.Precision.HIGHEST


def _tile(n, target, unit=LANE):
    if n <= target:
        return n
    best = unit
    for t in range(unit, target + 1, unit):
        if n % t == 0:
            best = t
    assert n % best == 0, (n, target, unit)
    return best


def _cparams(*sem):
    return pltpu.CompilerParams(dimension_semantics=sem, vmem_limit_bytes=VMEM_LIMIT_BYTES)


def _dot(a, b):
    return jnp.dot(a, b, preferred_element_type=F32)


def _dot_nt(a, b):
    return lax.dot_general(a, b, (((1,), (1,)), ((), ())), preferred_element_type=F32)


def _dot_tn(a, b):
    return lax.dot_general(a, b, (((0,), (0,)), ((), ())), preferred_element_type=F32)


def _mods_kernel(cond_ref, w_ref, b_ref, o_ref):
    c = cond_ref[...]
    s = c / (1.0 + jnp.exp(-c))
    o_ref[0] = _dot(s.astype(BF16), w_ref[0].astype(BF16)) + b_ref[0]


def _mods(cond, w_mod, b_mod):
    depth, d, n = w_mod.shape
    tn = _tile(n, 1536)
    return pl.pallas_call(
        _mods_kernel,
        grid=(depth, n // tn),
        in_specs=[pl.BlockSpec((SUBLANE, d), lambda l, j: (0, 0)),
                  pl.BlockSpec((1, d, tn), lambda l, j: (l, 0, j)),
                  pl.BlockSpec((1, 1, tn), lambda l, j: (l, 0, j))],
        out_specs=pl.BlockSpec((1, SUBLANE, tn), lambda l, j: (l, 0, j)),
        out_shape=jax.ShapeDtypeStruct((depth, SUBLANE, n), F32),
        compiler_params=_cparams("arbitrary", "arbitrary"),
        name="mods",
    )(cond, w_mod, b_mod.reshape(depth, 1, n))


def _inproj_kernel(x_ref, mod_ref, w_ref, o_ref, h_scr):
    @pl.when(pl.program_id(1) == 0)
    def _():
        sh = mod_ref[0, 0:1, :]
        sc = mod_ref[0, 1:2, :]
        h_scr[...] = (x_ref[...] * (1.0 + sc) + sh).astype(BF16)

    o_ref[...] = _dot(h_scr[...], w_ref[...])


def _inproj(x, mods_l, w_bf, layer, row_base, rows_per_cond):
    m, d = x.shape
    n = w_bf.shape[2]
    tm = _tile(m, 1024, SUBLANE)
    tm = min(tm, rows_per_cond)
    tn = _tile(n, 1152)
    per = rows_per_cond // tm
    return pl.pallas_call(
        _inproj_kernel,
        grid=(m // tm, n // tn),
        in_specs=[pl.BlockSpec((tm, d), lambda i, j: (i, 0)),
                  pl.BlockSpec((1, 6, d), lambda i, j: (row_base + i // per, 0, 0)),
                  pl.BlockSpec((None, d, tn), lambda i, j: (layer, 0, j))],
        out_specs=pl.BlockSpec((tm, tn), lambda i, j: (i, j)),
        out_shape=jax.ShapeDtypeStruct((m, n), F32),
        scratch_shapes=[pltpu.VMEM((tm, d), BF16)],
        compiler_params=_cparams("arbitrary", "arbitrary"),
        name="inproj",
    )(x, mods_l, w_bf)


def _lambda_value(lam_ref, lam_init):
    lf = lam_ref[...]
    t1 = jnp.sum(lf[0:1] * lf[1:2], axis=1, keepdims=True)
    t2 = jnp.sum(lf[2:3] * lf[3:4], axis=1, keepdims=True)
    return jnp.exp(t1) - jnp.exp(t2) + lam_init


def _split_maps(q):
    lane = lax.broadcasted_iota(jnp.int32, q.shape, 1)
    first = lane < B_QK
    return jnp.where(first, q, 0.0), jnp.where(first, 0.0, q)


def _subnorm(o, g_ref, lam_init):
    ms = jnp.mean(o * o, axis=-1, keepdims=True)
    return o * lax.rsqrt(ms + RMS_EPS) * g_ref[...] * (1.0 - lam_init)


def _ctx_attn_kernel(*refs, lam_init):
    qa_ref, ka_ref, va_ref = refs[0:3]
    b_refs = refs[3:3 + 3 * B_HEADS]
    lam_ref, subln_ref = refs[3 + 3 * B_HEADS:5 + 3 * B_HEADS]
    ya_ref, yb_ref, nak_ref, nav_ref, nbk_ref, nbv_ref = refs[-6:]
    a_scale = HEAD_DIM ** -0.5
    b_scale = B_QK ** -0.5
    lam = _lambda_value(lam_ref, lam_init)
    head = lambda h: slice(h * HEAD_DIM, (h + 1) * HEAD_DIM)

    def a_scores(h):
        nak_ref[h] = ka_ref[:, head(h)]
        nav_ref[h] = va_ref[:, head(h)]
        return (_dot_nt(qa_ref[:, head(h)].astype(BF16), ka_ref[:, head(h)].astype(BF16)),)

    def a_finish(h, s):
        s = s * a_scale
        e = jnp.exp(s - jnp.max(s, axis=-1, keepdims=True))
        l = jnp.sum(e, axis=-1, keepdims=True)
        ya_ref[:, head(h)] = (_dot(e.astype(BF16), va_ref[:, head(h)].astype(BF16)) / l).astype(ya_ref.dtype)

    def b_scores(h):
        q1, q2 = _split_maps(b_refs[3 * h][...] * b_scale)
        nbk_ref[h] = b_refs[3 * h + 1][...]
        nbv_ref[h] = b_refs[3 * h + 2][...]
        k = b_refs[3 * h + 1][...].astype(BF16)
        return _dot_nt(q1.astype(BF16), k), _dot_nt(q2.astype(BF16), k)

    def b_finish(h, s1, s2):
        e1 = jnp.exp(s1 - jnp.max(s1, axis=-1, keepdims=True))
        e2 = jnp.exp(s2 - jnp.max(s2, axis=-1, keepdims=True))
        p = e1 / jnp.sum(e1, axis=-1, keepdims=True) - lam * (e2 / jnp.sum(e2, axis=-1, keepdims=True))
        o = _dot(p.astype(BF16), b_refs[3 * h + 2][...].astype(BF16))
        yb_ref[:, head(h)] = _subnorm(o, subln_ref, lam_init).astype(yb_ref.dtype)

    stages = [(a_scores, a_finish, h) for h in range(A_HEADS)] + [(b_scores, b_finish, h) for h in range(B_HEADS)]
    pending = stages[0][0](stages[0][2])
    for idx, (_, finish, h) in enumerate(stages):
        nxt = stages[idx + 1][0](stages[idx + 1][2]) if idx + 1 < len(stages) else None
        finish(h, *pending)
        pending = nxt


def _ctx_attn(p, batch, seq, b_lambda_l, b_subln_l, lam_init, layer, depth, carried):
    m = p.shape[0]
    aw = A_HEADS * HEAD_DIM
    nb = aw // HEAD_DIM
    in_specs = [pl.BlockSpec((seq, aw), lambda b: (b, 0)),
                pl.BlockSpec((seq, aw), lambda b: (b, 1)),
                pl.BlockSpec((seq, aw), lambda b: (b, 2))]
    args = [p, p, p]
    for h in range(B_HEADS):
        for seg in range(3):
            col = 3 * nb + seg * B_HEADS + h
            in_specs.append(pl.BlockSpec((seq, HEAD_DIM), functools.partial(lambda b, c: (b, c), c=col)))
            args.append(p)
    in_specs += [pl.BlockSpec((4, B_QK), lambda b: (0, 0)), pl.BlockSpec((1, HEAD_DIM), lambda b: (0, 0))]
    args += [b_lambda_l, b_subln_l.reshape(1, HEAD_DIM)]
    aliases = {}
    if carried is not None:
        for j, buf in enumerate(carried):
            aliases[len(args)] = 2 + j
            in_specs.append(pl.BlockSpec(memory_space=pl.ANY))
            args.append(buf)
    kv_spec = lambda nh: pl.BlockSpec((None, None, nh, seq, HEAD_DIM), lambda b: (b, layer, 0, 0, 0))
    kv_shape = lambda nh: jax.ShapeDtypeStruct((batch, depth, nh, seq, HEAD_DIM), F32)
    out = pl.pallas_call(
        functools.partial(_ctx_attn_kernel, lam_init=lam_init),
        grid=(batch,),
        in_specs=in_specs,
        out_specs=[pl.BlockSpec((seq, aw), lambda b: (b, 0)),
                   pl.BlockSpec((seq, B_HEADS * HEAD_DIM), lambda b: (b, 0)),
                   kv_spec(A_HEADS), kv_spec(A_HEADS), kv_spec(B_HEADS), kv_spec(B_HEADS)],
        out_shape=[jax.ShapeDtypeStruct((m, aw), BF16),
                   jax.ShapeDtypeStruct((m, B_HEADS * HEAD_DIM), BF16),
                   kv_shape(A_HEADS), kv_shape(A_HEADS), kv_shape(B_HEADS), kv_shape(B_HEADS)],
        input_output_aliases=aliases,
        compiler_params=_cparams("arbitrary"),
        name="ctx_attn",
    )(*args)
    return out[0], out[1], tuple(out[2:])


def _na_plan(n_rows):
    wr = min(NA_ROWS, n_rows)
    qrows = NA_QROWS
    uw = -(-(wr + qrows - 1) // 2) * 2
    if n_rows % qrows or n_rows < uw:
        qrows, uw = 1, wr
    rs = lambda r: int(np.clip(r - wr // 2, 0, n_rows - wr))
    starts, var_of_block, variants = [], [], []
    for blk in range(n_rows // qrows):
        q0 = blk * qrows
        ws = int(np.clip(rs(q0), 0, n_rows - uw))
        valid = tuple(tuple(rs(q0 + i) <= ws + j < rs(q0 + i) + wr for j in range(uw)) for i in range(qrows))
        var = (q0 - ws, valid)
        if var not in variants:
            variants.append(var)
        starts.append(ws)
        var_of_block.append(variants.index(var))
    return qrows, uw, starts, var_of_block, variants


def _na_bias_kernel(toe_ref, o_ref, *, n_rows):
    qrows, uw, _, _, variants = _na_plan(n_rows)
    w = GRID_W
    neg = jnp.full((w, w), -jnp.inf, F32)
    for v, (off, valid) in enumerate(variants):
        for i in range(qrows):
            for j in range(uw):
                dr = NA_ROWS - 1 + j - off - i
                tile = toe_ref[dr] if valid[i][j] else neg
                o_ref[v, i * w:(i + 1) * w, j * w:(j + 1) * w] = tile


def _na_bias_table(a_rpb, n_rows):
    depth, heads, nr, _ = a_rpb.shape
    qrows, uw, _, _, variants = _na_plan(n_rows)
    cols = np.arange(GRID_W)
    cs = np.clip(cols - NA_COLS // 2, 0, GRID_W - NA_COLS)
    col_mask = (cols[None, :] >= cs[:, None]) & (cols[None, :] < cs[:, None] + NA_COLS)
    col_idx = np.clip(cols[None, :] - cols[:, None] + NA_COLS - 1, 0, 2 * NA_COLS - 2)
    onehot = (col_idx[None] == np.arange(2 * NA_COLS - 1)[:, None, None]).astype(np.float32)
    toe = jnp.einsum('lhrd,dqk->lhrqk', a_rpb.astype(F32), jnp.asarray(onehot), precision=HI)
    toe = jnp.where(jnp.asarray(col_mask), toe, -jnp.inf)
    shape = (len(variants), qrows * GRID_W, uw * GRID_W)
    return pl.pallas_call(
        functools.partial(_na_bias_kernel, n_rows=n_rows),
        grid=(depth, heads),
        in_specs=[pl.BlockSpec((None, None, nr, GRID_W, GRID_W), lambda l, h: (l, h, 0, 0, 0))],
        out_specs=pl.BlockSpec((None, None) + shape, lambda l, h: (l, h, 0, 0, 0)),
        out_shape=jax.ShapeDtypeStruct((depth, heads) + shape, F32),
        compiler_params=_cparams("arbitrary", "arbitrary"),
        name="na_bias",
    )(toe)


def _select_by_block(blk, values):
    out = jnp.int32(values[0])
    for k in range(1, len(values)):
        out = jnp.where(blk == k, jnp.int32(values[k]), out)
    return out


def _na_kernel(q_ref, k_ref, v_ref, kc_ref, vc_ref, bias_ref, o_ref, kb_scr, vb_scr, kcb_scr, vcb_scr, *, n_rows):
    scale = HEAD_DIM ** -0.5
    qrows, uw, starts, var_of_block, _ = _na_plan(n_rows)
    nq = qrows * GRID_W
    kb_scr[...] = k_ref[...].astype(BF16)
    vb_scr[...] = v_ref[...].astype(BF16)
    kcb_scr[...] = kc_ref[...].astype(BF16)
    vcb_scr[...] = vc_ref[...].astype(BF16)

    def scores(blk):
        q0 = pl.multiple_of(blk * nq, nq)
        q = q_ref[pl.ds(q0, nq), :].astype(BF16)
        w0 = pl.multiple_of(_select_by_block(blk, starts) * GRID_W, GRID_W)
        s_nb = _dot_nt(q, kb_scr[pl.ds(w0, uw * GRID_W), :])
        s_c = _dot_nt(q, kcb_scr[...])
        return q0, w0, s_nb, s_c

    def finish(blk, q0, w0, s_nb, s_c):
        s_nb = s_nb * scale + bias_ref[_select_by_block(blk, var_of_block)]
        s_c = s_c * scale
        mx = jnp.maximum(jnp.max(s_nb, axis=-1, keepdims=True), jnp.max(s_c, axis=-1, keepdims=True))
        p_nb = jnp.exp(s_nb - mx)
        p_c = jnp.exp(s_c - mx)
        l = jnp.sum(p_nb, axis=-1, keepdims=True) + jnp.sum(p_c, axis=-1, keepdims=True)
        vw = vb_scr[pl.ds(w0, uw * GRID_W), :]
        o = (_dot(p_nb.astype(BF16), vw) + _dot(p_c.astype(BF16), vcb_scr[...])) / l
        o_ref[pl.ds(q0, nq), :] = o.astype(o_ref.dtype)

    n_blocks = n_rows // qrows
    group = 4 if n_blocks % 4 == 0 else (2 if n_blocks % 2 == 0 else 1)

    def body(j, carry):
        blks = [j * group + g for g in range(group)]
        pending = [scores(b) for b in blks]
        for b, s in zip(blks, pending):
            finish(b, *s)
        return carry

    lax.fori_loop(0, n_blocks // group, body, 0)


def _na_attn(p, batch, seq, cache_k, cache_v, layer, bias):
    n_rows = seq // GRID_W
    past = cache_k.shape[3]
    blk = lambda off: pl.BlockSpec((seq, HEAD_DIM), lambda b, h: (b, off + h))
    cspec = pl.BlockSpec((None, None, None, past, HEAD_DIM), lambda b, h: (b, layer, h, 0, 0))
    return pl.pallas_call(
        functools.partial(_na_kernel, n_rows=n_rows),
        grid=(batch, A_HEADS),
        in_specs=[blk(0), blk(A_HEADS), blk(2 * A_HEADS), cspec, cspec,
                  pl.BlockSpec((None, None) + bias.shape[2:], lambda b, h: (layer, h, 0, 0, 0))],
        out_specs=pl.BlockSpec((seq, HEAD_DIM), lambda b, h: (b, h)),
        out_shape=jax.ShapeDtypeStruct((batch * seq, A_HEADS * HEAD_DIM), BF16),
        scratch_shapes=[pltpu.VMEM((seq, HEAD_DIM), BF16), pltpu.VMEM((seq, HEAD_DIM), BF16),
                        pltpu.VMEM((past, HEAD_DIM), BF16), pltpu.VMEM((past, HEAD_DIM), BF16)],
        compiler_params=_cparams("arbitrary", "arbitrary"),
        name="na_attn",
    )(p, p, p, cache_k, cache_v, bias)


def _rope_tables(seq):
    t = np.arange(seq)
    rows = (t // GRID_W).astype(np.float32)
    cols = (t % GRID_W).astype(np.float32)
    half = B_QK // 2
    freqs = jnp.asarray(ROPE_BASE, F32) ** (-jnp.arange(0, half, 2, dtype=F32) / half)
    ar = jnp.asarray(rows)[:, None] * freqs
    ac = jnp.asarray(cols)[:, None] * freqs
    cr, sr, cc, sc = jnp.cos(ar), jnp.sin(ar), jnp.cos(ac), jnp.sin(ac)
    cos = jnp.concatenate([cr, cr, cc, cc] * 2, axis=-1)
    sin = jnp.concatenate([-sr, sr, -sc, sc] * 2, axis=-1)
    return cos, sin


def _rope(x, cos, sin):
    q = B_QK // 4
    lane = lax.broadcasted_iota(jnp.int32, x.shape, 1)
    first = (lane & (2 * q - 1)) < q
    partner = jnp.where(first, pltpu.roll(x, LANE - q, 1), pltpu.roll(x, q, 1))
    return x * cos + partner * sin


def _diff_lat_kernel(q_ref, k_ref, v_ref, kc_ref, vc_ref, cosq_ref, sinq_ref, cosk_ref, sink_ref, lam_ref, subln_ref,
                     o_ref, kr_scr, kcb_scr, vb_scr, vcb_scr, *, lam_init):
    scale = B_QK ** -0.5

    @pl.when(pl.program_id(2) == 0)
    def _():
        kr_scr[...] = _rope(k_ref[...], cosk_ref[...], sink_ref[...]).astype(BF16)
        kcb_scr[...] = kc_ref[...].astype(BF16)
        vb_scr[...] = v_ref[...].astype(BF16)
        vcb_scr[...] = vc_ref[...].astype(BF16)

    lam = _lambda_value(lam_ref, lam_init)
    q = q_ref[...] * (scale * LOG2_E)
    qr1, qr2 = _split_maps(_rope(q, cosq_ref[...], sinq_ref[...]))
    q1, q2 = _split_maps(q)

    def numerators(qr, qp):
        s_l = _dot_nt(qr.astype(BF16), kr_scr[...])
        s_c = _dot_nt(qp.astype(BF16), kcb_scr[...])
        mx = jnp.maximum(jnp.max(s_l, axis=-1, keepdims=True), jnp.max(s_c, axis=-1, keepdims=True))
        e_l = jnp.exp2(s_l - mx)
        e_c = jnp.exp2(s_c - mx)
        inv = 1.0 / (jnp.sum(e_l, axis=-1, keepdims=True) + jnp.sum(e_c, axis=-1, keepdims=True))
        return e_l, e_c, inv

    e1_l, e1_c, inv1 = numerators(qr1, q1)
    e2_l, e2_c, inv2 = numerators(qr2, q2)
    w2 = lam * inv2
    o = (_dot((e1_l * inv1 - e2_l * w2).astype(BF16), vb_scr[...])
         + _dot((e1_c * inv1 - e2_c * w2).astype(BF16), vcb_scr[...]))
    o_ref[...] = _subnorm(o, subln_ref, lam_init).astype(o_ref.dtype)


def _diff_lat_attn(p, batch, seq, cache_k, cache_v, layer, cos, sin, b_lambda_l, b_subln_l, lam_init):
    past = cache_k.shape[3]
    tq = _tile(seq, 256, SUBLANE)
    nq = seq // tq
    base = 3 * A_HEADS
    cspec = pl.BlockSpec((None, None, None, past, HEAD_DIM), lambda b, h, i: (b, layer, h, 0, 0))
    kv = lambda off: pl.BlockSpec((seq, HEAD_DIM), lambda b, h, i: (b, off + h))
    return pl.pallas_call(
        functools.partial(_diff_lat_kernel, lam_init=lam_init),
        grid=(batch, B_HEADS, nq),
        in_specs=[pl.BlockSpec((tq, HEAD_DIM), lambda b, h, i: (b * nq + i, base + h)),
                  kv(base + B_HEADS), kv(base + 2 * B_HEADS), cspec, cspec,
                  pl.BlockSpec((tq, HEAD_DIM), lambda b, h, i: (i, 0)),
                  pl.BlockSpec((tq, HEAD_DIM), lambda b, h, i: (i, 0)),
                  pl.BlockSpec((seq, HEAD_DIM), lambda b, h, i: (0, 0)),
                  pl.BlockSpec((seq, HEAD_DIM), lambda b, h, i: (0, 0)),
                  pl.BlockSpec((4, B_QK), lambda b, h, i: (0, 0)),
                  pl.BlockSpec((1, HEAD_DIM), lambda b, h, i: (0, 0))],
        out_specs=pl.BlockSpec((tq, HEAD_DIM), lambda b, h, i: (b * nq + i, h)),
        out_shape=jax.ShapeDtypeStruct((batch * seq, B_HEADS * HEAD_DIM), BF16),
        scratch_shapes=[pltpu.VMEM((seq, HEAD_DIM), BF16), pltpu.VMEM((past, HEAD_DIM), BF16),
                        pltpu.VMEM((seq, HEAD_DIM), BF16), pltpu.VMEM((past, HEAD_DIM), BF16)],
        compiler_params=_cparams("arbitrary", "arbitrary", "arbitrary"),
        name="diff_lat_attn",
    )(p, p, p, cache_k, cache_v, cos, sin, cos, sin, b_lambda_l, b_subln_l.reshape(1, HEAD_DIM))


def _log_sigmoid(x):
    return jnp.minimum(x, 0.0) - jnp.log1p(jnp.exp(-jnp.abs(x)))


def _split3(x):
    hi = x.astype(BF16)
    r = x - hi.astype(F32)
    mid = r.astype(BF16)
    lo = (r - mid.astype(F32)).astype(BF16)
    return hi, mid, lo


def _mlstm_kernel(*refs, chunk, n_chunks, zero_init):
    nh = C_HEADS
    q_refs, k_refs, v_refs, og_refs = (refs[i * nh:(i + 1) * nh] for i in range(4))
    g_ref, gb_ref, cn_ref = refs[4 * nh:4 * nh + 3]
    if not zero_init:
        c0_ref, n0_ref, m0_ref = refs[4 * nh + 3:4 * nh + 6]
    y_ref, cf_ref, nf_ref, mf_ref, hf_scr, hb_scr = refs[-6:]
    scale = HEAD_DIM ** -0.5
    ln = chunk
    n_sel = 4 * SUBLANE
    assert 4 * nh <= n_sel
    sel = jnp.where(lax.broadcasted_iota(jnp.int32, (n_sel, LANE), 0)
                    == lax.broadcasted_iota(jnp.int32, (n_sel, LANE), 1), 1.0, 0.0).astype(BF16)
    ti = lax.broadcasted_iota(jnp.int32, (ln, ln), 0)
    si = lax.broadcasted_iota(jnp.int32, (ln, ln), 1)
    lower = si <= ti
    upper = si >= ti
    lower_b = jnp.where(lower, 1.0, 0.0).astype(BF16)
    upper_b = jnp.where(upper, 1.0, 0.0).astype(BF16)

    def gate_terms(r0, directions):
        g = g_ref[pl.ds(r0, ln), :] + gb_ref[...]
        gp = _split3(g)
        rows = sum(_dot_nt(sel, p) for p in gp)
        cols = sum(_dot_nt(p, sel) for p in gp)
        lf_rows = _split3(_log_sigmoid(rows))
        lf_cols = _split3(_log_sigmoid(cols))
        t = dict(rows=rows, cols=cols)
        if 0 in directions:
            t["row0"] = sum(_dot(p, upper_b) for p in lf_rows)
            t["col0"] = sum(_dot(lower_b, p) for p in lf_cols)
        if 1 in directions:
            t["row1"] = sum(_dot(p, lower_b) for p in lf_rows)
            t["col1"] = sum(_dot(upper_b, p) for p in lf_cols)
        return t

    def head_terms(h, r0):
        qf = q_refs[h][pl.ds(r0, ln), :] * scale
        kf = k_refs[h][pl.ds(r0, ln), :]
        qb = qf.astype(BF16)
        vb = v_refs[h][pl.ds(r0, ln), :].astype(BF16)
        return dict(qf=qf, kf=kf, qb=qb, vb=vb, qk=_dot_nt(qb, kf.astype(BF16)))

    def chunk_step(gt, t, h, state, backward):
        rows, cols, qf, kf, qb, vb = gt["rows"], gt["cols"], t["qf"], t["kf"], t["qb"], t["vb"]
        gi = (2 * nh if backward else 0) + h
        gf = gi + nh
        i_row, i_col = rows[gi:gi + 1], cols[:, gi:gi + 1]
        if backward:
            b_row, b_col, mask = gt["row1"][gf:gf + 1], gt["col1"][:, gf:gf + 1], upper
            b_last = b_col[0:1]
        else:
            b_row, b_col, mask = gt["row0"][gf:gf + 1], gt["col0"][:, gf:gf + 1], lower
            b_last = b_col[ln - 1:ln]
        d = jnp.where(mask, b_col - b_row + i_row, -jnp.inf)
        gg = b_last - b_col + i_col
        if state is None:
            m_row = jnp.maximum(jnp.max(d, axis=-1, keepdims=True), b_col)
            sc = t["qk"] * jnp.exp(d - m_row)
            num = _dot(sc.astype(BF16), vb)
            den = jnp.sum(sc, axis=-1, keepdims=True)
            m_new = jnp.maximum(b_last, jnp.max(gg, axis=0, keepdims=True))
            kw = kf * jnp.exp(gg - m_new)
            c_new = _dot_tn(kw.astype(BF16), vb)
            n_new = jnp.sum(kw, axis=0, keepdims=True)
        else:
            c_st, n_st, m_st = state
            inter = b_col + m_st
            m_row = jnp.maximum(jnp.max(d, axis=-1, keepdims=True), inter)
            w_state = jnp.exp(inter - m_row)
            sc = t["qk"] * jnp.exp(d - m_row)
            num = w_state * _dot(qb, c_st.astype(BF16)) + _dot(sc.astype(BF16), vb)
            den = w_state * jnp.sum(qf * n_st, axis=-1, keepdims=True) + jnp.sum(sc, axis=-1, keepdims=True)
            m_new = jnp.maximum(b_last + m_st, jnp.max(gg, axis=0, keepdims=True))
            w_old = jnp.exp(b_last + m_st - m_new)
            kw = kf * jnp.exp(gg - m_new)
            c_new = w_old * c_st + _dot_tn(kw.astype(BF16), vb)
            n_new = w_old * n_st + jnp.sum(kw, axis=0, keepdims=True)
        h_out = num / jnp.maximum(jnp.abs(den), jnp.exp(-m_row))
        return h_out, (c_new, n_new, m_new)

    def start(c):
        return pl.multiple_of(c * ln, ln)

    def finish(h, r0, hs):
        ms = jnp.mean(hs * hs, axis=-1, keepdims=True)
        hn = hs * lax.rsqrt(ms + RMS_EPS) * cn_ref[...]
        og = og_refs[h][pl.ds(r0, ln), :]
        y_ref[pl.ds(r0, ln), h * HEAD_DIM:(h + 1) * HEAD_DIM] = (hn / (1.0 + jnp.exp(-og))).astype(y_ref.dtype)

    if zero_init:
        assert n_chunks == 1
        init = ((None,) * nh,) * 2
    else:
        init = tuple(tuple((c0_ref[dr, h], n0_ref[dr, h], m0_ref[dr, h][:, 0:1]) for h in range(nh))
                     for dr in (0, 1))
    if n_chunks == 1:
        gt = gate_terms(0, (0, 1))
        st_f, st_b = [], []
        for h in range(nh):
            t = head_terms(h, 0)
            h_f, s_f = chunk_step(gt, t, h, init[0][h], False)
            h_b, s_b = chunk_step(gt, t, h, init[1][h], True)
            finish(h, 0, h_f + h_b)
            st_f.append(s_f)
            st_b.append(s_b)
        final = (st_f, st_b)
    else:
        def body(j, states):
            r_f = start(j)
            r_b = start(n_chunks - 1 - j)
            gt_f = gate_terms(r_f, (0,))
            gt_b = gate_terms(r_b, (1,))
            terms_f = [head_terms(h, r_f) for h in range(nh)]
            terms_b = [head_terms(h, r_b) for h in range(nh)]
            st_f, st_b = [], []
            for h in range(nh):
                sl = slice(h * HEAD_DIM, (h + 1) * HEAD_DIM)
                h_f, s_f = chunk_step(gt_f, terms_f[h], h, states[0][h], False)
                h_b, s_b = chunk_step(gt_b, terms_b[h], h, states[1][h], True)
                hf_scr[pl.ds(r_f, ln), sl] = h_f
                hb_scr[pl.ds(r_b, ln), sl] = h_b
                st_f.append(s_f)
                st_b.append(s_b)
            return tuple(st_f), tuple(st_b)

        final = lax.fori_loop(0, n_chunks, body, init)

        def fin_body(j, carry):
            r0 = start(j)
            for h in range(nh):
                sl = slice(h * HEAD_DIM, (h + 1) * HEAD_DIM)
                finish(h, r0, hf_scr[pl.ds(r0, ln), sl] + hb_scr[pl.ds(r0, ln), sl])
            return carry

        lax.fori_loop(0, n_chunks, fin_body, 0)
    for dr in (0, 1):
        for h in range(nh):
            c_f, n_f, m_f = final[dr][h]
            cf_ref[dr, h] = c_f
            nf_ref[dr, h] = n_f
            mf_ref[dr, h] = jnp.broadcast_to(m_f, (1, LANE))


def _mlstm(p, batch, seq, gate_bias, c_norm_l, c0, n0, m0, lin, out_depth, lout, carried):
    chunk = min(MLSTM_CHUNK, seq)
    n_chunks = seq // chunk
    base = 3 * A_HEADS + 3 * B_HEADS
    hd = HEAD_DIM
    nh = C_HEADS
    head_bytes = 4 * nh * seq * hd * 4
    mode = dict(pipeline_mode=pl.Buffered(1)) if 2 * head_bytes > VMEM_LIMIT_BYTES // 2 else {}
    blk = lambda col: pl.BlockSpec((seq, hd), functools.partial(lambda b, c: (b, c), c=col), **mode)
    st_c = lambda l: pl.BlockSpec((None, None, 2, nh, hd, hd), lambda b: (b, l, 0, 0, 0, 0))
    st_n = lambda l: pl.BlockSpec((None, None, 2, nh, 1, hd), lambda b: (b, l, 0, 0, 0, 0))
    if c0 is None and n_chunks > 1:
        c0 = jnp.zeros((batch, 1, 2, nh, hd, hd), F32)
        n0 = jnp.zeros((batch, 1, 2, nh, hd), F32)
        m0 = jnp.zeros((batch, 1, 2, nh), F32)
    zero_init = c0 is None
    in_specs = ([blk(base + seg * nh + h) for seg in range(4) for h in range(nh)]
                + [pl.BlockSpec((seq, LANE), lambda b: (b, base + 4 * nh)),
                   pl.BlockSpec((1, LANE), lambda b: (0, 0)),
                   pl.BlockSpec((1, hd), lambda b: (0, 0))])
    args = [p] * (4 * nh + 1) + [gate_bias, c_norm_l.reshape(1, hd)]
    if not zero_init:
        in_specs += [st_c(lin), st_n(lin), st_n(lin)]
        args += [c0, n0.reshape(n0.shape[:4] + (1, hd)), jnp.broadcast_to(m0[..., None, None], m0.shape + (1, LANE))]
    aliases = {}
    if carried is not None:
        for j, buf in enumerate(carried):
            aliases[len(args)] = 1 + j
            in_specs.append(pl.BlockSpec(memory_space=pl.ANY))
            args.append(buf)
    out = pl.pallas_call(
        functools.partial(_mlstm_kernel, chunk=chunk, n_chunks=n_chunks, zero_init=zero_init),
        grid=(batch,),
        in_specs=in_specs,
        out_specs=[pl.BlockSpec((seq, nh * hd), lambda b: (b, 0)), st_c(lout), st_n(lout), st_n(lout)],
        out_shape=[jax.ShapeDtypeStruct((batch * seq, nh * hd), BF16),
                   jax.ShapeDtypeStruct((batch, out_depth, 2, nh, hd, hd), F32),
                   jax.ShapeDtypeStruct((batch, out_depth, 2, nh, 1, hd), F32),
                   jax.ShapeDtypeStruct((batch, out_depth, 2, nh, 1, LANE), F32)],
        scratch_shapes=[pltpu.VMEM((seq, nh * hd), F32), pltpu.VMEM((seq, nh * hd), F32)],
        input_output_aliases=aliases,
        compiler_params=_cparams("arbitrary"),
        name="mlstm",
    )(*args)
    return out[0], tuple(out[1:])


def _layernorm(z, g_ref, b_ref):
    mu = jnp.mean(z, axis=-1, keepdims=True)
    zc = z - mu
    var = jnp.mean(zc * zc, axis=-1, keepdims=True)
    return zc * lax.rsqrt(var + LN_EPS) * g_ref[...] + b_ref[...]


def _outproj_kernel(ya_ref, yb_ref, yc_ref, w_ref, x_ref, mod_ref, g_ref, b_ref, x1_ref, h2_ref):
    tm = x_ref.shape[0]
    rc = min(tm, OUT_ROW_CHUNK)

    def proj(c):
        rows = slice(c * rc, (c + 1) * rc)
        ycat = jnp.concatenate([ya_ref[rows, :], yb_ref[rows, :], yc_ref[rows, :]], axis=-1)
        return _dot(ycat, w_ref[...])

    def finish(c, y):
        rows = slice(c * rc, (c + 1) * rc)
        x1 = _layernorm(ALPHA * x_ref[rows, :] + mod_ref[0, 2:3, :] * y, g_ref, b_ref)
        x1_ref[rows, :] = x1
        h2_ref[rows, :] = (x1 * (1.0 + mod_ref[0, 4:5, :]) + mod_ref[0, 3:4, :]).astype(BF16)

    y = proj(0)
    for c in range(tm // rc):
        y_next = proj(c + 1) if c + 1 < tm // rc else None
        finish(c, y)
        y = y_next


def _outproj(ya, yb, yc, w_bf, layer, x, mods_l, ln_g, ln_b, row_base, rows_per_cond):
    m, d = x.shape
    tm = min(_tile(m, 512, SUBLANE), rows_per_cond)
    per = rows_per_cond // tm
    row = lambda w: pl.BlockSpec((tm, w), lambda i: (i, 0))
    vec = pl.BlockSpec((1, d), lambda i: (0, 0))
    return pl.pallas_call(
        _outproj_kernel,
        grid=(m // tm,),
        in_specs=[row(ya.shape[1]), row(yb.shape[1]), row(yc.shape[1]),
                  pl.BlockSpec((None, d, d), lambda i: (layer, 0, 0)), row(d),
                  pl.BlockSpec((1, 6, d), lambda i: (row_base + i // per, 0, 0)), vec, vec],
        out_specs=[row(d), row(d)],
        out_shape=[jax.ShapeDtypeStruct((m, d), F32), jax.ShapeDtypeStruct((m, d), BF16)],
        compiler_params=_cparams("arbitrary"),
        name="outproj_ln",
    )(ya, yb, yc, w_bf, x, mods_l, ln_g.reshape(1, d), ln_b.reshape(1, d))


def _ffn_kernel(h_ref, hp_ref, hn_ref, wa_ref, wg_ref, cwa_ref, cwg_ref, cba_ref, cbg_ref, wd_ref, x_ref, mod_ref,
                g_ref, b_ref, o_ref, hext_scr, acc_scr, ua_scr, ug_scr, *, seq):
    i = pl.program_id(0)
    f = pl.program_id(1)
    tm = h_ref.shape[0]
    halo = BF16_ROWS

    @pl.when(f == 0)
    def _():
        hext_scr[0:halo, :] = hp_ref[...]
        hext_scr[halo:halo + tm, :] = h_ref[...]
        hext_scr[halo + tm:, :] = hn_ref[...]
        acc_scr[...] = jnp.zeros_like(acc_scr)

    period = min(seq, tm)
    grp = lax.broadcasted_iota(jnp.int32, (SUBLANE, 1), 0)

    rc = min(tm, FFN_ROW_CHUNK)
    ext = tm + 2 * halo
    up_bounds = [0] + [min(ext, -(-(halo + (k + 1) * rc + 1) // BF16_ROWS) * BF16_ROWS) for k in range(tm // rc)]
    up_bounds[-1] = ext

    def up_chunk(k):
        lo, hi = up_bounds[k], up_bounds[k + 1]
        ua_scr[lo:hi, :] = _dot(hext_scr[lo:hi, :], wa_ref[...])
        ug_scr[lo:hi, :] = _dot(hext_scr[lo:hi, :], wg_ref[...])

    def mask_group(x, r0, first):
        g0 = 0 if first else rc - SUBLANE
        pos = (i * tm + r0 + g0 + grp) & (seq - 1)
        keep = (pos != 0) if first else (pos != seq - 1)
        masked = jnp.where(keep, x[g0:g0 + SUBLANE], 0.0)
        return jnp.concatenate([masked, x[SUBLANE:]] if first else [x[:g0], masked], axis=0)

    def down_chunk(k):
        r0 = k * rc

        def conv(u_scr, cw_ref, cb_ref):
            u_prev = u_scr[halo - 1 + r0:halo - 1 + r0 + rc, :]
            u_next = u_scr[halo + 1 + r0:halo + 1 + r0 + rc, :]
            if r0 % period == 0:
                u_prev = mask_group(u_prev, r0, True)
            if (r0 + rc) % period == 0:
                u_next = mask_group(u_next, r0, False)
            return (u_prev * cw_ref[0:1, :] + u_scr[halo + r0:halo + r0 + rc, :] * cw_ref[1:2, :]
                    + u_next * cw_ref[2:3, :] + cb_ref[...])

        a = conv(ua_scr, cwa_ref, cba_ref)
        g = conv(ug_scr, cwg_ref, cbg_ref)
        act = (g / (1.0 + jnp.exp2(g * NEG_LOG2_E))) * a
        acc_scr[r0:r0 + rc, :] += _dot(act.astype(BF16), wd_ref[...])

    up_chunk(0)
    for k in range(tm // rc):
        if k + 1 < tm // rc:
            up_chunk(k + 1)
        down_chunk(k)

    @pl.when(f == pl.num_programs(1) - 1)
    def _():
        z = ALPHA * x_ref[...] + mod_ref[0, 5:6, :] * acc_scr[...]
        o_ref[...] = _layernorm(z, g_ref, b_ref)


def _ffn(h2, x1, ffn_w, layer, mods_l, ln_g, ln_b, row_base, rows_per_cond, seq, tf):
    wu_bf, cw, cb, wd_bf = ffn_w
    m, d = x1.shape
    fp = wd_bf.shape[1]
    tm = min(_tile(m, 512, SUBLANE), rows_per_cond)
    per = rows_per_cond // tm
    assert seq & (seq - 1) == 0 and m % seq == 0
    nh = m // BF16_ROWS
    hb = tm // BF16_ROWS
    vec = pl.BlockSpec((1, d), lambda i, f: (0, 0))
    wcol = lambda rows, half: pl.BlockSpec((None, None, rows, tf),
                                           functools.partial(lambda i, f, hf: (layer, hf, 0, f), hf=half))
    return pl.pallas_call(
        functools.partial(_ffn_kernel, seq=seq),
        grid=(m // tm, fp // tf),
        in_specs=[pl.BlockSpec((tm, d), lambda i, f: (i, 0)),
                  pl.BlockSpec((BF16_ROWS, d), lambda i, f: (jnp.maximum(i * hb - 1, 0), 0)),
                  pl.BlockSpec((BF16_ROWS, d), lambda i, f: (jnp.minimum((i + 1) * hb, nh - 1), 0)),
                  wcol(d, 0), wcol(d, 1), wcol(CONV_W, 0), wcol(CONV_W, 1), wcol(1, 0), wcol(1, 1),
                  pl.BlockSpec((None, tf, d), lambda i, f: (layer, f, 0)),
                  pl.BlockSpec((tm, d), lambda i, f: (i, 0)),
                  pl.BlockSpec((1, 6, d), lambda i, f: (row_base + i // per, 0, 0)), vec, vec],
        out_specs=pl.BlockSpec((tm, d), lambda i, f: (i, 0)),
        out_shape=jax.ShapeDtypeStruct((m, d), F32),
        scratch_shapes=[pltpu.VMEM((tm + 2 * BF16_ROWS, d), BF16), pltpu.VMEM((tm, d), F32),
                        pltpu.VMEM((tm + 2 * BF16_ROWS, tf), F32), pltpu.VMEM((tm + 2 * BF16_ROWS, tf), F32)],
        compiler_params=_cparams("arbitrary", "arbitrary"),
        name="ffn_ln",
    )(h2, h2, h2, wu_bf, wu_bf, cw, cw, cb, cb, wd_bf, x1, mods_l, ln_g.reshape(1, d), ln_b.reshape(1, d))


def _pad_cols(w, n):
    return jnp.pad(w, [(0, 0)] * (w.ndim - 1) + [(0, n - w.shape[-1])])


def _cast_kernel(x_ref, o_ref, *, rows_valid, cols_valid):
    tr, tc = x_ref.shape
    x = x_ref[...]
    if rows_valid is not None:
        r = pl.program_id(1) * tr + lax.broadcasted_iota(jnp.int32, (tr, tc), 0)
        x = jnp.where(r < rows_valid, x, 0.0)
    if cols_valid is not None:
        c = pl.program_id(2) * tc + lax.broadcasted_iota(jnp.int32, (tr, tc), 1)
        x = jnp.where(c < cols_valid, x, 0.0)
    o_ref[:, :tc] = x.astype(BF16)
    if o_ref.shape[1] > tc:
        o_ref[:, tc:] = jnp.zeros((tr, o_ref.shape[1] - tc), BF16)


def _cast_weights(w, tr, tc, rows_out, cols_out):
    depth, r, c = w.shape
    nr, nc = -(-rows_out // tr), -(-cols_out // tc)
    assert nr * tr == rows_out and nc * tc == cols_out
    return pl.pallas_call(
        functools.partial(_cast_kernel, rows_valid=r if rows_out > r else None, cols_valid=c if cols_out > c else None),
        grid=(depth, nr, nc),
        in_specs=[pl.BlockSpec((None, tr, tc), lambda l, i, j: (l, i, j))],
        out_specs=pl.BlockSpec((None, tr, tc), lambda l, i, j: (l, i, j)),
        out_shape=jax.ShapeDtypeStruct((depth, rows_out, cols_out), BF16),
        compiler_params=_cparams("arbitrary", "arbitrary", "arbitrary"),
        name="cast_weights",
    )(w)


def _cast_in_kernel(x_ref, o_ref, *, n_valid):
    tn = x_ref.shape[0]
    col = pl.program_id(0) * tn + lax.broadcasted_iota(jnp.int32, (o_ref.shape[1], tn), 1)
    for l in range(x_ref.shape[1]):
        o_ref[l] = jnp.where(col < n_valid, x_ref[:, l, :].T, 0.0).astype(BF16)


def _cast_in_weights(w_in, n_pad, tn):
    depth, d, n = w_in.shape
    return pl.pallas_call(
        functools.partial(_cast_in_kernel, n_valid=n),
        grid=(n_pad // tn,),
        in_specs=[pl.BlockSpec((tn, depth, d), lambda j: (j, 0, 0))],
        out_specs=pl.BlockSpec((depth, d, tn), lambda j: (0, 0, j)),
        out_shape=jax.ShapeDtypeStruct((depth, d, n_pad), BF16),
        compiler_params=_cparams("arbitrary"),
        name="cast_in_weights",
    )(jnp.transpose(w_in, (2, 0, 1)))


def _cast_up_weights(w_up, dff, fp, tr):
    depth, d, _ = w_up.shape
    return pl.pallas_call(
        functools.partial(_cast_kernel, rows_valid=None, cols_valid=None),
        grid=(depth, d // tr, 2),
        in_specs=[pl.BlockSpec((None, tr, dff), lambda l, i, j: (l, i, j))],
        out_specs=pl.BlockSpec((None, None, tr, fp), lambda l, i, j: (l, j, i, 0)),
        out_shape=jax.ShapeDtypeStruct((depth, 2, d, fp), BF16),
        compiler_params=_cparams("arbitrary", "arbitrary", "arbitrary"),
        name="cast_up_weights",
    )(w_up)


def _prep_weights(w_in, w_out, w_up, conv_w, conv_b, w_down, tf):
    depth, d, n_in = w_in.shape
    n_pad = -(-n_in // LANE) * LANE
    dff = w_down.shape[1]
    fp = -(-dff // tf) * tf
    w_in_bf = _cast_in_weights(w_in, n_pad, _tile(n_pad, 768))
    w_out_bf = _cast_weights(w_out, _tile(d, 1024, SUBLANE), d, d, d)
    w_up_bf = _cast_up_weights(w_up, dff, fp, _tile(d, 512, SUBLANE))
    w_down_bf = _cast_weights(w_down, _tile(fp, 1536, tf), d, fp, d)
    halves = lambda t: jnp.stack([_pad_cols(t[..., :dff], fp), _pad_cols(t[..., dff:], fp)], axis=1)
    return w_in_bf, w_out_bf, w_up_bf, halves(conv_w), halves(conv_b[:, None, :]), w_down_bf


def _gate_bias_row(c_gate_b_l):
    return _pad_cols(c_gate_b_l.reshape(1, -1), LANE)


def kernel(x_prompt, x_sample, cache_a_k, cache_a_v, cache_b_k, cache_b_v, state_c_C, state_c_n, state_c_m, c, c_ctx, w_mod, b_mod, w_in, c_gate_b, a_rpb, b_lambda, b_subln, c_norm, w_out, ln1_g, ln1_b, ln2_g, ln2_b, w_up, conv_w, conv_b, w_down):
    batch, seq, d = x_prompt.shape
    dec_batch, dec_seq, _ = x_sample.shape
    depth = w_in.shape[0]
    tf = 512

    cond = jnp.concatenate([c_ctx[None, :], c], axis=0)
    cond = jnp.pad(cond, ((0, SUBLANE - cond.shape[0]), (0, 0)))
    mods = _mods(cond, w_mod, b_mod).reshape(depth, SUBLANE, 6, d)

    xp = x_prompt.reshape(batch * seq, d)
    xs = x_sample.reshape(dec_batch * dec_seq, d)
    cos, sin = _rope_tables(dec_seq)
    na_bias = _na_bias_table(a_rpb, dec_seq // GRID_W)

    w_in_bf, w_out_bf, *ffn_w = _prep_weights(w_in, w_out, w_up, conv_w, conv_b, w_down, tf)
    new_kv = None
    new_state = None
    for l in range(depth):
        lam_init = 0.8 - 0.6 * math.exp(-0.3 * l)
        gate_bias = _gate_bias_row(c_gate_b[l])

        pp = _inproj(xp, mods[l], w_in_bf, l, 0, batch * seq)
        ya, yb, new_kv = _ctx_attn(pp, batch, seq, b_lambda[l], b_subln[l], lam_init, l, depth, new_kv)
        yc, new_state = _mlstm(pp, batch, seq, gate_bias, c_norm[l], None, None, None, 0, depth, l, new_state)
        x1, h2 = _outproj(ya, yb, yc, w_out_bf, l, xp, mods[l], ln1_g[l], ln1_b[l], 0, batch * seq)
        xp = _ffn(h2, x1, ffn_w, l, mods[l], ln2_g[l], ln2_b[l], 0, batch * seq, seq, tf)

        ps = _inproj(xs, mods[l], w_in_bf, l, 1, dec_seq)
        ya = _na_attn(ps, dec_batch, dec_seq, cache_a_k, cache_a_v, l, na_bias)
        yb = _diff_lat_attn(ps, dec_batch, dec_seq, cache_b_k, cache_b_v, l, cos, sin, b_lambda[l], b_subln[l], lam_init)
        yc, _ = _mlstm(ps, dec_batch, dec_seq, gate_bias, c_norm[l], state_c_C, state_c_n, state_c_m, l, 1, 0, None)
        x1, h2 = _outproj(ya, yb, yc, w_out_bf, l, xs, mods[l], ln1_g[l], ln1_b[l], 1, dec_seq)
        xs = _ffn(h2, x1, ffn_w, l, mods[l], ln2_g[l], ln2_b[l], 1, dec_seq, dec_seq, tf)

    c_f, n_f, m_f = new_state
    return (xp.reshape(batch, seq, d), xs.reshape(dec_batch, dec_seq, d), *new_kv,
            c_f, n_f[..., 0, :], m_f[..., 0, 0])
```

```python
import functools
import math

import jax
import jax.numpy as jnp
import numpy as np
from jax import lax
from jax.experimental import pallas as pl
from jax.experimental.pallas import tpu as pltpu

F32 = jnp.float32
BF16 = jnp.bfloat16

DEPTH = 2
GRID_W = 64
A_HEADS = 6
NA_ROWS = 8
NA_COLS = 16
NA_QROWS = 4
B_HEADS = 5
B_QK = 64
C_HEADS = 5
HEAD_DIM = 128
CONV_W = 3
ROPE_BASE = 10000.0
LN_EPS = 1e-5
RMS_EPS = 1e-6
ALPHA = (2 * DEPTH) ** 0.25
NEG_LOG2_E = -math.log2(math.e)

LANE = 128
SUBLANE = 8
BF16_ROWS = 16
VMEM_LIMIT_BYTES = 56 * 1024 * 1024

MLSTM_CHUNK = 256
FFN_ROW_CHUNK = 256
OUT_ROW_CHUNK = 128
HI = lax.Precision.HIGHEST


def _tile(n, target, unit=LANE):
    if n <= target:
        return n
    best = unit
    for t in range(unit, target + 1, unit):
        if n % t == 0:
            best = t
    assert n % best == 0, (n, target, unit)
    return best


def _cparams(*sem):
    return pltpu.CompilerParams(dimension_semantics=sem, vmem_limit_bytes=VMEM_LIMIT_BYTES)


def _dot(a, b):
    return jnp.dot(a, b, preferred_element_type=F32)


def _dot_nt(a, b):
    return lax.dot_general(a, b, (((1,), (1,)), ((), ())), preferred_element_type=F32)


def _dot_tn(a, b):
    return lax.dot_general(a, b, (((0,), (0,)), ((), ())), preferred_element_type=F32)


def _mods_kernel(cond_ref, w_ref, b_ref, o_ref):
    c = cond_ref[...]
    s = c / (1.0 + jnp.exp(-c))
    o_ref[0] = _dot(s.astype(BF16), w_ref[0].astype(BF16)) + b_ref[0]


def _mods(cond, w_mod, b_mod):
    depth, d, n = w_mod.shape
    tn = _tile(n, 1536)
    return pl.pallas_call(
        _mods_kernel,
        grid=(depth, n // tn),
        in_specs=[pl.BlockSpec((SUBLANE, d), lambda l, j: (0, 0)),
                  pl.BlockSpec((1, d, tn), lambda l, j: (l, 0, j)),
                  pl.BlockSpec((1, 1, tn), lambda l, j: (l, 0, j))],
        out_specs=pl.BlockSpec((1, SUBLANE, tn), lambda l, j: (l, 0, j)),
        out_shape=jax.ShapeDtypeStruct((depth, SUBLANE, n), F32),
        compiler_params=_cparams("arbitrary", "arbitrary"),
        name="mods",
    )(cond, w_mod, b_mod.reshape(depth, 1, n))


def _inproj_kernel(x_ref, mod_ref, w_ref, o_ref, h_scr):
    @pl.when(pl.program_id(1) == 0)
    def _():
        sh = mod_ref[0, 0:1, :]
        sc = mod_ref[0, 1:2, :]
        h_scr[...] = (x_ref[...] * (1.0 + sc) + sh).astype(BF16)

    o_ref[...] = _dot(h_scr[...], w_ref[...])


def _inproj(x, mods_l, w_bf, layer, row_base, rows_per_cond):
    m, d = x.shape
    n = w_bf.shape[2]
    tm = _tile(m, 1024, SUBLANE)
    tm = min(tm, rows_per_cond)
    tn = _tile(n, 1152)
    per = rows_per_cond // tm
    return pl.pallas_call(
        _inproj_kernel,
        grid=(m // tm, n // tn),
        in_specs=[pl.BlockSpec((tm, d), lambda i, j: (i, 0)),
                  pl.BlockSpec((1, 6, d), lambda i, j: (row_base + i // per, 0, 0)),
                  pl.BlockSpec((None, d, tn), lambda i, j: (layer, 0, j))],
        out_specs=pl.BlockSpec((tm, tn), lambda i, j: (i, j)),
        out_shape=jax.ShapeDtypeStruct((m, n), F32),
        scratch_shapes=[pltpu.VMEM((tm, d), BF16)],
        compiler_params=_cparams("arbitrary", "arbitrary"),
        name="inproj",
    )(x, mods_l, w_bf)


def _lambda_value(lam_ref, lam_init):
    lf = lam_ref[...]
    t1 = jnp.sum(lf[0:1] * lf[1:2], axis=1, keepdims=True)
    t2 = jnp.sum(lf[2:3] * lf[3:4], axis=1, keepdims=True)
    return jnp.exp(t1) - jnp.exp(t2) + lam_init


def _split_maps(q):
    lane = lax.broadcasted_iota(jnp.int32, q.shape, 1)
    first = lane < B_QK
    return jnp.where(first, q, 0.0), jnp.where(first, 0.0, q)


def _subnorm(o, g_ref, lam_init):
    ms = jnp.mean(o * o, axis=-1, keepdims=True)
    return o * lax.rsqrt(ms + RMS_EPS) * g_ref[...] * (1.0 - lam_init)


def _ctx_attn_kernel(*refs, lam_init):
    qa_ref, ka_ref, va_ref = refs[0:3]
    b_refs = refs[3:3 + 3 * B_HEADS]
    lam_ref, subln_ref = refs[3 + 3 * B_HEADS:5 + 3 * B_HEADS]
    ya_ref, yb_ref, nak_ref, nav_ref, nbk_ref, nbv_ref = refs[-6:]
    a_scale = HEAD_DIM ** -0.5
    b_scale = B_QK ** -0.5
    lam = _lambda_value(lam_ref, lam_init)
    head = lambda h: slice(h * HEAD_DIM, (h + 1) * HEAD_DIM)

    def a_scores(h):
        nak_ref[h] = ka_ref[:, head(h)]
        nav_ref[h] = va_ref[:, head(h)]
        return (_dot_nt(qa_ref[:, head(h)].astype(BF16), ka_ref[:, head(h)].astype(BF16)),)

    def a_finish(h, s):
        s = s * a_scale
        e = jnp.exp(s - jnp.max(s, axis=-1, keepdims=True))
        l = jnp.sum(e, axis=-1, keepdims=True)
        ya_ref[:, head(h)] = (_dot(e.astype(BF16), va_ref[:, head(h)].astype(BF16)) / l).astype(ya_ref.dtype)

    def b_scores(h):
        q1, q2 = _split_maps(b_refs[3 * h][...] * b_scale)
        nbk_ref[h] = b_refs[3 * h + 1][...]
        nbv_ref[h] = b_refs[3 * h + 2][...]
        k = b_refs[3 * h + 1][...].astype(BF16)
        return _dot_nt(q1.astype(BF16), k), _dot_nt(q2.astype(BF16), k)

    def b_finish(h, s1, s2):
        e1 = jnp.exp(s1 - jnp.max(s1, axis=-1, keepdims=True))
        e2 = jnp.exp(s2 - jnp.max(s2, axis=-1, keepdims=True))
        p = e1 / jnp.sum(e1, axis=-1, keepdims=True) - lam * (e2 / jnp.sum(e2, axis=-1, keepdims=True))
        o = _dot(p.astype(BF16), b_refs[3 * h + 2][...].astype(BF16))
        yb_ref[:, head(h)] = _subnorm(o, subln_ref, lam_init).astype(yb_ref.dtype)

    stages = [(a_scores, a_finish, h) for h in range(A_HEADS)] + [(b_scores, b_finish, h) for h in range(B_HEADS)]
    ahead = 2
    queue = [stages[j][0](stages[j][2]) for j in range(ahead)]
    for idx, (_, finish, h) in enumerate(stages):
        if idx + ahead < len(stages):
            queue.append(stages[idx + ahead][0](stages[idx + ahead][2]))
        finish(h, *queue.pop(0))


def _ctx_attn(p, batch, seq, b_lambda_l, b_subln_l, lam_init, layer, depth, carried):
    m = p.shape[0]
    aw = A_HEADS * HEAD_DIM
    nb = aw // HEAD_DIM
    in_specs = [pl.BlockSpec((seq, aw), lambda b: (b, 0)),
                pl.BlockSpec((seq, aw), lambda b: (b, 1)),
                pl.BlockSpec((seq, aw), lambda b: (b, 2))]
    args = [p, p, p]
    for h in range(B_HEADS):
        for seg in range(3):
            col = 3 * nb + seg * B_HEADS + h
            in_specs.append(pl.BlockSpec((seq, HEAD_DIM), functools.partial(lambda b, c: (b, c), c=col)))
            args.append(p)
    in_specs += [pl.BlockSpec((4, B_QK), lambda b: (0, 0)), pl.BlockSpec((1, HEAD_DIM), lambda b: (0, 0))]
    args += [b_lambda_l, b_subln_l.reshape(1, HEAD_DIM)]
    aliases = {}
    if carried is not None:
        for j, buf in enumerate(carried):
            aliases[len(args)] = 2 + j
            in_specs.append(pl.BlockSpec(memory_space=pl.ANY))
            args.append(buf)
    kv_spec = lambda nh: pl.BlockSpec((None, None, nh, seq, HEAD_DIM), lambda b: (b, layer, 0, 0, 0))
    kv_shape = lambda nh: jax.ShapeDtypeStruct((batch, depth, nh, seq, HEAD_DIM), F32)
    out = pl.pallas_call(
        functools.partial(_ctx_attn_kernel, lam_init=lam_init),
        grid=(batch,),
        in_specs=in_specs,
        out_specs=[pl.BlockSpec((seq, aw), lambda b: (b, 0)),
                   pl.BlockSpec((seq, B_HEADS * HEAD_DIM), lambda b: (b, 0)),
                   kv_spec(A_HEADS), kv_spec(A_HEADS), kv_spec(B_HEADS), kv_spec(B_HEADS)],
        out_shape=[jax.ShapeDtypeStruct((m, aw), BF16),
                   jax.ShapeDtypeStruct((m, B_HEADS * HEAD_DIM), BF16),
                   kv_shape(A_HEADS), kv_shape(A_HEADS), kv_shape(B_HEADS), kv_shape(B_HEADS)],
        input_output_aliases=aliases,
        compiler_params=_cparams("arbitrary"),
        name="ctx_attn",
    )(*args)
    return out[0], out[1], tuple(out[2:])


def _na_plan(n_rows):
    wr = min(NA_ROWS, n_rows)
    qrows = NA_QROWS
    uw = -(-(wr + qrows - 1) // 2) * 2
    if n_rows % qrows or n_rows < uw:
        qrows, uw = 1, wr
    rs = lambda r: int(np.clip(r - wr // 2, 0, n_rows - wr))
    starts, var_of_block, variants = [], [], []
    for blk in range(n_rows // qrows):
        q0 = blk * qrows
        ws = int(np.clip(rs(q0), 0, n_rows - uw))
        valid = tuple(tuple(rs(q0 + i) <= ws + j < rs(q0 + i) + wr for j in range(uw)) for i in range(qrows))
        var = (q0 - ws, valid)
        if var not in variants:
            variants.append(var)
        starts.append(ws)
        var_of_block.append(variants.index(var))
    return qrows, uw, starts, var_of_block, variants


def _na_bias_kernel(toe_ref, o_ref, *, n_rows):
    qrows, uw, _, _, variants = _na_plan(n_rows)
    w = GRID_W
    neg = jnp.full((w, w), -jnp.inf, F32)
    for v, (off, valid) in enumerate(variants):
        for i in range(qrows):
            for j in range(uw):
                dr = NA_ROWS - 1 + j - off - i
                tile = toe_ref[dr] if valid[i][j] else neg
                o_ref[v, i * w:(i + 1) * w, j * w:(j + 1) * w] = tile


def _na_bias_table(a_rpb, n_rows):
    depth, heads, nr, _ = a_rpb.shape
    qrows, uw, _, _, variants = _na_plan(n_rows)
    cols = np.arange(GRID_W)
    cs = np.clip(cols - NA_COLS // 2, 0, GRID_W - NA_COLS)
    col_mask = (cols[None, :] >= cs[:, None]) & (cols[None, :] < cs[:, None] + NA_COLS)
    col_idx = np.clip(cols[None, :] - cols[:, None] + NA_COLS - 1, 0, 2 * NA_COLS - 2)
    onehot = (col_idx[None] == np.arange(2 * NA_COLS - 1)[:, None, None]).astype(np.float32)
    toe = jnp.einsum('lhrd,dqk->lhrqk', a_rpb.astype(F32), jnp.asarray(onehot), precision=HI)
    toe = jnp.where(jnp.asarray(col_mask), toe, -jnp.inf)
    shape = (len(variants), qrows * GRID_W, uw * GRID_W)
    return pl.pallas_call(
        functools.partial(_na_bias_kernel, n_rows=n_rows),
        grid=(depth, heads),
        in_specs=[pl.BlockSpec((None, None, nr, GRID_W, GRID_W), lambda l, h: (l, h, 0, 0, 0))],
        out_specs=pl.BlockSpec((None, None) + shape, lambda l, h: (l, h, 0, 0, 0)),
        out_shape=jax.ShapeDtypeStruct((depth, heads) + shape, F32),
        compiler_params=_cparams("arbitrary", "arbitrary"),
        name="na_bias",
    )(toe)


def _select_by_block(blk, values):
    out = jnp.int32(values[0])
    for k in range(1, len(values)):
        out = jnp.where(blk == k, jnp.int32(values[k]), out)
    return out


def _na_kernel(q_ref, k_ref, v_ref, kc_ref, vc_ref, bias_ref, o_ref, kb_scr, vb_scr, kcb_scr, vcb_scr, *, n_rows):
    scale = HEAD_DIM ** -0.5
    qrows, uw, starts, var_of_block, _ = _na_plan(n_rows)
    nq = qrows * GRID_W
    kb_scr[...] = k_ref[...].astype(BF16)
    vb_scr[...] = v_ref[...].astype(BF16)
    kcb_scr[...] = kc_ref[...].astype(BF16)
    vcb_scr[...] = vc_ref[...].astype(BF16)

    def scores(blk):
        q0 = pl.multiple_of(blk * nq, nq)
        q = q_ref[pl.ds(q0, nq), :].astype(BF16)
        w0 = pl.multiple_of(_select_by_block(blk, starts) * GRID_W, GRID_W)
        s_nb = _dot_nt(q, kb_scr[pl.ds(w0, uw * GRID_W), :])
        s_c = _dot_nt(q, kcb_scr[...])
        return q0, w0, s_nb, s_c

    def finish(blk, q0, w0, s_nb, s_c):
        s_nb = s_nb * scale + bias_ref[_select_by_block(blk, var_of_block)]
        s_c = s_c * scale
        mx = jnp.maximum(jnp.max(s_nb, axis=-1, keepdims=True), jnp.max(s_c, axis=-1, keepdims=True))
        p_nb = jnp.exp(s_nb - mx)
        p_c = jnp.exp(s_c - mx)
        l = jnp.sum(p_nb, axis=-1, keepdims=True) + jnp.sum(p_c, axis=-1, keepdims=True)
        vw = vb_scr[pl.ds(w0, uw * GRID_W), :]
        o = (_dot(p_nb.astype(BF16), vw) + _dot(p_c.astype(BF16), vcb_scr[...])) / l
        o_ref[pl.ds(q0, nq), :] = o.astype(o_ref.dtype)

    n_blocks = n_rows // qrows
    group = 4 if n_blocks % 4 == 0 else (2 if n_blocks % 2 == 0 else 1)

    def body(j, carry):
        blks = [j * group + g for g in range(group)]
        pending = [scores(b) for b in blks]
        for b, s in zip(blks, pending):
            finish(b, *s)
        return carry

    lax.fori_loop(0, n_blocks // group, body, 0)


def _na_attn(p, batch, seq, cache_k, cache_v, layer, bias):
    n_rows = seq // GRID_W
    past = cache_k.shape[3]
    blk = lambda off: pl.BlockSpec((seq, HEAD_DIM), lambda b, h: (b, off + h))
    cspec = pl.BlockSpec((None, None, None, past, HEAD_DIM), lambda b, h: (b, layer, h, 0, 0))
    return pl.pallas_call(
        functools.partial(_na_kernel, n_rows=n_rows),
        grid=(batch, A_HEADS),
        in_specs=[blk(0), blk(A_HEADS), blk(2 * A_HEADS), cspec, cspec,
                  pl.BlockSpec((None, None) + bias.shape[2:], lambda b, h: (layer, h, 0, 0, 0))],
        out_specs=pl.BlockSpec((seq, HEAD_DIM), lambda b, h: (b, h)),
        out_shape=jax.ShapeDtypeStruct((batch * seq, A_HEADS * HEAD_DIM), BF16),
        scratch_shapes=[pltpu.VMEM((seq, HEAD_DIM), BF16), pltpu.VMEM((seq, HEAD_DIM), BF16),
                        pltpu.VMEM((past, HEAD_DIM), BF16), pltpu.VMEM((past, HEAD_DIM), BF16)],
        compiler_params=_cparams("arbitrary", "arbitrary"),
        name="na_attn",
    )(p, p, p, cache_k, cache_v, bias)


def _rope_tables(seq):
    t = np.arange(seq)
    rows = (t // GRID_W).astype(np.float32)
    cols = (t % GRID_W).astype(np.float32)
    half = B_QK // 2
    freqs = jnp.asarray(ROPE_BASE, F32) ** (-jnp.arange(0, half, 2, dtype=F32) / half)
    ar = jnp.asarray(rows)[:, None] * freqs
    ac = jnp.asarray(cols)[:, None] * freqs
    cr, sr, cc, sc = jnp.cos(ar), jnp.sin(ar), jnp.cos(ac), jnp.sin(ac)
    cos = jnp.concatenate([cr, cr, cc, cc] * 2, axis=-1)
    sin = jnp.concatenate([-sr, sr, -sc, sc] * 2, axis=-1)
    return cos, sin


def _rope(x, cos, sin):
    q = B_QK // 4
    lane = lax.broadcasted_iota(jnp.int32, x.shape, 1)
    first = (lane & (2 * q - 1)) < q
    partner = jnp.where(first, pltpu.roll(x, LANE - q, 1), pltpu.roll(x, q, 1))
    return x * cos + partner * sin


def _diff_lat_kernel(q_ref, k_ref, v_ref, kc_ref, vc_ref, cosq_ref, sinq_ref, cosk_ref, sink_ref, lam_ref, subln_ref,
                     o_ref, kr_scr, kcb_scr, vb_scr, vcb_scr, *, lam_init):
    scale = B_QK ** -0.5

    @pl.when(pl.program_id(2) == 0)
    def _():
        kr_scr[...] = _rope(k_ref[...], cosk_ref[...], sink_ref[...]).astype(BF16)
        kcb_scr[...] = kc_ref[...].astype(BF16)
        vb_scr[...] = v_ref[...].astype(BF16)
        vcb_scr[...] = vc_ref[...].astype(BF16)

    lam = _lambda_value(lam_ref, lam_init)
    q = q_ref[...] * scale
    qr1, qr2 = _split_maps(_rope(q, cosq_ref[...], sinq_ref[...]))
    q1, q2 = _split_maps(q)

    def probs(qr, qp):
        s_l = _dot_nt(qr.astype(BF16), kr_scr[...])
        s_c = _dot_nt(qp.astype(BF16), kcb_scr[...])
        mx = jnp.maximum(jnp.max(s_l, axis=-1, keepdims=True), jnp.max(s_c, axis=-1, keepdims=True))
        e_l = jnp.exp(s_l - mx)
        e_c = jnp.exp(s_c - mx)
        inv = 1.0 / (jnp.sum(e_l, axis=-1, keepdims=True) + jnp.sum(e_c, axis=-1, keepdims=True))
        return e_l * inv, e_c * inv

    p1_l, p1_c = probs(qr1, q1)
    p2_l, p2_c = probs(qr2, q2)
    o = (_dot((p1_l - lam * p2_l).astype(BF16), vb_scr[...])
         + _dot((p1_c - lam * p2_c).astype(BF16), vcb_scr[...]))
    o_ref[...] = _subnorm(o, subln_ref, lam_init).astype(o_ref.dtype)


def _diff_lat_attn(p, batch, seq, cache_k, cache_v, layer, cos, sin, b_lambda_l, b_subln_l, lam_init):
    past = cache_k.shape[3]
    tq = _tile(seq, 256, SUBLANE)
    nq = seq // tq
    base = 3 * A_HEADS
    cspec = pl.BlockSpec((None, None, None, past, HEAD_DIM), lambda b, h, i: (b, layer, h, 0, 0))
    kv = lambda off: pl.BlockSpec((seq, HEAD_DIM), lambda b, h, i: (b, off + h))
    return pl.pallas_call(
        functools.partial(_diff_lat_kernel, lam_init=lam_init),
        grid=(batch, B_HEADS, nq),
        in_specs=[pl.BlockSpec((tq, HEAD_DIM), lambda b, h, i: (b * nq + i, base + h)),
                  kv(base + B_HEADS), kv(base + 2 * B_HEADS), cspec, cspec,
                  pl.BlockSpec((tq, HEAD_DIM), lambda b, h, i: (i, 0)),
                  pl.BlockSpec((tq, HEAD_DIM), lambda b, h, i: (i, 0)),
                  pl.BlockSpec((seq, HEAD_DIM), lambda b, h, i: (0, 0)),
                  pl.BlockSpec((seq, HEAD_DIM), lambda b, h, i: (0, 0)),
                  pl.BlockSpec((4, B_QK), lambda b, h, i: (0, 0)),
                  pl.BlockSpec((1, HEAD_DIM), lambda b, h, i: (0, 0))],
        out_specs=pl.BlockSpec((tq, HEAD_DIM), lambda b, h, i: (b * nq + i, h)),
        out_shape=jax.ShapeDtypeStruct((batch * seq, B_HEADS * HEAD_DIM), BF16),
        scratch_shapes=[pltpu.VMEM((seq, HEAD_DIM), BF16), pltpu.VMEM((past, HEAD_DIM), BF16),
                        pltpu.VMEM((seq, HEAD_DIM), BF16), pltpu.VMEM((past, HEAD_DIM), BF16)],
        compiler_params=_cparams("arbitrary", "arbitrary", "arbitrary"),
        name="diff_lat_attn",
    )(p, p, p, cache_k, cache_v, cos, sin, cos, sin, b_lambda_l, b_subln_l.reshape(1, HEAD_DIM))


def _log_sigmoid(x):
    return jnp.minimum(x, 0.0) - jnp.log1p(jnp.exp(-jnp.abs(x)))


def _split3(x):
    hi = x.astype(BF16)
    r = x - hi.astype(F32)
    mid = r.astype(BF16)
    lo = (r - mid.astype(F32)).astype(BF16)
    return hi, mid, lo


def _mlstm_kernel(*refs, chunk, n_chunks, zero_init):
    nh = C_HEADS
    q_refs, k_refs, v_refs, og_refs = (refs[i * nh:(i + 1) * nh] for i in range(4))
    g_ref, gb_ref, cn_ref = refs[4 * nh:4 * nh + 3]
    if not zero_init:
        c0_ref, n0_ref, m0_ref = refs[4 * nh + 3:4 * nh + 6]
    y_ref, cf_ref, nf_ref, mf_ref, hf_scr, hb_scr = refs[-6:]
    scale = HEAD_DIM ** -0.5
    ln = chunk
    n_sel = 4 * SUBLANE
    assert 4 * nh <= n_sel
    sel = jnp.where(lax.broadcasted_iota(jnp.int32, (n_sel, LANE), 0)
                    == lax.broadcasted_iota(jnp.int32, (n_sel, LANE), 1), 1.0, 0.0).astype(BF16)
    ti = lax.broadcasted_iota(jnp.int32, (ln, ln), 0)
    si = lax.broadcasted_iota(jnp.int32, (ln, ln), 1)
    lower = si <= ti
    upper = si >= ti
    lower_b = jnp.where(lower, 1.0, 0.0).astype(BF16)
    upper_b = jnp.where(upper, 1.0, 0.0).astype(BF16)

    def gate_terms(r0, directions):
        g = g_ref[pl.ds(r0, ln), :] + gb_ref[...]
        gp = _split3(g)
        rows = sum(_dot_nt(sel, p) for p in gp)
        cols = sum(_dot_nt(p, sel) for p in gp)
        lf_rows = _split3(_log_sigmoid(rows))
        lf_cols = _split3(_log_sigmoid(cols))
        t = dict(rows=rows, cols=cols)
        if 0 in directions:
            t["row0"] = sum(_dot(p, upper_b) for p in lf_rows)
            t["col0"] = sum(_dot(lower_b, p) for p in lf_cols)
        if 1 in directions:
            t["row1"] = sum(_dot(p, lower_b) for p in lf_rows)
            t["col1"] = sum(_dot(upper_b, p) for p in lf_cols)
        return t

    def head_terms(h, r0):
        qf = q_refs[h][pl.ds(r0, ln), :] * scale
        kf = k_refs[h][pl.ds(r0, ln), :]
        qb = qf.astype(BF16)
        vb = v_refs[h][pl.ds(r0, ln), :].astype(BF16)
        return dict(qf=qf, kf=kf, qb=qb, vb=vb, qk=_dot_nt(qb, kf.astype(BF16)))

    def chunk_step(gt, t, h, state, backward):
        rows, cols, qf, kf, qb, vb = gt["rows"], gt["cols"], t["qf"], t["kf"], t["qb"], t["vb"]
        gi = (2 * nh if backward else 0) + h
        gf = gi + nh
        i_row, i_col = rows[gi:gi + 1], cols[:, gi:gi + 1]
        if backward:
            b_row, b_col, mask = gt["row1"][gf:gf + 1], gt["col1"][:, gf:gf + 1], upper
            b_last = b_col[0:1]
        else:
            b_row, b_col, mask = gt["row0"][gf:gf + 1], gt["col0"][:, gf:gf + 1], lower
            b_last = b_col[ln - 1:ln]
        d = jnp.where(mask, b_col - b_row + i_row, -jnp.inf)
        gg = b_last - b_col + i_col
        if state is None:
            m_row = jnp.maximum(jnp.max(d, axis=-1, keepdims=True), b_col)
            sc = t["qk"] * jnp.exp(d - m_row)
            num = _dot(sc.astype(BF16), vb)
            den = jnp.sum(sc, axis=-1, keepdims=True)
            m_new = jnp.maximum(b_last, jnp.max(gg, axis=0, keepdims=True))
            kw = kf * jnp.exp(gg - m_new)
            c_new = _dot_tn(kw.astype(BF16), vb)
            n_new = jnp.sum(kw, axis=0, keepdims=True)
        else:
            c_st, n_st, m_st = state
            inter = b_col + m_st
            m_row = jnp.maximum(jnp.max(d, axis=-1, keepdims=True), inter)
            w_state = jnp.exp(inter - m_row)
            sc = t["qk"] * jnp.exp(d - m_row)
            num = w_state * _dot(qb, c_st.astype(BF16)) + _dot(sc.astype(BF16), vb)
            den = w_state * jnp.sum(qf * n_st, axis=-1, keepdims=True) + jnp.sum(sc, axis=-1, keepdims=True)
            m_new = jnp.maximum(b_last + m_st, jnp.max(gg, axis=0, keepdims=True))
            w_old = jnp.exp(b_last + m_st - m_new)
            kw = kf * jnp.exp(gg - m_new)
            c_new = w_old * c_st + _dot_tn(kw.astype(BF16), vb)
            n_new = w_old * n_st + jnp.sum(kw, axis=0, keepdims=True)
        h_out = num / jnp.maximum(jnp.abs(den), jnp.exp(-m_row))
        return h_out, (c_new, n_new, m_new)

    def start(c):
        return pl.multiple_of(c * ln, ln)

    def finish(h, r0, hs):
        ms = jnp.mean(hs * hs, axis=-1, keepdims=True)
        hn = hs * lax.rsqrt(ms + RMS_EPS) * cn_ref[...]
        og = og_refs[h][pl.ds(r0, ln), :]
        y_ref[pl.ds(r0, ln), h * HEAD_DIM:(h + 1) * HEAD_DIM] = (hn / (1.0 + jnp.exp(-og))).astype(y_ref.dtype)

    if zero_init:
        assert n_chunks == 1
        init = ((None,) * nh,) * 2
    else:
        init = tuple(tuple((c0_ref[dr, h], n0_ref[dr, h], m0_ref[dr, h][:, 0:1]) for h in range(nh))
                     for dr in (0, 1))
    if n_chunks == 1:
        gt = gate_terms(0, (0, 1))
        st_f, st_b = [], []
        for h in range(nh):
            t = head_terms(h, 0)
            h_f, s_f = chunk_step(gt, t, h, init[0][h], False)
            h_b, s_b = chunk_step(gt, t, h, init[1][h], True)
            finish(h, 0, h_f + h_b)
            st_f.append(s_f)
            st_b.append(s_b)
        final = (st_f, st_b)
    else:
        def body(j, states):
            r_f = start(j)
            r_b = start(n_chunks - 1 - j)
            gt_f = gate_terms(r_f, (0,))
            gt_b = gate_terms(r_b, (1,))
            terms_f = [head_terms(h, r_f) for h in range(nh)]
            terms_b = [head_terms(h, r_b) for h in range(nh)]
            st_f, st_b = [], []
            for h in range(nh):
                sl = slice(h * HEAD_DIM, (h + 1) * HEAD_DIM)
                h_f, s_f = chunk_step(gt_f, terms_f[h], h, states[0][h], False)
                h_b, s_b = chunk_step(gt_b, terms_b[h], h, states[1][h], True)
                hf_scr[pl.ds(r_f, ln), sl] = h_f
                hb_scr[pl.ds(r_b, ln), sl] = h_b
                st_f.append(s_f)
                st_b.append(s_b)
            return tuple(st_f), tuple(st_b)

        final = lax.fori_loop(0, n_chunks, body, init)

        def fin_body(j, carry):
            r0 = start(j)
            for h in range(nh):
                sl = slice(h * HEAD_DIM, (h + 1) * HEAD_DIM)
                finish(h, r0, hf_scr[pl.ds(r0, ln), sl] + hb_scr[pl.ds(r0, ln), sl])
            return carry

        lax.fori_loop(0, n_chunks, fin_body, 0)
    for dr in (0, 1):
        for h in range(nh):
            c_f, n_f, m_f = final[dr][h]
            cf_ref[dr, h] = c_f
            nf_ref[dr, h] = n_f
            mf_ref[dr, h] = jnp.broadcast_to(m_f, (1, LANE))


def _mlstm(p, batch, seq, gate_bias, c_norm_l, c0, n0, m0, lin, out_depth, lout, carried):
    chunk = min(MLSTM_CHUNK, seq)
    n_chunks = seq // chunk
    base = 3 * A_HEADS + 3 * B_HEADS
    hd = HEAD_DIM
    nh = C_HEADS
    head_bytes = 4 * nh * seq * hd * 4
    mode = dict(pipeline_mode=pl.Buffered(1)) if 2 * head_bytes > VMEM_LIMIT_BYTES // 2 else {}
    blk = lambda col: pl.BlockSpec((seq, hd), functools.partial(lambda b, c: (b, c), c=col), **mode)
    st_c = lambda l: pl.BlockSpec((None, None, 2, nh, hd, hd), lambda b: (b, l, 0, 0, 0, 0))
    st_n = lambda l: pl.BlockSpec((None, None, 2, nh, 1, hd), lambda b: (b, l, 0, 0, 0, 0))
    if c0 is None and n_chunks > 1:
        c0 = jnp.zeros((batch, 1, 2, nh, hd, hd), F32)
        n0 = jnp.zeros((batch, 1, 2, nh, hd), F32)
        m0 = jnp.zeros((batch, 1, 2, nh), F32)
    zero_init = c0 is None
    in_specs = ([blk(base + seg * nh + h) for seg in range(4) for h in range(nh)]
                + [pl.BlockSpec((seq, LANE), lambda b: (b, base + 4 * nh)),
                   pl.BlockSpec((1, LANE), lambda b: (0, 0)),
                   pl.BlockSpec((1, hd), lambda b: (0, 0))])
    args = [p] * (4 * nh + 1) + [gate_bias, c_norm_l.reshape(1, hd)]
    if not zero_init:
        in_specs += [st_c(lin), st_n(lin), st_n(lin)]
        args += [c0, n0.reshape(n0.shape[:4] + (1, hd)), jnp.broadcast_to(m0[..., None, None], m0.shape + (1, LANE))]
    aliases = {}
    if carried is not None:
        for j, buf in enumerate(carried):
            aliases[len(args)] = 1 + j
            in_specs.append(pl.BlockSpec(memory_space=pl.ANY))
            args.append(buf)
    out = pl.pallas_call(
        functools.partial(_mlstm_kernel, chunk=chunk, n_chunks=n_chunks, zero_init=zero_init),
        grid=(batch,),
        in_specs=in_specs,
        out_specs=[pl.BlockSpec((seq, nh * hd), lambda b: (b, 0)), st_c(lout), st_n(lout), st_n(lout)],
        out_shape=[jax.ShapeDtypeStruct((batch * seq, nh * hd), BF16),
                   jax.ShapeDtypeStruct((batch, out_depth, 2, nh, hd, hd), F32),
                   jax.ShapeDtypeStruct((batch, out_depth, 2, nh, 1, hd), F32),
                   jax.ShapeDtypeStruct((batch, out_depth, 2, nh, 1, LANE), F32)],
        scratch_shapes=[pltpu.VMEM((seq, nh * hd), F32), pltpu.VMEM((seq, nh * hd), F32)],
        input_output_aliases=aliases,
        compiler_params=_cparams("arbitrary"),
        name="mlstm",
    )(*args)
    return out[0], tuple(out[1:])


def _layernorm(z, g_ref, b_ref):
    mu = jnp.mean(z, axis=-1, keepdims=True)
    zc = z - mu
    var = jnp.mean(zc * zc, axis=-1, keepdims=True)
    return zc * lax.rsqrt(var + LN_EPS) * g_ref[...] + b_ref[...]


def _outproj_kernel(ya_ref, yb_ref, yc_ref, w_ref, x_ref, mod_ref, g_ref, b_ref, x1_ref, h2_ref):
    tm = x_ref.shape[0]
    rc = min(tm, OUT_ROW_CHUNK)

    def proj(c):
        rows = slice(c * rc, (c + 1) * rc)
        ycat = jnp.concatenate([ya_ref[rows, :], yb_ref[rows, :], yc_ref[rows, :]], axis=-1)
        return _dot(ycat, w_ref[...])

    def finish(c, y):
        rows = slice(c * rc, (c + 1) * rc)
        x1 = _layernorm(ALPHA * x_ref[rows, :] + mod_ref[0, 2:3, :] * y, g_ref, b_ref)
        x1_ref[rows, :] = x1
        h2_ref[rows, :] = (x1 * (1.0 + mod_ref[0, 4:5, :]) + mod_ref[0, 3:4, :]).astype(BF16)

    y = proj(0)
    for c in range(tm // rc):
        y_next = proj(c + 1) if c + 1 < tm // rc else None
        finish(c, y)
        y = y_next


def _outproj(ya, yb, yc, w_bf, layer, x, mods_l, ln_g, ln_b, row_base, rows_per_cond):
    m, d = x.shape
    tm = min(_tile(m, 512, SUBLANE), rows_per_cond)
    per = rows_per_cond // tm
    row = lambda w: pl.BlockSpec((tm, w), lambda i: (i, 0))
    vec = pl.BlockSpec((1, d), lambda i: (0, 0))
    return pl.pallas_call(
        _outproj_kernel,
        grid=(m // tm,),
        in_specs=[row(ya.shape[1]), row(yb.shape[1]), row(yc.shape[1]),
                  pl.BlockSpec((None, d, d), lambda i: (layer, 0, 0)), row(d),
                  pl.BlockSpec((1, 6, d), lambda i: (row_base + i // per, 0, 0)), vec, vec],
        out_specs=[row(d), row(d)],
        out_shape=[jax.ShapeDtypeStruct((m, d), F32), jax.ShapeDtypeStruct((m, d), BF16)],
        compiler_params=_cparams("arbitrary"),
        name="outproj_ln",
    )(ya, yb, yc, w_bf, x, mods_l, ln_g.reshape(1, d), ln_b.reshape(1, d))


def _ffn_kernel(h_ref, hp_ref, hn_ref, wa_ref, wg_ref, cwa_ref, cwg_ref, cba_ref, cbg_ref, wd_ref, x_ref, mod_ref,
                g_ref, b_ref, o_ref, hext_scr, acc_scr, ua_scr, ug_scr, *, seq):
    i = pl.program_id(0)
    f = pl.program_id(1)
    tm = h_ref.shape[0]
    halo = BF16_ROWS

    @pl.when(f == 0)
    def _():
        hext_scr[0:halo, :] = hp_ref[...]
        hext_scr[halo:halo + tm, :] = h_ref[...]
        hext_scr[halo + tm:, :] = hn_ref[...]
        acc_scr[...] = jnp.zeros_like(acc_scr)

    period = min(seq, tm)
    grp = lax.broadcasted_iota(jnp.int32, (SUBLANE, 1), 0)

    rc = min(tm, FFN_ROW_CHUNK)
    ext = tm + 2 * halo
    up_bounds = [0] + [min(ext, -(-(halo + (k + 1) * rc + 1) // BF16_ROWS) * BF16_ROWS) for k in range(tm // rc)]
    up_bounds[-1] = ext

    def up_chunk(k):
        lo, hi = up_bounds[k], up_bounds[k + 1]
        ua_scr[lo:hi, :] = _dot(hext_scr[lo:hi, :], wa_ref[...])
        ug_scr[lo:hi, :] = _dot(hext_scr[lo:hi, :], wg_ref[...])

    def mask_group(x, r0, first):
        g0 = 0 if first else rc - SUBLANE
        pos = (i * tm + r0 + g0 + grp) & (seq - 1)
        keep = (pos != 0) if first else (pos != seq - 1)
        masked = jnp.where(keep, x[g0:g0 + SUBLANE], 0.0)
        return jnp.concatenate([masked, x[SUBLANE:]] if first else [x[:g0], masked], axis=0)

    def down_chunk(k):
        r0 = k * rc

        def conv(u_scr, cw_ref, cb_ref):
            u_prev = u_scr[halo - 1 + r0:halo - 1 + r0 + rc, :]
            u_next = u_scr[halo + 1 + r0:halo + 1 + r0 + rc, :]
            if r0 % period == 0:
                u_prev = mask_group(u_prev, r0, True)
            if (r0 + rc) % period == 0:
                u_next = mask_group(u_next, r0, False)
            return (u_prev * cw_ref[0:1, :] + u_scr[halo + r0:halo + r0 + rc, :] * cw_ref[1:2, :]
                    + u_next * cw_ref[2:3, :] + cb_ref[...])

        a = conv(ua_scr, cwa_ref, cba_ref)
        g = conv(ug_scr, cwg_ref, cbg_ref)
        act = (g / (1.0 + jnp.exp2(g * NEG_LOG2_E))) * a
        acc_scr[r0:r0 + rc, :] += _dot(act.astype(BF16), wd_ref[...])

    up_chunk(0)
    for k in range(tm // rc):
        if k + 1 < tm // rc:
            up_chunk(k + 1)
        down_chunk(k)

    @pl.when(f == pl.num_programs(1) - 1)
    def _():
        z = ALPHA * x_ref[...] + mod_ref[0, 5:6, :] * acc_scr[...]
        o_ref[...] = _layernorm(z, g_ref, b_ref)


def _ffn(h2, x1, ffn_w, layer, mods_l, ln_g, ln_b, row_base, rows_per_cond, seq, tf):
    wu_bf, cw, cb, wd_bf = ffn_w
    m, d = x1.shape
    fp = wd_bf.shape[1]
    tm = min(_tile(m, 512, SUBLANE), rows_per_cond)
    per = rows_per_cond // tm
    assert seq & (seq - 1) == 0 and m % seq == 0
    nh = m // BF16_ROWS
    hb = tm // BF16_ROWS
    vec = pl.BlockSpec((1, d), lambda i, f: (0, 0))
    wcol = lambda rows, half: pl.BlockSpec((None, None, rows, tf),
                                           functools.partial(lambda i, f, hf: (layer, hf, 0, f), hf=half))
    return pl.pallas_call(
        functools.partial(_ffn_kernel, seq=seq),
        grid=(m // tm, fp // tf),
        in_specs=[pl.BlockSpec((tm, d), lambda i, f: (i, 0)),
                  pl.BlockSpec((BF16_ROWS, d), lambda i, f: (jnp.maximum(i * hb - 1, 0), 0)),
                  pl.BlockSpec((BF16_ROWS, d), lambda i, f: (jnp.minimum((i + 1) * hb, nh - 1), 0)),
                  wcol(d, 0), wcol(d, 1), wcol(CONV_W, 0), wcol(CONV_W, 1), wcol(1, 0), wcol(1, 1),
                  pl.BlockSpec((None, tf, d), lambda i, f: (layer, f, 0)),
                  pl.BlockSpec((tm, d), lambda i, f: (i, 0)),
                  pl.BlockSpec((1, 6, d), lambda i, f: (row_base + i // per, 0, 0)), vec, vec],
        out_specs=pl.BlockSpec((tm, d), lambda i, f: (i, 0)),
        out_shape=jax.ShapeDtypeStruct((m, d), F32),
        scratch_shapes=[pltpu.VMEM((tm + 2 * BF16_ROWS, d), BF16), pltpu.VMEM((tm, d), F32),
                        pltpu.VMEM((tm + 2 * BF16_ROWS, tf), F32), pltpu.VMEM((tm + 2 * BF16_ROWS, tf), F32)],
        compiler_params=_cparams("arbitrary", "arbitrary"),
        name="ffn_ln",
    )(h2, h2, h2, wu_bf, wu_bf, cw, cw, cb, cb, wd_bf, x1, mods_l, ln_g.reshape(1, d), ln_b.reshape(1, d))


def _pad_cols(w, n):
    return jnp.pad(w, [(0, 0)] * (w.ndim - 1) + [(0, n - w.shape[-1])])


def _cast_kernel(x_ref, o_ref, *, rows_valid, cols_valid):
    tr, tc = x_ref.shape
    x = x_ref[...]
    if rows_valid is not None:
        r = pl.program_id(1) * tr + lax.broadcasted_iota(jnp.int32, (tr, tc), 0)
        x = jnp.where(r < rows_valid, x, 0.0)
    if cols_valid is not None:
        c = pl.program_id(2) * tc + lax.broadcasted_iota(jnp.int32, (tr, tc), 1)
        x = jnp.where(c < cols_valid, x, 0.0)
    o_ref[:, :tc] = x.astype(BF16)
    if o_ref.shape[1] > tc:
        o_ref[:, tc:] = jnp.zeros((tr, o_ref.shape[1] - tc), BF16)


def _cast_weights(w, tr, tc, rows_out, cols_out):
    depth, r, c = w.shape
    nr, nc = -(-rows_out // tr), -(-cols_out // tc)
    assert nr * tr == rows_out and nc * tc == cols_out
    return pl.pallas_call(
        functools.partial(_cast_kernel, rows_valid=r if rows_out > r else None, cols_valid=c if cols_out > c else None),
        grid=(depth, nr, nc),
        in_specs=[pl.BlockSpec((None, tr, tc), lambda l, i, j: (l, i, j))],
        out_specs=pl.BlockSpec((None, tr, tc), lambda l, i, j: (l, i, j)),
        out_shape=jax.ShapeDtypeStruct((depth, rows_out, cols_out), BF16),
        compiler_params=_cparams("arbitrary", "arbitrary", "arbitrary"),
        name="cast_weights",
    )(w)


def _cast_in_kernel(x_ref, o_ref, *, n_valid):
    tn = x_ref.shape[0]
    col = pl.program_id(0) * tn + lax.broadcasted_iota(jnp.int32, (o_ref.shape[1], tn), 1)
    for l in range(x_ref.shape[1]):
        o_ref[l] = jnp.where(col < n_valid, x_ref[:, l, :].T, 0.0).astype(BF16)


def _cast_in_weights(w_in, n_pad, tn):
    depth, d, n = w_in.shape
    return pl.pallas_call(
        functools.partial(_cast_in_kernel, n_valid=n),
        grid=(n_pad // tn,),
        in_specs=[pl.BlockSpec((tn, depth, d), lambda j: (j, 0, 0))],
        out_specs=pl.BlockSpec((depth, d, tn), lambda j: (0, 0, j)),
        out_shape=jax.ShapeDtypeStruct((depth, d, n_pad), BF16),
        compiler_params=_cparams("arbitrary"),
        name="cast_in_weights",
    )(jnp.transpose(w_in, (2, 0, 1)))


def _cast_up_weights(w_up, dff, fp, tr):
    depth, d, _ = w_up.shape
    return pl.pallas_call(
        functools.partial(_cast_kernel, rows_valid=None, cols_valid=None),
        grid=(depth, d // tr, 2),
        in_specs=[pl.BlockSpec((None, tr, dff), lambda l, i, j: (l, i, j))],
        out_specs=pl.BlockSpec((None, None, tr, fp), lambda l, i, j: (l, j, i, 0)),
        out_shape=jax.ShapeDtypeStruct((depth, 2, d, fp), BF16),
        compiler_params=_cparams("arbitrary", "arbitrary", "arbitrary"),
        name="cast_up_weights",
    )(w_up)


def _prep_weights(w_in, w_out, w_up, conv_w, conv_b, w_down, tf):
    depth, d, n_in = w_in.shape
    n_pad = -(-n_in // LANE) * LANE
    dff = w_down.shape[1]
    fp = -(-dff // tf) * tf
    w_in_bf = _cast_in_weights(w_in, n_pad, _tile(n_pad, 384))
    w_out_bf = _cast_weights(w_out, _tile(d, 1024, SUBLANE), d, d, d)
    w_up_bf = _cast_up_weights(w_up, dff, fp, _tile(d, 512, SUBLANE))
    w_down_bf = _cast_weights(w_down, _tile(fp, 1536, tf), d, fp, d)
    halves = lambda t: jnp.stack([_pad_cols(t[..., :dff], fp), _pad_cols(t[..., dff:], fp)], axis=1)
    return w_in_bf, w_out_bf, w_up_bf, halves(conv_w), halves(conv_b[:, None, :]), w_down_bf


def _gate_bias_row(c_gate_b_l):
    return _pad_cols(c_gate_b_l.reshape(1, -1), LANE)


def kernel(x_prompt, x_sample, cache_a_k, cache_a_v, cache_b_k, cache_b_v, state_c_C, state_c_n, state_c_m, c, c_ctx, w_mod, b_mod, w_in, c_gate_b, a_rpb, b_lambda, b_subln, c_norm, w_out, ln1_g, ln1_b, ln2_g, ln2_b, w_up, conv_w, conv_b, w_down):
    batch, seq, d = x_prompt.shape
    dec_batch, dec_seq, _ = x_sample.shape
    depth = w_in.shape[0]
    tf = 512

    cond = jnp.concatenate([c_ctx[None, :], c], axis=0)
    cond = jnp.pad(cond, ((0, SUBLANE - cond.shape[0]), (0, 0)))
    mods = _mods(cond, w_mod, b_mod).reshape(depth, SUBLANE, 6, d)

    xp = x_prompt.reshape(batch * seq, d)
    xs = x_sample.reshape(dec_batch * dec_seq, d)
    cos, sin = _rope_tables(dec_seq)
    na_bias = _na_bias_table(a_rpb, dec_seq // GRID_W)

    w_in_bf, w_out_bf, *ffn_w = _prep_weights(w_in, w_out, w_up, conv_w, conv_b, w_down, tf)
    new_kv = None
    new_state = None
    for l in range(depth):
        lam_init = 0.8 - 0.6 * math.exp(-0.3 * l)
        gate_bias = _gate_bias_row(c_gate_b[l])

        pp = _inproj(xp, mods[l], w_in_bf, l, 0, batch * seq)
        ya, yb, new_kv = _ctx_attn(pp, batch, seq, b_lambda[l], b_subln[l], lam_init, l, depth, new_kv)
        yc, new_state = _mlstm(pp, batch, seq, gate_bias, c_norm[l], None, None, None, 0, depth, l, new_state)
        x1, h2 = _outproj(ya, yb, yc, w_out_bf, l, xp, mods[l], ln1_g[l], ln1_b[l], 0, batch * seq)
        xp = _ffn(h2, x1, ffn_w, l, mods[l], ln2_g[l], ln2_b[l], 0, batch * seq, seq, tf)

        ps = _inproj(xs, mods[l], w_in_bf, l, 1, dec_seq)
        ya = _na_attn(ps, dec_batch, dec_seq, cache_a_k, cache_a_v, l, na_bias)
        yb = _diff_lat_attn(ps, dec_batch, dec_seq, cache_b_k, cache_b_v, l, cos, sin, b_lambda[l], b_subln[l], lam_init)
        yc, _ = _mlstm(ps, dec_batch, dec_seq, gate_bias, c_norm[l], state_c_C, state_c_n, state_c_m, l, 1, 0, None)
        x1, h2 = _outproj(ya, yb, yc, w_out_bf, l, xs, mods[l], ln1_g[l], ln1_b[l], 1, dec_seq)
        xs = _ffn(h2, x1, ffn_w, l, mods[l], ln2_g[l], ln2_b[l], 1, dec_seq, dec_seq, tf)

    c_f, n_f, m_f = new_state
    return (xp.reshape(batch, seq, d), xs.reshape(dec_batch, dec_seq, d), *new_kv,
            c_f, n_f[..., 0, :], m_f[..., 0, 0])
```

```python
import functools
import math

import jax
import jax.numpy as jnp
import numpy as np
from jax import lax
from jax.experimental import pallas as pl
from jax.experimental.pallas import tpu as pltpu

F32 = jnp.float32
BF16 = jnp.bfloat16

DEPTH = 2
GRID_W = 64
A_HEADS = 6
NA_ROWS = 8
NA_COLS = 16
NA_QROWS = 4
B_HEADS = 5
B_QK = 64
C_HEADS = 5
HEAD_DIM = 128
CONV_W = 3
ROPE_BASE = 10000.0
LN_EPS = 1e-5
RMS_EPS = 1e-6
ALPHA = (2 * DEPTH) ** 0.25
NEG_LOG2_E = -math.log2(math.e)

LANE = 128
SUBLANE = 8
BF16_ROWS = 16
VMEM_LIMIT_BYTES = 56 * 1024 * 1024

MLSTM_CHUNK = 256
FFN_ROW_CHUNK = 256
OUT_ROW_CHUNK = 128
HI = lax.Precision.HIGHEST


def _tile(n, target, unit=LANE):
    if n <= target:
        return n
    best = unit
    for t in range(unit, target + 1, unit):
        if n % t == 0:
            best = t
    assert n % best == 0, (n, target, unit)
    return best


def _cparams(*sem):
    return pltpu.CompilerParams(dimension_semantics=sem, vmem_limit_bytes=VMEM_LIMIT_BYTES)


def _dot(a, b):
    return jnp.dot(a, b, preferred_element_type=F32)


def _dot_nt(a, b):
    return lax.dot_general(a, b, (((1,), (1,)), ((), ())), preferred_element_type=F32)


def _dot_tn(a, b):
    return lax.dot_general(a, b, (((0,), (0,)), ((), ())), preferred_element_type=F32)


def _mods_kernel(cond_ref, w_ref, b_ref, o_ref):
    c = cond_ref[...]
    s = c / (1.0 + jnp.exp(-c))
    o_ref[0] = _dot(s.astype(BF16), w_ref[0].astype(BF16)) + b_ref[0]


def _mods(cond, w_mod, b_mod):
    depth, d, n = w_mod.shape
    tn = _tile(n, 1536)
    return pl.pallas_call(
        _mods_kernel,
        grid=(depth, n // tn),
        in_specs=[pl.BlockSpec((SUBLANE, d), lambda l, j: (0, 0)),
                  pl.BlockSpec((1, d, tn), lambda l, j: (l, 0, j)),
                  pl.BlockSpec((1, 1, tn), lambda l, j: (l, 0, j))],
        out_specs=pl.BlockSpec((1, SUBLANE, tn), lambda l, j: (l, 0, j)),
        out_shape=jax.ShapeDtypeStruct((depth, SUBLANE, n), F32),
        compiler_params=_cparams("arbitrary", "arbitrary"),
        name="mods",
    )(cond, w_mod, b_mod.reshape(depth, 1, n))


def _inproj_kernel(x_ref, mod_ref, w_ref, o_ref, h_scr):
    @pl.when(pl.program_id(1) == 0)
    def _():
        sh = mod_ref[0, 0:1, :]
        sc = mod_ref[0, 1:2, :]
        h_scr[...] = (x_ref[...] * (1.0 + sc) + sh).astype(BF16)

    o_ref[...] = _dot(h_scr[...], w_ref[...])


def _inproj(x, mods_l, w_bf, layer, row_base, rows_per_cond):
    m, d = x.shape
    n = w_bf.shape[2]
    tm = _tile(m, 1024, SUBLANE)
    tm = min(tm, rows_per_cond)
    tn = _tile(n, 1152)
    per = rows_per_cond // tm
    return pl.pallas_call(
        _inproj_kernel,
        grid=(m // tm, n // tn),
        in_specs=[pl.BlockSpec((tm, d), lambda i, j: (i, 0)),
                  pl.BlockSpec((1, 6, d), lambda i, j: (row_base + i // per, 0, 0)),
                  pl.BlockSpec((None, d, tn), lambda i, j: (layer, 0, j))],
        out_specs=pl.BlockSpec((tm, tn), lambda i, j: (i, j)),
        out_shape=jax.ShapeDtypeStruct((m, n), F32),
        scratch_shapes=[pltpu.VMEM((tm, d), BF16)],
        compiler_params=_cparams("arbitrary", "arbitrary"),
        name="inproj",
    )(x, mods_l, w_bf)


def _lambda_value(lam_ref, lam_init):
    lf = lam_ref[...]
    t1 = jnp.sum(lf[0:1] * lf[1:2], axis=1, keepdims=True)
    t2 = jnp.sum(lf[2:3] * lf[3:4], axis=1, keepdims=True)
    return jnp.exp(t1) - jnp.exp(t2) + lam_init


def _split_maps(q):
    lane = lax.broadcasted_iota(jnp.int32, q.shape, 1)
    first = lane < B_QK
    return jnp.where(first, q, 0.0), jnp.where(first, 0.0, q)


def _subnorm(o, g_ref, lam_init):
    ms = jnp.mean(o * o, axis=-1, keepdims=True)
    return o * lax.rsqrt(ms + RMS_EPS) * g_ref[...] * (1.0 - lam_init)


def _ctx_attn_kernel(*refs, lam_init):
    qa_ref, ka_ref, va_ref = refs[0:3]
    b_refs = refs[3:3 + 3 * B_HEADS]
    lam_ref, subln_ref = refs[3 + 3 * B_HEADS:5 + 3 * B_HEADS]
    ya_ref, yb_ref, nak_ref, nav_ref, nbk_ref, nbv_ref = refs[-6:]
    a_scale = HEAD_DIM ** -0.5
    b_scale = B_QK ** -0.5
    lam = _lambda_value(lam_ref, lam_init)
    head = lambda h: slice(h * HEAD_DIM, (h + 1) * HEAD_DIM)

    def a_scores(h):
        nak_ref[h] = ka_ref[:, head(h)]
        nav_ref[h] = va_ref[:, head(h)]
        return (_dot_nt(qa_ref[:, head(h)].astype(BF16), ka_ref[:, head(h)].astype(BF16)),)

    def a_finish(h, s):
        s = s * a_scale
        e = jnp.exp(s - jnp.max(s, axis=-1, keepdims=True))
        l = jnp.sum(e, axis=-1, keepdims=True)
        ya_ref[:, head(h)] = (_dot(e.astype(BF16), va_ref[:, head(h)].astype(BF16)) / l).astype(ya_ref.dtype)

    def b_scores(h):
        q1, q2 = _split_maps(b_refs[3 * h][...] * b_scale)
        nbk_ref[h] = b_refs[3 * h + 1][...]
        nbv_ref[h] = b_refs[3 * h + 2][...]
        k = b_refs[3 * h + 1][...].astype(BF16)
        return _dot_nt(q1.astype(BF16), k), _dot_nt(q2.astype(BF16), k)

    def b_finish(h, s1, s2):
        e1 = jnp.exp(s1 - jnp.max(s1, axis=-1, keepdims=True))
        e2 = jnp.exp(s2 - jnp.max(s2, axis=-1, keepdims=True))
        p = e1 / jnp.sum(e1, axis=-1, keepdims=True) - lam * (e2 / jnp.sum(e2, axis=-1, keepdims=True))
        o = _dot(p.astype(BF16), b_refs[3 * h + 2][...].astype(BF16))
        yb_ref[:, head(h)] = _subnorm(o, subln_ref, lam_init).astype(yb_ref.dtype)

    stages = [(a_scores, a_finish, h) for h in range(A_HEADS)] + [(b_scores, b_finish, h) for h in range(B_HEADS)]
    ahead = 2
    queue = [stages[j][0](stages[j][2]) for j in range(ahead)]
    for idx, (_, finish, h) in enumerate(stages):
        if idx + ahead < len(stages):
            queue.append(stages[idx + ahead][0](stages[idx + ahead][2]))
        finish(h, *queue.pop(0))


def _ctx_attn(p, batch, seq, b_lambda_l, b_subln_l, lam_init, layer, depth, carried):
    m = p.shape[0]
    aw = A_HEADS * HEAD_DIM
    nb = aw // HEAD_DIM
    in_specs = [pl.BlockSpec((seq, aw), lambda b: (b, 0)),
                pl.BlockSpec((seq, aw), lambda b: (b, 1)),
                pl.BlockSpec((seq, aw), lambda b: (b, 2))]
    args = [p, p, p]
    for h in range(B_HEADS):
        for seg in range(3):
            col = 3 * nb + seg * B_HEADS + h
            in_specs.append(pl.BlockSpec((seq, HEAD_DIM), functools.partial(lambda b, c: (b, c), c=col)))
            args.append(p)
    in_specs += [pl.BlockSpec((4, B_QK), lambda b: (0, 0)), pl.BlockSpec((1, HEAD_DIM), lambda b: (0, 0))]
    args += [b_lambda_l, b_subln_l.reshape(1, HEAD_DIM)]
    aliases = {}
    if carried is not None:
        for j, buf in enumerate(carried):
            aliases[len(args)] = 2 + j
            in_specs.append(pl.BlockSpec(memory_space=pl.ANY))
            args.append(buf)
    kv_spec = lambda nh: pl.BlockSpec((None, None, nh, seq, HEAD_DIM), lambda b: (b, layer, 0, 0, 0))
    kv_shape = lambda nh: jax.ShapeDtypeStruct((batch, depth, nh, seq, HEAD_DIM), F32)
    out = pl.pallas_call(
        functools.partial(_ctx_attn_kernel, lam_init=lam_init),
        grid=(batch,),
        in_specs=in_specs,
        out_specs=[pl.BlockSpec((seq, aw), lambda b: (b, 0)),
                   pl.BlockSpec((seq, B_HEADS * HEAD_DIM), lambda b: (b, 0)),
                   kv_spec(A_HEADS), kv_spec(A_HEADS), kv_spec(B_HEADS), kv_spec(B_HEADS)],
        out_shape=[jax.ShapeDtypeStruct((m, aw), BF16),
                   jax.ShapeDtypeStruct((m, B_HEADS * HEAD_DIM), BF16),
                   kv_shape(A_HEADS), kv_shape(A_HEADS), kv_shape(B_HEADS), kv_shape(B_HEADS)],
        input_output_aliases=aliases,
        compiler_params=_cparams("arbitrary"),
        name="ctx_attn",
    )(*args)
    return out[0], out[1], tuple(out[2:])


def _na_plan(n_rows):
    wr = min(NA_ROWS, n_rows)
    qrows = NA_QROWS
    uw = -(-(wr + qrows - 1) // 2) * 2
    if n_rows % qrows or n_rows < uw:
        qrows, uw = 1, wr
    rs = lambda r: int(np.clip(r - wr // 2, 0, n_rows - wr))
    starts, var_of_block, variants = [], [], []
    for blk in range(n_rows // qrows):
        q0 = blk * qrows
        ws = int(np.clip(rs(q0), 0, n_rows - uw))
        valid = tuple(tuple(rs(q0 + i) <= ws + j < rs(q0 + i) + wr for j in range(uw)) for i in range(qrows))
        var = (q0 - ws, valid)
        if var not in variants:
            variants.append(var)
        starts.append(ws)
        var_of_block.append(variants.index(var))
    return qrows, uw, starts, var_of_block, variants


def _na_bias_kernel(toe_ref, o_ref, *, n_rows):
    qrows, uw, _, _, variants = _na_plan(n_rows)
    w = GRID_W
    neg = jnp.full((w, w), -jnp.inf, F32)
    for v, (off, valid) in enumerate(variants):
        for i in range(qrows):
            for j in range(uw):
                dr = NA_ROWS - 1 + j - off - i
                tile = toe_ref[dr] if valid[i][j] else neg
                o_ref[v, i * w:(i + 1) * w, j * w:(j + 1) * w] = tile


def _na_bias_table(a_rpb, n_rows):
    depth, heads, nr, _ = a_rpb.shape
    qrows, uw, _, _, variants = _na_plan(n_rows)
    cols = np.arange(GRID_W)
    cs = np.clip(cols - NA_COLS // 2, 0, GRID_W - NA_COLS)
    col_mask = (cols[None, :] >= cs[:, None]) & (cols[None, :] < cs[:, None] + NA_COLS)
    col_idx = np.clip(cols[None, :] - cols[:, None] + NA_COLS - 1, 0, 2 * NA_COLS - 2)
    onehot = (col_idx[None] == np.arange(2 * NA_COLS - 1)[:, None, None]).astype(np.float32)
    toe = jnp.einsum('lhrd,dqk->lhrqk', a_rpb.astype(F32), jnp.asarray(onehot), precision=HI)
    toe = jnp.where(jnp.asarray(col_mask), toe, -jnp.inf)
    shape = (len(variants), qrows * GRID_W, uw * GRID_W)
    return pl.pallas_call(
        functools.partial(_na_bias_kernel, n_rows=n_rows),
        grid=(depth, heads),
        in_specs=[pl.BlockSpec((None, None, nr, GRID_W, GRID_W), lambda l, h: (l, h, 0, 0, 0))],
        out_specs=pl.BlockSpec((None, None) + shape, lambda l, h: (l, h, 0, 0, 0)),
        out_shape=jax.ShapeDtypeStruct((depth, heads) + shape, F32),
        compiler_params=_cparams("arbitrary", "arbitrary"),
        name="na_bias",
    )(toe)


def _select_by_block(blk, values):
    out = jnp.int32(values[0])
    for k in range(1, len(values)):
        out = jnp.where(blk == k, jnp.int32(values[k]), out)
    return out


def _na_kernel(q_ref, k_ref, v_ref, kc_ref, vc_ref, bias_ref, o_ref, kb_scr, vb_scr, kcb_scr, vcb_scr, *, n_rows):
    scale = HEAD_DIM ** -0.5
    qrows, uw, starts, var_of_block, _ = _na_plan(n_rows)
    nq = qrows * GRID_W
    kb_scr[...] = k_ref[...].astype(BF16)
    vb_scr[...] = v_ref[...].astype(BF16)
    kcb_scr[...] = kc_ref[...].astype(BF16)
    vcb_scr[...] = vc_ref[...].astype(BF16)

    def scores(blk):
        q0 = pl.multiple_of(blk * nq, nq)
        q = q_ref[pl.ds(q0, nq), :].astype(BF16)
        w0 = pl.multiple_of(_select_by_block(blk, starts) * GRID_W, GRID_W)
        s_nb = _dot_nt(q, kb_scr[pl.ds(w0, uw * GRID_W), :])
        s_c = _dot_nt(q, kcb_scr[...])
        return q0, w0, s_nb, s_c

    def finish(blk, q0, w0, s_nb, s_c):
        s_nb = s_nb * scale + bias_ref[_select_by_block(blk, var_of_block)]
        s_c = s_c * scale
        mx = jnp.maximum(jnp.max(s_nb, axis=-1, keepdims=True), jnp.max(s_c, axis=-1, keepdims=True))
        p_nb = jnp.exp(s_nb - mx)
        p_c = jnp.exp(s_c - mx)
        l = jnp.sum(p_nb, axis=-1, keepdims=True) + jnp.sum(p_c, axis=-1, keepdims=True)
        vw = vb_scr[pl.ds(w0, uw * GRID_W), :]
        o = (_dot(p_nb.astype(BF16), vw) + _dot(p_c.astype(BF16), vcb_scr[...])) / l
        o_ref[pl.ds(q0, nq), :] = o.astype(o_ref.dtype)

    n_blocks = n_rows // qrows
    group = 4 if n_blocks % 4 == 0 else (2 if n_blocks % 2 == 0 else 1)

    def body(j, carry):
        blks = [j * group + g for g in range(group)]
        pending = [scores(b) for b in blks]
        for b, s in zip(blks, pending):
            finish(b, *s)
        return carry

    lax.fori_loop(0, n_blocks // group, body, 0)


def _na_attn(p, batch, seq, cache_k, cache_v, layer, bias):
    n_rows = seq // GRID_W
    past = cache_k.shape[3]
    blk = lambda off: pl.BlockSpec((seq, HEAD_DIM), lambda b, h: (b, off + h))
    cspec = pl.BlockSpec((None, None, None, past, HEAD_DIM), lambda b, h: (b, layer, h, 0, 0))
    return pl.pallas_call(
        functools.partial(_na_kernel, n_rows=n_rows),
        grid=(batch, A_HEADS),
        in_specs=[blk(0), blk(A_HEADS), blk(2 * A_HEADS), cspec, cspec,
                  pl.BlockSpec((None, None) + bias.shape[2:], lambda b, h: (layer, h, 0, 0, 0))],
        out_specs=pl.BlockSpec((seq, HEAD_DIM), lambda b, h: (b, h)),
        out_shape=jax.ShapeDtypeStruct((batch * seq, A_HEADS * HEAD_DIM), BF16),
        scratch_shapes=[pltpu.VMEM((seq, HEAD_DIM), BF16), pltpu.VMEM((seq, HEAD_DIM), BF16),
                        pltpu.VMEM((past, HEAD_DIM), BF16), pltpu.VMEM((past, HEAD_DIM), BF16)],
        compiler_params=_cparams("arbitrary", "arbitrary"),
        name="na_attn",
    )(p, p, p, cache_k, cache_v, bias)


def _rope_tables(seq):
    t = np.arange(seq)
    half = B_QK // 2
    freqs = ROPE_BASE ** (-np.arange(0, half, 2, dtype=np.float64) / half)
    ar = (t // GRID_W)[:, None] * freqs
    ac = (t % GRID_W)[:, None] * freqs
    cr, sr, cc, sc = np.cos(ar), np.sin(ar), np.cos(ac), np.sin(ac)
    cos = np.concatenate([cr, cr, cc, cc] * 2, axis=-1).astype(np.float32)
    sin = np.concatenate([-sr, sr, -sc, sc] * 2, axis=-1).astype(np.float32)
    return jnp.asarray(cos), jnp.asarray(sin)


def _rope(x, cos, sin):
    q = B_QK // 4
    lane = lax.broadcasted_iota(jnp.int32, x.shape, 1)
    first = (lane & (2 * q - 1)) < q
    partner = jnp.where(first, pltpu.roll(x, LANE - q, 1), pltpu.roll(x, q, 1))
    return x * cos + partner * sin


def _diff_lat_kernel(q_ref, k_ref, v_ref, kc_ref, vc_ref, cosq_ref, sinq_ref, cosk_ref, sink_ref, lam_ref, subln_ref,
                     o_ref, kr_scr, kcb_scr, vb_scr, vcb_scr, *, lam_init):
    scale = B_QK ** -0.5

    @pl.when(pl.program_id(2) == 0)
    def _():
        kr_scr[...] = _rope(k_ref[...], cosk_ref[...], sink_ref[...]).astype(BF16)
        kcb_scr[...] = kc_ref[...].astype(BF16)
        vb_scr[...] = v_ref[...].astype(BF16)
        vcb_scr[...] = vc_ref[...].astype(BF16)

    lam = _lambda_value(lam_ref, lam_init)
    q = q_ref[...] * scale
    qr1, qr2 = _split_maps(_rope(q, cosq_ref[...], sinq_ref[...]))
    q1, q2 = _split_maps(q)

    def probs(qr, qp):
        s_l = _dot_nt(qr.astype(BF16), kr_scr[...])
        s_c = _dot_nt(qp.astype(BF16), kcb_scr[...])
        mx = jnp.maximum(jnp.max(s_l, axis=-1, keepdims=True), jnp.max(s_c, axis=-1, keepdims=True))
        e_l = jnp.exp(s_l - mx)
        e_c = jnp.exp(s_c - mx)
        inv = 1.0 / (jnp.sum(e_l, axis=-1, keepdims=True) + jnp.sum(e_c, axis=-1, keepdims=True))
        return e_l * inv, e_c * inv

    p1_l, p1_c = probs(qr1, q1)
    p2_l, p2_c = probs(qr2, q2)
    o = (_dot((p1_l - lam * p2_l).astype(BF16), vb_scr[...])
         + _dot((p1_c - lam * p2_c).astype(BF16), vcb_scr[...]))
    o_ref[...] = _subnorm(o, subln_ref, lam_init).astype(o_ref.dtype)


def _diff_lat_attn(p, batch, seq, cache_k, cache_v, layer, cos, sin, b_lambda_l, b_subln_l, lam_init):
    past = cache_k.shape[3]
    tq = _tile(seq, 256, SUBLANE)
    nq = seq // tq
    base = 3 * A_HEADS
    cspec = pl.BlockSpec((None, None, None, past, HEAD_DIM), lambda b, h, i: (b, layer, h, 0, 0))
    kv = lambda off: pl.BlockSpec((seq, HEAD_DIM), lambda b, h, i: (b, off + h))
    return pl.pallas_call(
        functools.partial(_diff_lat_kernel, lam_init=lam_init),
        grid=(batch, B_HEADS, nq),
        in_specs=[pl.BlockSpec((tq, HEAD_DIM), lambda b, h, i: (b * nq + i, base + h)),
                  kv(base + B_HEADS), kv(base + 2 * B_HEADS), cspec, cspec,
                  pl.BlockSpec((tq, HEAD_DIM), lambda b, h, i: (i, 0)),
                  pl.BlockSpec((tq, HEAD_DIM), lambda b, h, i: (i, 0)),
                  pl.BlockSpec((seq, HEAD_DIM), lambda b, h, i: (0, 0)),
                  pl.BlockSpec((seq, HEAD_DIM), lambda b, h, i: (0, 0)),
                  pl.BlockSpec((4, B_QK), lambda b, h, i: (0, 0)),
                  pl.BlockSpec((1, HEAD_DIM), lambda b, h, i: (0, 0))],
        out_specs=pl.BlockSpec((tq, HEAD_DIM), lambda b, h, i: (b * nq + i, h)),
        out_shape=jax.ShapeDtypeStruct((batch * seq, B_HEADS * HEAD_DIM), BF16),
        scratch_shapes=[pltpu.VMEM((seq, HEAD_DIM), BF16), pltpu.VMEM((past, HEAD_DIM), BF16),
                        pltpu.VMEM((seq, HEAD_DIM), BF16), pltpu.VMEM((past, HEAD_DIM), BF16)],
        compiler_params=_cparams("arbitrary", "arbitrary", "arbitrary"),
        name="diff_lat_attn",
    )(p, p, p, cache_k, cache_v, cos, sin, cos, sin, b_lambda_l, b_subln_l.reshape(1, HEAD_DIM))


def _log_sigmoid(x):
    return jnp.minimum(x, 0.0) - jnp.log1p(jnp.exp(-jnp.abs(x)))


def _split3(x):
    hi = x.astype(BF16)
    r = x - hi.astype(F32)
    mid = r.astype(BF16)
    lo = (r - mid.astype(F32)).astype(BF16)
    return hi, mid, lo


def _mlstm_kernel(*refs, chunk, n_chunks, zero_init):
    nh = C_HEADS
    q_refs, k_refs, v_refs, og_refs = (refs[i * nh:(i + 1) * nh] for i in range(4))
    g_ref, gb_ref, cn_ref = refs[4 * nh:4 * nh + 3]
    if not zero_init:
        c0_ref, n0_ref, m0_ref = refs[4 * nh + 3:4 * nh + 6]
    y_ref, cf_ref, nf_ref, mf_ref, hf_scr, hb_scr = refs[-6:]
    scale = HEAD_DIM ** -0.5
    ln = chunk
    n_sel = 4 * SUBLANE
    assert 4 * nh <= n_sel
    sel = jnp.where(lax.broadcasted_iota(jnp.int32, (n_sel, LANE), 0)
                    == lax.broadcasted_iota(jnp.int32, (n_sel, LANE), 1), 1.0, 0.0).astype(BF16)
    ti = lax.broadcasted_iota(jnp.int32, (ln, ln), 0)
    si = lax.broadcasted_iota(jnp.int32, (ln, ln), 1)
    lower = si <= ti
    upper = si >= ti
    lower_b = jnp.where(lower, 1.0, 0.0).astype(BF16)
    upper_b = jnp.where(upper, 1.0, 0.0).astype(BF16)

    def gate_terms(r0, directions):
        g = g_ref[pl.ds(r0, ln), :] + gb_ref[...]
        gp = _split3(g)
        rows = sum(_dot_nt(sel, p) for p in gp)
        cols = sum(_dot_nt(p, sel) for p in gp)
        lf_rows = _split3(_log_sigmoid(rows))
        lf_cols = _split3(_log_sigmoid(cols))
        t = dict(rows=rows, cols=cols)
        if 0 in directions:
            t["row0"] = sum(_dot(p, upper_b) for p in lf_rows)
            t["col0"] = sum(_dot(lower_b, p) for p in lf_cols)
        if 1 in directions:
            t["row1"] = sum(_dot(p, lower_b) for p in lf_rows)
            t["col1"] = sum(_dot(upper_b, p) for p in lf_cols)
        return t

    def head_terms(h, r0):
        qf = q_refs[h][pl.ds(r0, ln), :] * scale
        kf = k_refs[h][pl.ds(r0, ln), :]
        qb = qf.astype(BF16)
        vb = v_refs[h][pl.ds(r0, ln), :].astype(BF16)
        return dict(qf=qf, kf=kf, qb=qb, vb=vb, qk=_dot_nt(qb, kf.astype(BF16)))

    def chunk_step(gt, t, h, state, backward):
        rows, cols, qf, kf, qb, vb = gt["rows"], gt["cols"], t["qf"], t["kf"], t["qb"], t["vb"]
        gi = (2 * nh if backward else 0) + h
        gf = gi + nh
        i_row, i_col = rows[gi:gi + 1], cols[:, gi:gi + 1]
        if backward:
            b_row, b_col, mask = gt["row1"][gf:gf + 1], gt["col1"][:, gf:gf + 1], upper
            b_last = b_col[0:1]
        else:
            b_row, b_col, mask = gt["row0"][gf:gf + 1], gt["col0"][:, gf:gf + 1], lower
            b_last = b_col[ln - 1:ln]
        d = jnp.where(mask, b_col - b_row + i_row, -jnp.inf)
        gg = b_last - b_col + i_col
        if state is None:
            m_row = jnp.maximum(jnp.max(d, axis=-1, keepdims=True), b_col)
            sc = t["qk"] * jnp.exp(d - m_row)
            num = _dot(sc.astype(BF16), vb)
            den = jnp.sum(sc, axis=-1, keepdims=True)
            m_new = jnp.maximum(b_last, jnp.max(gg, axis=0, keepdims=True))
            kw = kf * jnp.exp(gg - m_new)
            c_new = _dot_tn(kw.astype(BF16), vb)
            n_new = jnp.sum(kw, axis=0, keepdims=True)
        else:
            c_st, n_st, m_st = state
            inter = b_col + m_st
            m_row = jnp.maximum(jnp.max(d, axis=-1, keepdims=True), inter)
            w_state = jnp.exp(inter - m_row)
            sc = t["qk"] * jnp.exp(d - m_row)
            num = w_state * _dot(qb, c_st.astype(BF16)) + _dot(sc.astype(BF16), vb)
            den = w_state * jnp.sum(qf * n_st, axis=-1, keepdims=True) + jnp.sum(sc, axis=-1, keepdims=True)
            m_new = jnp.maximum(b_last + m_st, jnp.max(gg, axis=0, keepdims=True))
            w_old = jnp.exp(b_last + m_st - m_new)
            kw = kf * jnp.exp(gg - m_new)
            c_new = w_old * c_st + _dot_tn(kw.astype(BF16), vb)
            n_new = w_old * n_st + jnp.sum(kw, axis=0, keepdims=True)
        h_out = num / jnp.maximum(jnp.abs(den), jnp.exp(-m_row))
        return h_out, (c_new, n_new, m_new)

    def start(c):
        return pl.multiple_of(c * ln, ln)

    def finish(h, r0, hs):
        ms = jnp.mean(hs * hs, axis=-1, keepdims=True)
        hn = hs * lax.rsqrt(ms + RMS_EPS) * cn_ref[...]
        og = og_refs[h][pl.ds(r0, ln), :]
        y_ref[pl.ds(r0, ln), h * HEAD_DIM:(h + 1) * HEAD_DIM] = (hn / (1.0 + jnp.exp(-og))).astype(y_ref.dtype)

    if zero_init:
        assert n_chunks == 1
        init = ((None,) * nh,) * 2
    else:
        init = tuple(tuple((c0_ref[dr, h], n0_ref[dr, h], m0_ref[dr, h][:, 0:1]) for h in range(nh))
                     for dr in (0, 1))
    if n_chunks == 1:
        gt = gate_terms(0, (0, 1))
        st_f, st_b = [], []
        for h in range(nh):
            t = head_terms(h, 0)
            h_f, s_f = chunk_step(gt, t, h, init[0][h], False)
            h_b, s_b = chunk_step(gt, t, h, init[1][h], True)
            finish(h, 0, h_f + h_b)
            st_f.append(s_f)
            st_b.append(s_b)
        final = (st_f, st_b)
    else:
        def body(j, states):
            r_f = start(j)
            r_b = start(n_chunks - 1 - j)
            gt_f = gate_terms(r_f, (0,))
            gt_b = gate_terms(r_b, (1,))
            terms_f = [head_terms(h, r_f) for h in range(nh)]
            terms_b = [head_terms(h, r_b) for h in range(nh)]
            st_f, st_b = [], []
            for h in range(nh):
                sl = slice(h * HEAD_DIM, (h + 1) * HEAD_DIM)
                h_f, s_f = chunk_step(gt_f, terms_f[h], h, states[0][h], False)
                h_b, s_b = chunk_step(gt_b, terms_b[h], h, states[1][h], True)
                hf_scr[pl.ds(r_f, ln), sl] = h_f
                hb_scr[pl.ds(r_b, ln), sl] = h_b
                st_f.append(s_f)
                st_b.append(s_b)
            return tuple(st_f), tuple(st_b)

        final = lax.fori_loop(0, n_chunks, body, init)

        def fin_body(j, carry):
            r0 = start(j)
            for h in range(nh):
                sl = slice(h * HEAD_DIM, (h + 1) * HEAD_DIM)
                finish(h, r0, hf_scr[pl.ds(r0, ln), sl] + hb_scr[pl.ds(r0, ln), sl])
            return carry

        lax.fori_loop(0, n_chunks, fin_body, 0)
    for dr in (0, 1):
        for h in range(nh):
            c_f, n_f, m_f = final[dr][h]
            cf_ref[dr, h] = c_f
            nf_ref[dr, h] = n_f
            mf_ref[dr, h] = jnp.broadcast_to(m_f, (1, LANE))


def _mlstm(p, batch, seq, gate_bias, c_norm_l, c0, n0, m0, lin, out_depth, lout, carried):
    chunk = min(MLSTM_CHUNK, seq)
    n_chunks = seq // chunk
    base = 3 * A_HEADS + 3 * B_HEADS
    hd = HEAD_DIM
    nh = C_HEADS
    head_bytes = 4 * nh * seq * hd * 4
    mode = dict(pipeline_mode=pl.Buffered(1)) if 2 * head_bytes > VMEM_LIMIT_BYTES // 2 else {}
    blk = lambda col: pl.BlockSpec((seq, hd), functools.partial(lambda b, c: (b, c), c=col), **mode)
    st_c = lambda l: pl.BlockSpec((None, None, 2, nh, hd, hd), lambda b: (b, l, 0, 0, 0, 0))
    st_n = lambda l: pl.BlockSpec((None, None, 2, nh, 1, hd), lambda b: (b, l, 0, 0, 0, 0))
    if c0 is None and n_chunks > 1:
        c0 = jnp.zeros((batch, 1, 2, nh, hd, hd), F32)
        n0 = jnp.zeros((batch, 1, 2, nh, hd), F32)
        m0 = jnp.zeros((batch, 1, 2, nh), F32)
    zero_init = c0 is None
    in_specs = ([blk(base + seg * nh + h) for seg in range(4) for h in range(nh)]
                + [pl.BlockSpec((seq, LANE), lambda b: (b, base + 4 * nh)),
                   pl.BlockSpec((1, LANE), lambda b: (0, 0)),
                   pl.BlockSpec((1, hd), lambda b: (0, 0))])
    args = [p] * (4 * nh + 1) + [gate_bias, c_norm_l.reshape(1, hd)]
    if not zero_init:
        in_specs += [st_c(lin), st_n(lin), st_n(lin)]
        args += [c0, n0.reshape(n0.shape[:4] + (1, hd)), jnp.broadcast_to(m0[..., None, None], m0.shape + (1, LANE))]
    aliases = {}
    if carried is not None:
        for j, buf in enumerate(carried):
            aliases[len(args)] = 1 + j
            in_specs.append(pl.BlockSpec(memory_space=pl.ANY))
            args.append(buf)
    out = pl.pallas_call(
        functools.partial(_mlstm_kernel, chunk=chunk, n_chunks=n_chunks, zero_init=zero_init),
        grid=(batch,),
        in_specs=in_specs,
        out_specs=[pl.BlockSpec((seq, nh * hd), lambda b: (b, 0)), st_c(lout), st_n(lout), st_n(lout)],
        out_shape=[jax.ShapeDtypeStruct((batch * seq, nh * hd), BF16),
                   jax.ShapeDtypeStruct((batch, out_depth, 2, nh, hd, hd), F32),
                   jax.ShapeDtypeStruct((batch, out_depth, 2, nh, 1, hd), F32),
                   jax.ShapeDtypeStruct((batch, out_depth, 2, nh, 1, LANE), F32)],
        scratch_shapes=[pltpu.VMEM((seq, nh * hd), F32), pltpu.VMEM((seq, nh * hd), F32)],
        input_output_aliases=aliases,
        compiler_params=_cparams("arbitrary"),
        name="mlstm",
    )(*args)
    return out[0], tuple(out[1:])


def _layernorm(z, g_ref, b_ref):
    mu = jnp.mean(z, axis=-1, keepdims=True)
    zc = z - mu
    var = jnp.mean(zc * zc, axis=-1, keepdims=True)
    return zc * lax.rsqrt(var + LN_EPS) * g_ref[...] + b_ref[...]


def _outproj_kernel(ya_ref, yb_ref, yc_ref, w_ref, x_ref, mod_ref, g_ref, b_ref, x1_ref, h2_ref):
    tm = x_ref.shape[0]
    rc = min(tm, OUT_ROW_CHUNK)

    def proj(c):
        rows = slice(c * rc, (c + 1) * rc)
        ycat = jnp.concatenate([ya_ref[rows, :], yb_ref[rows, :], yc_ref[rows, :]], axis=-1)
        return _dot(ycat, w_ref[...])

    def finish(c, y):
        rows = slice(c * rc, (c + 1) * rc)
        x1 = _layernorm(ALPHA * x_ref[rows, :] + mod_ref[0, 2:3, :] * y, g_ref, b_ref)
        x1_ref[rows, :] = x1
        h2_ref[rows, :] = (x1 * (1.0 + mod_ref[0, 4:5, :]) + mod_ref[0, 3:4, :]).astype(BF16)

    y = proj(0)
    for c in range(tm // rc):
        y_next = proj(c + 1) if c + 1 < tm // rc else None
        finish(c, y)
        y = y_next


def _outproj(ya, yb, yc, w_bf, layer, x, mods_l, ln_g, ln_b, row_base, rows_per_cond):
    m, d = x.shape
    tm = min(_tile(m, 512, SUBLANE), rows_per_cond)
    per = rows_per_cond // tm
    row = lambda w: pl.BlockSpec((tm, w), lambda i: (i, 0))
    vec = pl.BlockSpec((1, d), lambda i: (0, 0))
    return pl.pallas_call(
        _outproj_kernel,
        grid=(m // tm,),
        in_specs=[row(ya.shape[1]), row(yb.shape[1]), row(yc.shape[1]),
                  pl.BlockSpec((None, d, d), lambda i: (layer, 0, 0)), row(d),
                  pl.BlockSpec((1, 6, d), lambda i: (row_base + i // per, 0, 0)), vec, vec],
        out_specs=[row(d), row(d)],
        out_shape=[jax.ShapeDtypeStruct((m, d), F32), jax.ShapeDtypeStruct((m, d), BF16)],
        compiler_params=_cparams("arbitrary"),
        name="outproj_ln",
    )(ya, yb, yc, w_bf, x, mods_l, ln_g.reshape(1, d), ln_b.reshape(1, d))


def _ffn_kernel(h_ref, hp_ref, hn_ref, wa_ref, wg_ref, cwa_ref, cwg_ref, cba_ref, cbg_ref, wd_ref, x_ref, mod_ref,
                g_ref, b_ref, o_ref, hext_scr, acc_scr, ua_scr, ug_scr, *, seq):
    i = pl.program_id(0)
    f = pl.program_id(1)
    tm = h_ref.shape[0]
    halo = BF16_ROWS

    @pl.when(f == 0)
    def _():
        hext_scr[0:halo, :] = hp_ref[...]
        hext_scr[halo:halo + tm, :] = h_ref[...]
        hext_scr[halo + tm:, :] = hn_ref[...]
        acc_scr[...] = jnp.zeros_like(acc_scr)

    period = min(seq, tm)
    grp = lax.broadcasted_iota(jnp.int32, (SUBLANE, 1), 0)

    rc = min(tm, FFN_ROW_CHUNK)
    ext = tm + 2 * halo
    up_bounds = [0] + [min(ext, -(-(halo + (k + 1) * rc + 1) // BF16_ROWS) * BF16_ROWS) for k in range(tm // rc)]
    up_bounds[-1] = ext

    def up_chunk(k):
        lo, hi = up_bounds[k], up_bounds[k + 1]
        ua_scr[lo:hi, :] = _dot(hext_scr[lo:hi, :], wa_ref[...])
        ug_scr[lo:hi, :] = _dot(hext_scr[lo:hi, :], wg_ref[...])

    def mask_group(x, r0, first):
        g0 = 0 if first else rc - SUBLANE
        pos = (i * tm + r0 + g0 + grp) & (seq - 1)
        keep = (pos != 0) if first else (pos != seq - 1)
        masked = jnp.where(keep, x[g0:g0 + SUBLANE], 0.0)
        return jnp.concatenate([masked, x[SUBLANE:]] if first else [x[:g0], masked], axis=0)

    def down_chunk(k):
        r0 = k * rc

        def conv(u_scr, cw_ref, cb_ref):
            u_prev = u_scr[halo - 1 + r0:halo - 1 + r0 + rc, :]
            u_next = u_scr[halo + 1 + r0:halo + 1 + r0 + rc, :]
            if r0 % period == 0:
                u_prev = mask_group(u_prev, r0, True)
            if (r0 + rc) % period == 0:
                u_next = mask_group(u_next, r0, False)
            return (u_prev * cw_ref[0:1, :] + u_scr[halo + r0:halo + r0 + rc, :] * cw_ref[1:2, :]
                    + u_next * cw_ref[2:3, :] + cb_ref[...])

        a = conv(ua_scr, cwa_ref, cba_ref)
        g = conv(ug_scr, cwg_ref, cbg_ref)
        act = (g / (1.0 + jnp.exp2(g * NEG_LOG2_E))) * a
        acc_scr[r0:r0 + rc, :] += _dot(act.astype(BF16), wd_ref[...])

    up_chunk(0)
    for k in range(tm // rc):
        if k + 1 < tm // rc:
            up_chunk(k + 1)
        down_chunk(k)

    @pl.when(f == pl.num_programs(1) - 1)
    def _():
        z = ALPHA * x_ref[...] + mod_ref[0, 5:6, :] * acc_scr[...]
        o_ref[...] = _layernorm(z, g_ref, b_ref)


def _ffn(h2, x1, ffn_w, layer, mods_l, ln_g, ln_b, row_base, rows_per_cond, seq, tf):
    wu_bf, cw, cb, wd_bf = ffn_w
    m, d = x1.shape
    fp = wd_bf.shape[1]
    tm = min(_tile(m, 512, SUBLANE), rows_per_cond)
    per = rows_per_cond // tm
    assert seq & (seq - 1) == 0 and m % seq == 0
    nh = m // BF16_ROWS
    hb = tm // BF16_ROWS
    vec = pl.BlockSpec((1, d), lambda i, f: (0, 0))
    wcol = lambda rows, half: pl.BlockSpec((None, None, rows, tf),
                                           functools.partial(lambda i, f, hf: (layer, hf, 0, f), hf=half))
    return pl.pallas_call(
        functools.partial(_ffn_kernel, seq=seq),
        grid=(m // tm, fp // tf),
        in_specs=[pl.BlockSpec((tm, d), lambda i, f: (i, 0)),
                  pl.BlockSpec((BF16_ROWS, d), lambda i, f: (jnp.maximum(i * hb - 1, 0), 0)),
                  pl.BlockSpec((BF16_ROWS, d), lambda i, f: (jnp.minimum((i + 1) * hb, nh - 1), 0)),
                  wcol(d, 0), wcol(d, 1), wcol(CONV_W, 0), wcol(CONV_W, 1), wcol(1, 0), wcol(1, 1),
                  pl.BlockSpec((None, tf, d), lambda i, f: (layer, f, 0)),
                  pl.BlockSpec((tm, d), lambda i, f: (i, 0)),
                  pl.BlockSpec((1, 6, d), lambda i, f: (row_base + i // per, 0, 0)), vec, vec],
        out_specs=pl.BlockSpec((tm, d), lambda i, f: (i, 0)),
        out_shape=jax.ShapeDtypeStruct((m, d), F32),
        scratch_shapes=[pltpu.VMEM((tm + 2 * BF16_ROWS, d), BF16), pltpu.VMEM((tm, d), F32),
                        pltpu.VMEM((tm + 2 * BF16_ROWS, tf), F32), pltpu.VMEM((tm + 2 * BF16_ROWS, tf), F32)],
        compiler_params=_cparams("arbitrary", "arbitrary"),
        name="ffn_ln",
    )(h2, h2, h2, wu_bf, wu_bf, cw, cw, cb, cb, wd_bf, x1, mods_l, ln_g.reshape(1, d), ln_b.reshape(1, d))


def _pad_cols(w, n):
    return jnp.pad(w, [(0, 0)] * (w.ndim - 1) + [(0, n - w.shape[-1])])


def _cast_kernel(x_ref, o_ref, *, rows_valid, cols_valid):
    tr, tc = x_ref.shape
    x = x_ref[...]
    if rows_valid is not None:
        r = pl.program_id(1) * tr + lax.broadcasted_iota(jnp.int32, (tr, tc), 0)
        x = jnp.where(r < rows_valid, x, 0.0)
    if cols_valid is not None:
        c = pl.program_id(2) * tc + lax.broadcasted_iota(jnp.int32, (tr, tc), 1)
        x = jnp.where(c < cols_valid, x, 0.0)
    o_ref[:, :tc] = x.astype(BF16)
    if o_ref.shape[1] > tc:
        o_ref[:, tc:] = jnp.zeros((tr, o_ref.shape[1] - tc), BF16)


def _cast_weights(w, tr, tc, rows_out, cols_out):
    depth, r, c = w.shape
    nr, nc = -(-rows_out // tr), -(-cols_out // tc)
    assert nr * tr == rows_out and nc * tc == cols_out
    return pl.pallas_call(
        functools.partial(_cast_kernel, rows_valid=r if rows_out > r else None, cols_valid=c if cols_out > c else None),
        grid=(depth, nr, nc),
        in_specs=[pl.BlockSpec((None, tr, tc), lambda l, i, j: (l, i, j))],
        out_specs=pl.BlockSpec((None, tr, tc), lambda l, i, j: (l, i, j)),
        out_shape=jax.ShapeDtypeStruct((depth, rows_out, cols_out), BF16),
        compiler_params=_cparams("arbitrary", "arbitrary", "arbitrary"),
        name="cast_weights",
    )(w)


def _cast_in_kernel(x_ref, o_ref, *, n_valid):
    tn = x_ref.shape[0]
    col = pl.program_id(0) * tn + lax.broadcasted_iota(jnp.int32, (o_ref.shape[1], tn), 1)
    for l in range(x_ref.shape[1]):
        o_ref[l] = jnp.where(col < n_valid, x_ref[:, l, :].T, 0.0).astype(BF16)


def _cast_in_weights(w_in, n_pad, tn):
    depth, d, n = w_in.shape
    return pl.pallas_call(
        functools.partial(_cast_in_kernel, n_valid=n),
        grid=(n_pad // tn,),
        in_specs=[pl.BlockSpec((tn, depth, d), lambda j: (j, 0, 0))],
        out_specs=pl.BlockSpec((depth, d, tn), lambda j: (0, 0, j)),
        out_shape=jax.ShapeDtypeStruct((depth, d, n_pad), BF16),
        compiler_params=_cparams("arbitrary"),
        name="cast_in_weights",
    )(jnp.transpose(w_in, (2, 0, 1)))


def _cast_up_weights(w_up, dff, fp, tr):
    depth, d, _ = w_up.shape
    return pl.pallas_call(
        functools.partial(_cast_kernel, rows_valid=None, cols_valid=None),
        grid=(depth, d // tr, 2),
        in_specs=[pl.BlockSpec((None, tr, dff), lambda l, i, j: (l, i, j))],
        out_specs=pl.BlockSpec((None, None, tr, fp), lambda l, i, j: (l, j, i, 0)),
        out_shape=jax.ShapeDtypeStruct((depth, 2, d, fp), BF16),
        compiler_params=_cparams("arbitrary", "arbitrary", "arbitrary"),
        name="cast_up_weights",
    )(w_up)


def _prep_weights(w_in, w_out, w_up, conv_w, conv_b, w_down, tf):
    depth, d, n_in = w_in.shape
    n_pad = -(-n_in // LANE) * LANE
    dff = w_down.shape[1]
    fp = -(-dff // tf) * tf
    w_in_bf = _cast_in_weights(w_in, n_pad, _tile(n_pad, 384))
    w_out_bf = _cast_weights(w_out, _tile(d, 1024, SUBLANE), d, d, d)
    w_up_bf = _cast_up_weights(w_up, dff, fp, _tile(d, 512, SUBLANE))
    w_down_bf = _cast_weights(w_down, _tile(fp, 1536, tf), d, fp, d)
    halves = lambda t: jnp.stack([_pad_cols(t[..., :dff], fp), _pad_cols(t[..., dff:], fp)], axis=1)
    return w_in_bf, w_out_bf, w_up_bf, halves(conv_w), halves(conv_b[:, None, :]), w_down_bf


def _gate_bias_row(c_gate_b_l):
    return _pad_cols(c_gate_b_l.reshape(1, -1), LANE)


def kernel(x_prompt, x_sample, cache_a_k, cache_a_v, cache_b_k, cache_b_v, state_c_C, state_c_n, state_c_m, c, c_ctx, w_mod, b_mod, w_in, c_gate_b, a_rpb, b_lambda, b_subln, c_norm, w_out, ln1_g, ln1_b, ln2_g, ln2_b, w_up, conv_w, conv_b, w_down):
    batch, seq, d = x_prompt.shape
    dec_batch, dec_seq, _ = x_sample.shape
    depth = w_in.shape[0]
    tf = 512

    cond = jnp.concatenate([c_ctx[None, :], c], axis=0)
    cond = jnp.pad(cond, ((0, SUBLANE - cond.shape[0]), (0, 0)))
    mods = _mods(cond, w_mod, b_mod).reshape(depth, SUBLANE, 6, d)

    xp = x_prompt.reshape(batch * seq, d)
    xs = x_sample.reshape(dec_batch * dec_seq, d)
    cos, sin = _rope_tables(dec_seq)
    na_bias = _na_bias_table(a_rpb, dec_seq // GRID_W)

    w_in_bf, w_out_bf, *ffn_w = _prep_weights(w_in, w_out, w_up, conv_w, conv_b, w_down, tf)
    new_kv = None
    new_state = None
    for l in range(depth):
        lam_init = 0.8 - 0.6 * math.exp(-0.3 * l)
        gate_bias = _gate_bias_row(c_gate_b[l])

        pp = _inproj(xp, mods[l], w_in_bf, l, 0, batch * seq)
        ya, yb, new_kv = _ctx_attn(pp, batch, seq, b_lambda[l], b_subln[l], lam_init, l, depth, new_kv)
        yc, new_state = _mlstm(pp, batch, seq, gate_bias, c_norm[l], None, None, None, 0, depth, l, new_state)
        x1, h2 = _outproj(ya, yb, yc, w_out_bf, l, xp, mods[l], ln1_g[l], ln1_b[l], 0, batch * seq)
        xp = _ffn(h2, x1, ffn_w, l, mods[l], ln2_g[l], ln2_b[l], 0, batch * seq, seq, tf)

        ps = _inproj(xs, mods[l], w_in_bf, l, 1, dec_seq)
        ya = _na_attn(ps, dec_batch, dec_seq, cache_a_k, cache_a_v, l, na_bias)
        yb = _diff_lat_attn(ps, dec_batch, dec_seq, cache_b_k, cache_b_v, l, cos, sin, b_lambda[l], b_subln[l], lam_init)
        yc, _ = _mlstm(ps, dec_batch, dec_seq, gate_bias, c_norm[l], state_c_C, state_c_n, state_c_m, l, 1, 0, None)
        x1, h2 = _outproj(ya, yb, yc, w_out_bf, l, xs, mods[l], ln1_g[l], ln1_b[l], 1, dec_seq)
        xs = _ffn(h2, x1, ffn_w, l, mods[l], ln2_g[l], ln2_b[l], 1, dec_seq, dec_seq, tf)

    c_f, n_f, m_f = new_state
    return (xp.reshape(batch, seq, d), xs.reshape(dec_batch, dec_seq, d), *new_kv,
            c_f, n_f[..., 0, :], m_f[..., 0, 0])
```
